```python
import jax, jax.numpy as jnp
from jax import lax
import numpy as np

D_MODEL = 2048
BATCH = 4
SEQ = 2048
DEPTH = 4
DEC_BATCH = 32
DEC_SEQ = 4
PAST_LEN = 16384
PAGE_SIZE = 128

ATTN_WIDTH = D_MODEL // 2
HG_WIDTH = D_MODEL // 4
CONV_WIDTH = D_MODEL // 4
MIX_WIDTH = ATTN_WIDTH + HG_WIDTH + CONV_WIDTH

HEAD_DIM = 64
N_Q_HEADS = ATTN_WIDTH // HEAD_DIM
N_KV_HEADS = N_Q_HEADS // 4
Q_PER_KV = N_Q_HEADS // N_KV_HEADS
KV_WIDTH = N_KV_HEADS * HEAD_DIM
WINDOW = 128
ROPE_THETA = 500000.0
ROT_DIM = HEAD_DIM // 4
ATTN_SCALE = HEAD_DIM ** -0.5
NEG_BIG = -1e30

HG_DV = 128
HG_HEADS = HG_WIDTH // HG_DV
HG_DK = 128
HG_KEY = HG_HEADS * HG_DK
HG_CHUNK = 64
LB_TINY = 1e-30

CONV_K = 3

D_FF = -(-8 * D_MODEL // (3 * 256)) * 256

NORM_EPS = 1e-6
IN_SIZES = (ATTN_WIDTH, KV_WIDTH, KV_WIDTH,
            HG_KEY, HG_KEY, HG_WIDTH, HG_WIDTH,
            CONV_WIDTH, CONV_WIDTH, CONV_WIDTH)
IN_WIDTH = sum(IN_SIZES)

kernel_name = "hymba_swa_hgrn2_shortconv_decode_step"


def rms_norm(x, g):
    xf = x.astype(jnp.float32)
    y = xf * lax.rsqrt(jnp.mean(xf * xf, axis=-1, keepdims=True) + NORM_EPS)
    return (y * g.astype(jnp.float32)).astype(x.dtype)


def partial_rope(x, pos):
    half = ROT_DIM // 2
    inv = ROPE_THETA ** (-jnp.arange(half, dtype=jnp.float32) * 2.0 / ROT_DIM)
    ang = pos.astype(jnp.float32)[:, None] * inv[None, :]
    cos = jnp.cos(ang)[:, None, :]
    sin = jnp.sin(ang)[:, None, :]
    xr = x[..., :ROT_DIM].astype(jnp.float32)
    x1, x2 = xr[..., :half], xr[..., half:]
    rot = jnp.concatenate([x1 * cos - x2 * sin, x2 * cos + x1 * sin], axis=-1)
    return jnp.concatenate([rot.astype(x.dtype), x[..., ROT_DIM:]], axis=-1)


def sink_attend(q, k, v, mask, sink):
    s = jnp.einsum('bnqhgd,bnshd->bnhgqs', q, k).astype(jnp.float32) * ATTN_SCALE
    s = jnp.where(mask[None, :, None, None], s, NEG_BIG)
    sk = sink.astype(jnp.float32).reshape(N_KV_HEADS, Q_PER_KV)[None, None, :, :, None, None]
    m = jnp.maximum(jnp.max(s, axis=-1, keepdims=True), sk)
    p = jnp.exp(s - m)
    p = p / (jnp.sum(p, axis=-1, keepdims=True) + jnp.exp(sk - m))
    return jnp.einsum('bnhgqs,bnshd->bnqhgd', p.astype(v.dtype), v)


def swa_prompt(q, k, v, sink):
    B, L, _ = q.shape
    C = WINDOW
    N = L // C
    pos = jnp.arange(L, dtype=jnp.int32)
    q = partial_rope(q.reshape(B, L, N_Q_HEADS, HEAD_DIM), pos)
    k = partial_rope(k.reshape(B, L, N_KV_HEADS, HEAD_DIM), pos)
    v = v.reshape(B, L, N_KV_HEADS, HEAD_DIM)
    qb = q.reshape(B, N, C, N_KV_HEADS, Q_PER_KV, HEAD_DIM)
    kb = k.reshape(B, N, C, N_KV_HEADS, HEAD_DIM)
    vb = v.reshape(B, N, C, N_KV_HEADS, HEAD_DIM)
    pad = ((0, 0), (1, 0), (0, 0), (0, 0), (0, 0))
    kk = jnp.concatenate([jnp.pad(kb, pad)[:, :-1], kb], axis=2)
    vv = jnp.concatenate([jnp.pad(vb, pad)[:, :-1], vb], axis=2)
    blk = jnp.arange(N, dtype=jnp.int32)[:, None] * C
    qpos = blk + jnp.arange(C, dtype=jnp.int32)[None, :]
    kpos = blk - C + jnp.arange(2 * C, dtype=jnp.int32)[None, :]
    diff = qpos[:, :, None] - kpos[:, None, :]
    mask = (diff >= 0) & (diff <= WINDOW) & (kpos[:, None, :] >= 0)
    o = sink_attend(qb, kk, vv, mask, sink).reshape(B, L, ATTN_WIDTH)
    keep = min(WINDOW, L)
    return o, k[:, L - keep:], v[:, L - keep:]


def swa_sample(q, k, v, k_buf, v_buf, sink):
    B, L, _ = q.shape
    Lb = k_buf.shape[1]
    pos = PAST_LEN + jnp.arange(L, dtype=jnp.int32)
    q = partial_rope(q.reshape(B, L, N_Q_HEADS, HEAD_DIM), pos)
    k = partial_rope(k.reshape(B, L, N_KV_HEADS, HEAD_DIM), pos)
    v = v.reshape(B, L, N_KV_HEADS, HEAD_DIM)
    kk = jnp.concatenate([k_buf.astype(k.dtype), k], axis=1)
    vv = jnp.concatenate([v_buf.astype(v.dtype), v], axis=1)
    kpos = jnp.concatenate([PAST_LEN - Lb + jnp.arange(Lb, dtype=jnp.int32), pos])
    diff = pos[:, None] - kpos[None, :]
    mask = (diff >= 0) & (diff <= WINDOW)
    qb = q.reshape(B, 1, L, N_KV_HEADS, Q_PER_KV, HEAD_DIM)
    o = sink_attend(qb, kk[:, None], vv[:, None], mask[None], sink).reshape(B, L, ATTN_WIDTH)
    return o, kk[:, -Lb:], vv[:, -Lb:]


def hgrn2_chunked(q, log_f, k, v, S0):
    B, L, H, DK = q.shape
    c = HG_CHUNK if L % HG_CHUNK == 0 else L
    n = L // c

    def chunks(a):
        return jnp.swapaxes(a.reshape(B, n, c, H, a.shape[-1]), 0, 1)

    causal = jnp.tril(jnp.ones((c, c), dtype=bool))[None, :, :, None, None]

    def step(S, inp):
        qc, gc, kc, vc = inp
        G = jnp.cumsum(gc, axis=1)
        G_last = G[:, -1]
        o_inter = jnp.einsum('bthd,bhde->bthe', qc * jnp.exp(G), S)
        decay = jnp.exp(jnp.where(causal, G[:, :, None] - G[:, None, :], NEG_BIG))
        A = jnp.einsum('bthd,btshd->btsh', qc, decay * kc[:, None])
        o_intra = jnp.einsum('btsh,bshe->bthe', A, vc)
        S_new = S * jnp.exp(G_last)[..., None] + jnp.einsum(
            'bshd,bshe->bhde', kc * jnp.exp(G_last[:, None] - G), vc)
        return S_new, o_inter + o_intra

    S_fin, o = lax.scan(step, S0, (chunks(q), chunks(log_f), chunks(k), chunks(v)))
    o = jnp.swapaxes(o, 0, 1).reshape(B, L, H, v.shape[-1])
    return o, S_fin


def hgrn2_mixer(hq, hf, hi, hg, S0, lb, g_norm):
    B, L, _ = hq.shape
    lb = lb.reshape(HG_HEADS, HG_DK)
    q = (jax.nn.silu(hq.astype(jnp.float32)) * HG_DK ** -0.5).reshape(B, L, HG_HEADS, HG_DK)
    fpre = hf.astype(jnp.float32).reshape(B, L, HG_HEADS, HG_DK)
    log_lb = jnp.log(jnp.maximum(lb, LB_TINY))
    log_f = jnp.logaddexp(log_lb, jnp.log1p(-lb) + jax.nn.log_sigmoid(fpre))
    k = (1.0 - lb) * jax.nn.sigmoid(-fpre)
    v = hi.astype(jnp.float32).reshape(B, L, HG_HEADS, HG_DV)
    o, S_fin = hgrn2_chunked(q, log_f, k, v, S0)
    o = rms_norm(o, g_norm).reshape(B, L, HG_WIDTH).astype(hq.dtype)
    return o * jax.nn.silu(hg), S_fin


def short_conv(u, buf, w):
    L = u.shape[1]
    up = jnp.concatenate([buf.astype(u.dtype), u], axis=1)
    y = w[0] * up[:, 0:L]
    for j in range(1, CONV_K):
        y = y + w[j] * up[:, j:j + L]
    return y, up[:, -(CONV_K - 1):]


def run_trunk(x, k_cache, v_cache, s_cache, c_cache, w_in, attn_sink, lower_bound, hgrn_norm,
              conv_w, w_out, norm_mix, norm_ffn, w_gate_up, w_down, norm_final):
    B = x.shape[0]
    sample = k_cache is not None
    split_idx = np.cumsum(IN_SIZES)[:-1].tolist()
    ks, vs, ss, cs = [], [], [], []
    for l in range(DEPTH):
        h = rms_norm(x, norm_mix[l])
        proj = h @ w_in[l]
        q, k, v, hq, hf, hi, hg, cb, cc, ch = jnp.split(proj, split_idx, axis=-1)
        if sample:
            a, k_new, v_new = swa_sample(q, k, v, k_cache[l], v_cache[l], attn_sink[l])
            S0 = s_cache[l].astype(jnp.float32)
            buf = c_cache[l]
        else:
            a, k_new, v_new = swa_prompt(q, k, v, attn_sink[l])
            S0 = jnp.zeros((B, HG_HEADS, HG_DK, HG_DV), jnp.float32)
            buf = jnp.zeros((B, CONV_K - 1, CONV_WIDTH), x.dtype)
        o, S_new = hgrn2_mixer(hq, hf, hi, hg, S0, lower_bound[l], hgrn_norm[l])
        yc, buf_new = short_conv(cc * ch, buf, conv_w[l])
        x = x + jnp.concatenate([a, o, cb * yc], axis=-1) @ w_out[l]
        h = rms_norm(x, norm_ffn[l])
        gate, up = jnp.split(h @ w_gate_up[l], 2, axis=-1)
        x = x + (jax.nn.silu(gate) * up) @ w_down[l]
        ks.append(k_new)
        vs.append(v_new)
        ss.append(S_new.astype(x.dtype))
        cs.append(buf_new)
    y = rms_norm(x, norm_final)
    return y, jnp.stack(ks), jnp.stack(vs), jnp.stack(ss), jnp.stack(cs)


def setup_inputs(seed: int = 0) -> dict:
    key = jax.random.key(seed)
    ks = jax.random.split(key, 17)
    f32 = jnp.float32
    win_buf = min(WINDOW, PAST_LEN)
    nrm = lambda k, shape, s: jax.random.normal(k, shape, f32) * s
    return {
        "x_prompt": nrm(ks[0], (BATCH, SEQ, D_MODEL), 1.0),
        "x_sample": nrm(ks[1], (DEC_BATCH, DEC_SEQ, D_MODEL), 1.0),
        "cache_attn_k": nrm(ks[2], (DEPTH, DEC_BATCH, win_buf, N_KV_HEADS, HEAD_DIM), 1.0),
        "cache_attn_v": nrm(ks[3], (DEPTH, DEC_BATCH, win_buf, N_KV_HEADS, HEAD_DIM), 1.0),
        "state_hgrn": nrm(ks[4], (DEPTH, DEC_BATCH, HG_HEADS, HG_DK, HG_DV), 0.5),
        "state_conv": nrm(ks[5], (DEPTH, DEC_BATCH, CONV_K - 1, CONV_WIDTH), 1.0),
        "w_in": nrm(ks[6], (DEPTH, D_MODEL, IN_WIDTH), D_MODEL ** -0.5),
        "attn_sink": nrm(ks[7], (DEPTH, N_Q_HEADS), 1.0),
        "hgrn_lower_bounds": nrm(ks[8], (DEPTH, HG_KEY), 0.1),
        "hgrn_norm": 1.0 + nrm(ks[9], (DEPTH, HG_HEADS, HG_DV), 0.02),
        "conv_w": nrm(ks[10], (DEPTH, CONV_K, CONV_WIDTH), CONV_K ** -0.5),
        "w_out": nrm(ks[11], (DEPTH, MIX_WIDTH, D_MODEL), MIX_WIDTH ** -0.5),
        "norm_mix": 1.0 + nrm(ks[12], (DEPTH, D_MODEL), 0.02),
        "norm_ffn": 1.0 + nrm(ks[13], (DEPTH, D_MODEL), 0.02),
        "w_gate_up": nrm(ks[14], (DEPTH, D_MODEL, 2 * D_FF), D_MODEL ** -0.5),
        "w_down": nrm(ks[15], (DEPTH, D_FF, D_MODEL), D_FF ** -0.5),
        "norm_final": 1.0 + nrm(ks[16], (D_MODEL,), 0.02),
    }


def reference(x_prompt, x_sample, cache_attn_k, cache_attn_v, state_hgrn, state_conv,
              w_in, attn_sink, hgrn_lower_bounds, hgrn_norm, conv_w, w_out,
              norm_mix, norm_ffn, w_gate_up, w_down, norm_final):
    lbp = jax.nn.softmax(hgrn_lower_bounds.astype(jnp.float32), axis=0)
    lower_bound = jnp.cumsum(lbp, axis=0) - lbp[0:1]
    y_prompt, k_p, v_p, s_p, c_p = run_trunk(
        x_prompt, None, None, None, None, w_in, attn_sink, lower_bound, hgrn_norm,
        conv_w, w_out, norm_mix, norm_ffn, w_gate_up, w_down, norm_final)
    y_sample, k_s, v_s, s_s, c_s = run_trunk(
        x_sample, cache_attn_k, cache_attn_v, state_hgrn, state_conv, w_in, attn_sink,
        lower_bound, hgrn_norm, conv_w, w_out, norm_mix, norm_ffn, w_gate_up, w_down, norm_final)
    return (y_prompt, y_sample, k_p, v_p, s_p, c_p, k_s, v_s, s_s, c_s)
```

```python
import functools

import jax
import jax.numpy as jnp
import numpy as np
from jax import lax
from jax.experimental import pallas as pl
from jax.experimental.pallas import tpu as pltpu

F32 = jnp.float32
BF16 = jnp.bfloat16

D_MODEL = 2048
DEPTH = 4
PAST_LEN = 16384
HEAD_DIM = 64
N_Q_HEADS = 16
N_KV_HEADS = 4
Q_PER_KV = 4
ATTN_WIDTH = 1024
KV_WIDTH = 256
WINDOW = 128
ROPE_THETA = 500000.0
ROT_DIM = 16
ATTN_SCALE = HEAD_DIM ** -0.5
NEG_BIG = -1e30
HG_HEADS = 4
HG_DK = 128
HG_DV = 128
HG_WIDTH = 512
HG_CHUNK = 64
HG_SUB = 16
LB_TINY = 1e-30
CONV_K = 3
CONV_WIDTH = 512
D_FF = 5632
IN_WIDTH = 5120
NORM_EPS = 1e-6
SAMPLE_T = 4
SAMPLE_ROWS = 8

Q_BLK_1024 = 0
K_BLK_256 = 4
V_BLK_256 = 5
HQ_BLK_512, HF_BLK_512, HI_BLK_512, HG_BLK_512 = 3, 4, 5, 6
CB_BLK_512, CC_BLK_512, CH_BLK_512 = 7, 8, 9

VMEM_LIMIT = 56 * 1024 * 1024


def _cparams(*sem):
    return pltpu.CompilerParams(dimension_semantics=sem, vmem_limit_bytes=VMEM_LIMIT)


def _rms(x, g):
    ms = jnp.mean(x * x, axis=-1, keepdims=True)
    return x * lax.rsqrt(ms + NORM_EPS) * g


def _sigmoid(x):
    return 1.0 / (1.0 + jnp.exp(-x))


def _dot(a, b):
    return jnp.dot(a, b, preferred_element_type=F32)


def _dot_nt(a, b):
    return lax.dot_general(a, b, (((1,), (1,)), ((), ())), preferred_element_type=F32)


def _dot_tn(a, b):
    return lax.dot_general(a, b, (((0,), (0,)), ((), ())), preferred_element_type=F32)


def _norm_matmul_kernel(x_ref, g_ref, w_ref, o_ref, h_ref):
    @pl.when(pl.program_id(1) == 0)
    def _():
        h_ref[...] = _rms(x_ref[...], g_ref[...]).astype(BF16)

    o_ref[...] = _dot(h_ref[...], w_ref[...])


def norm_matmul(x, g, w, tm, tn):
    m, k = x.shape
    n = w.shape[1]
    return pl.pallas_call(
        _norm_matmul_kernel,
        grid=(m // tm, n // tn),
        in_specs=[
            pl.BlockSpec((tm, k), lambda i, j: (i, 0)),
            pl.BlockSpec((1, k), lambda i, j: (0, 0)),
            pl.BlockSpec((k, tn), lambda i, j: (0, j)),
        ],
        out_specs=pl.BlockSpec((tm, tn), lambda i, j: (i, j)),
        out_shape=jax.ShapeDtypeStruct((m, n), F32),
        scratch_shapes=[pltpu.VMEM((tm, k), BF16)],
        compiler_params=_cparams("parallel", "arbitrary"),
        name="norm_matmul",
    )(x, g.reshape(1, k), w)


def _out_proj_kernel(x_ref, a_ref, o_ref, c_ref, wa_ref, wo_ref, wc_ref, y_ref):
    acc = x_ref[...] + _dot(a_ref[...].astype(BF16), wa_ref[...])
    acc = acc + _dot(o_ref[...].astype(BF16), wo_ref[...])
    y_ref[...] = acc + _dot(c_ref[...].astype(BF16), wc_ref[...])


def out_proj(x, a, o, c, w_out, tm):
    m, d = x.shape
    return pl.pallas_call(
        _out_proj_kernel,
        grid=(m // tm,),
        in_specs=[
            pl.BlockSpec((tm, d), lambda i: (i, 0)),
            pl.BlockSpec((tm, ATTN_WIDTH), lambda i: (i, 0)),
            pl.BlockSpec((tm, HG_WIDTH), lambda i: (i, 0)),
            pl.BlockSpec((tm, CONV_WIDTH), lambda i: (i, 0)),
            pl.BlockSpec((ATTN_WIDTH, d), lambda i: (0, 0)),
            pl.BlockSpec((HG_WIDTH, d), lambda i: (2, 0)),
            pl.BlockSpec((CONV_WIDTH, d), lambda i: (3, 0)),
        ],
        out_specs=pl.BlockSpec((tm, d), lambda i: (i, 0)),
        out_shape=jax.ShapeDtypeStruct((m, d), F32),
        compiler_params=_cparams("parallel"),
        name="out_proj",
    )(x, a, o, c, w_out, w_out, w_out)


def _ffn_kernel(x_ref, g_ref, wg_ref, wu_ref, wd_ref, y_ref, h_ref):
    @pl.when(pl.program_id(1) == 0)
    def _():
        x = x_ref[...]
        h_ref[...] = _rms(x, g_ref[...]).astype(BF16)
        y_ref[...] = x

    h = h_ref[...]
    gate = _dot(h, wg_ref[...])
    up = _dot(h, wu_ref[...])
    act = (gate * _sigmoid(gate)) * up
    y_ref[...] += _dot(act.astype(BF16), wd_ref[...])


def ffn(x, g, w_gate_up, w_down, tm, tf):
    m, d = x.shape
    nf = D_FF // tf
    return pl.pallas_call(
        _ffn_kernel,
        grid=(m // tm, nf),
        in_specs=[
            pl.BlockSpec((tm, d), lambda i, f: (i, 0)),
            pl.BlockSpec((1, d), lambda i, f: (0, 0)),
            pl.BlockSpec((d, tf), lambda i, f: (0, f)),
            pl.BlockSpec((d, tf), lambda i, f: (0, f + nf)),
            pl.BlockSpec((tf, d), lambda i, f: (f, 0)),
        ],
        out_specs=pl.BlockSpec((tm, d), lambda i, f: (i, 0)),
        out_shape=jax.ShapeDtypeStruct((m, d), F32),
        scratch_shapes=[pltpu.VMEM((tm, d), BF16)],
        compiler_params=_cparams("parallel", "arbitrary"),
        name="ffn",
    )(x, g.reshape(1, d), w_gate_up, w_gate_up, w_down)


def _final_norm_kernel(x_ref, g_ref, y_ref):
    y_ref[...] = _rms(x_ref[...], g_ref[...])


def final_norm(x, g, tm):
    m, d = x.shape
    return pl.pallas_call(
        _final_norm_kernel,
        grid=(m // tm,),
        in_specs=[pl.BlockSpec((tm, d), lambda i: (i, 0)), pl.BlockSpec((1, d), lambda i: (0, 0))],
        out_specs=pl.BlockSpec((tm, d), lambda i: (i, 0)),
        out_shape=jax.ShapeDtypeStruct((m, d), F32),
        compiler_params=_cparams("parallel"),
        name="final_norm",
    )(x, g.reshape(1, d))


def _rope_tables(pos):
    half = ROT_DIM // 2
    inv = ROPE_THETA ** (-jnp.arange(half, dtype=F32) * 2.0 / ROT_DIM)
    ang = pos.astype(F32)[:, None] * inv[None, :]
    cos, sin = jnp.cos(ang), jnp.sin(ang)
    n = pos.shape[0]
    ones = jnp.ones((n, HEAD_DIM - ROT_DIM), F32)
    zeros = jnp.zeros((n, HEAD_DIM - ROT_DIM), F32)
    zh = jnp.zeros((n, half), F32)
    c = jnp.concatenate([cos, cos, ones], axis=-1)
    s_lo = jnp.concatenate([-sin, zh, zeros], axis=-1)
    s_hi = jnp.concatenate([zh, sin, zeros], axis=-1)
    rep = lambda t: jnp.concatenate([t, t], axis=-1)
    return rep(c), rep(s_lo), rep(s_hi)


def _rope(x, c, s_lo, s_hi):
    half = ROT_DIM // 2
    outs = []
    for j in range(x.shape[1] // 128):
        xc = x[:, j * 128:(j + 1) * 128]
        outs.append(xc * c + pltpu.roll(xc, 128 - half, 1) * s_lo + pltpu.roll(xc, half, 1) * s_hi)
    return outs[0] if len(outs) == 1 else jnp.concatenate(outs, axis=1)


def _softmax_sink_pv(s, sk, v):
    m = jnp.maximum(jnp.max(s, axis=-1, keepdims=True), sk)
    p = jnp.exp(s - m)
    denom = jnp.sum(p, axis=-1, keepdims=True) + jnp.exp(sk - m)
    return _dot(p.astype(BF16), v) / denom


def _attn_prompt_kernel(sink_ref, q_ref, k_ref, v_ref, c_ref, slo_ref, shi_ref,
                        a_ref, ko_ref, vo_ref, kprev_ref, vprev_ref):
    n = pl.program_id(1)
    nblk = pl.num_programs(1)
    blk = WINDOW

    @pl.when(n == 0)
    def _():
        kprev_ref[...] = jnp.zeros_like(kprev_ref)
        vprev_ref[...] = jnp.zeros_like(vprev_ref)

    c, s_lo, s_hi = c_ref[...], slo_ref[...], shi_ref[...]
    q = _rope(q_ref[0], c, s_lo, s_hi)
    k = _rope(k_ref[0], c, s_lo, s_hi)
    v = v_ref[0]

    rows = Q_PER_KV * blk
    qi = lax.broadcasted_iota(jnp.int32, (rows, 2 * blk), 0) % blk
    kj = lax.broadcasted_iota(jnp.int32, (rows, 2 * blk), 1)
    mask = (kj >= qi) & (kj <= qi + WINDOW) & ((kj >= blk) | (n > 0))

    for h in range(N_KV_HEADS):
        hs = slice(h * HEAD_DIM, (h + 1) * HEAD_DIM)
        kk = jnp.concatenate([kprev_ref[:, hs], k[:, hs]], axis=0).astype(BF16)
        vv = jnp.concatenate([vprev_ref[:, hs], v[:, hs]], axis=0).astype(BF16)
        qh = jnp.concatenate(
            [q[:, (Q_PER_KV * h + g) * HEAD_DIM:(Q_PER_KV * h + g + 1) * HEAD_DIM] for g in range(Q_PER_KV)],
            axis=0).astype(BF16)
        s = _dot_nt(qh, kk) * ATTN_SCALE
        s = jnp.where(mask, s, NEG_BIG)
        sk = jnp.concatenate(
            [jnp.full((blk, 1), sink_ref[Q_PER_KV * h + g], F32) for g in range(Q_PER_KV)], axis=0)
        o = _softmax_sink_pv(s, sk, vv)
        for g in range(Q_PER_KV):
            col = (Q_PER_KV * h + g) * HEAD_DIM
            a_ref[0, :, col:col + HEAD_DIM] = o[g * blk:(g + 1) * blk].astype(a_ref.dtype)

    kprev_ref[...] = k
    vprev_ref[...] = v

    @pl.when(n == nblk - 1)
    def _():
        ko_ref[0] = k
        vo_ref[0] = v


def attn_prompt(proj, sink, tables):
    b, l, _ = proj.shape
    blk = WINDOW
    tab_spec = pl.BlockSpec((blk, 128), lambda i, n: (n, 0))
    kv_out = jax.ShapeDtypeStruct((b, blk, KV_WIDTH), F32)
    return pl.pallas_call(
        _attn_prompt_kernel,
        grid=(b, l // blk),
        in_specs=[
            pl.BlockSpec(memory_space=pltpu.SMEM),
            pl.BlockSpec((1, blk, ATTN_WIDTH), lambda i, n: (i, n, Q_BLK_1024)),
            pl.BlockSpec((1, blk, KV_WIDTH), lambda i, n: (i, n, K_BLK_256)),
            pl.BlockSpec((1, blk, KV_WIDTH), lambda i, n: (i, n, V_BLK_256)),
            tab_spec, tab_spec, tab_spec,
        ],
        out_specs=[
            pl.BlockSpec((1, blk, ATTN_WIDTH), lambda i, n: (i, n, 0)),
            pl.BlockSpec((1, blk, KV_WIDTH), lambda i, n: (i, 0, 0)),
            pl.BlockSpec((1, blk, KV_WIDTH), lambda i, n: (i, 0, 0)),
        ],
        out_shape=[jax.ShapeDtypeStruct((b, l, ATTN_WIDTH), BF16), kv_out, kv_out],
        scratch_shapes=[pltpu.VMEM((blk, KV_WIDTH), F32), pltpu.VMEM((blk, KV_WIDTH), F32)],
        compiler_params=_cparams("parallel", "arbitrary"),
        name="attn_prompt",
    )(sink, proj, proj, proj, *tables)


def _attn_sample_kernel(sink_ref, q_ref, k_ref, v_ref, kc_ref, vc_ref, c_ref, slo_ref, shi_ref,
                        a_ref, ko_ref, vo_ref):
    r8 = SAMPLE_ROWS
    nseq = r8 // SAMPLE_T
    lb = kc_ref.shape[1]
    c, s_lo, s_hi = c_ref[...], slo_ref[...], shi_ref[...]
    q = _rope(q_ref[...], c, s_lo, s_hi)
    k = _rope(k_ref[...], c, s_lo, s_hi)
    v = v_ref[...]
    pad = jnp.zeros((lb - r8, KV_WIDTH), F32)
    k_pad = jnp.concatenate([k, pad], axis=0)
    v_pad = jnp.concatenate([v, pad], axis=0)

    rows = Q_PER_KV * r8
    ri = lax.broadcasted_iota(jnp.int32, (rows, 2 * lb), 0) % r8
    t = ri % SAMPLE_T
    kj = lax.broadcasted_iota(jnp.int32, (rows, 2 * lb), 1)
    cj = kj - lb
    row_seq = lax.broadcasted_iota(jnp.int32, (rows, HEAD_DIM), 0) % r8 // SAMPLE_T

    outs = [None] * N_Q_HEADS
    for h in range(N_KV_HEADS):
        hs = slice(h * HEAD_DIM, (h + 1) * HEAD_DIM)
        qh = jnp.concatenate(
            [q[:, (Q_PER_KV * h + g) * HEAD_DIM:(Q_PER_KV * h + g + 1) * HEAD_DIM] for g in range(Q_PER_KV)],
            axis=0).astype(BF16)
        sk = jnp.concatenate(
            [jnp.full((r8, 1), sink_ref[Q_PER_KV * h + g], F32) for g in range(Q_PER_KV)], axis=0)
        o = None
        for e in range(nseq):
            mask = ((kj < lb) & (kj >= t)) | (
                (cj >= e * SAMPLE_T) & (cj < (e + 1) * SAMPLE_T) & (cj - e * SAMPLE_T <= t))
            kk = jnp.concatenate([kc_ref[e, :, hs], k_pad[:, hs]], axis=0).astype(BF16)
            vv = jnp.concatenate([vc_ref[e, :, hs], v_pad[:, hs]], axis=0).astype(BF16)
            s = jnp.where(mask, _dot_nt(qh, kk) * ATTN_SCALE, NEG_BIG)
            oe = _softmax_sink_pv(s, sk, vv)
            o = oe if o is None else jnp.where(row_seq == e, oe, o)
        for g in range(Q_PER_KV):
            outs[Q_PER_KV * h + g] = o[g * r8:(g + 1) * r8]
    a_ref[...] = jnp.concatenate(outs, axis=1)

    row = lax.broadcasted_iota(jnp.int32, (lb, KV_WIDTH), 0)
    for e in range(nseq):
        shift = (lb - SAMPLE_T - e * SAMPLE_T) % lb
        for new_pad, cache_ref, out_ref in ((k_pad, kc_ref, ko_ref), (v_pad, vc_ref, vo_ref)):
            new_rows = new_pad if shift == 0 else pltpu.roll(new_pad, shift, 0)
            old_rows = pltpu.roll(cache_ref[e], lb - SAMPLE_T, 0)
            out_ref[e] = jnp.where(row >= lb - SAMPLE_T, new_rows, old_rows)


def attn_sample(proj, k_cache, v_cache, sink, tables):
    m = proj.shape[0]
    r8 = SAMPLE_ROWS
    nseq = r8 // SAMPLE_T
    b, lb, _ = k_cache.shape
    tab_spec = pl.BlockSpec((r8, 128), lambda i: (0, 0))
    cache_spec = pl.BlockSpec((nseq, lb, KV_WIDTH), lambda i: (i, 0, 0))
    kv_out = jax.ShapeDtypeStruct((b, lb, KV_WIDTH), F32)
    return pl.pallas_call(
        _attn_sample_kernel,
        grid=(m // r8,),
        in_specs=[
            pl.BlockSpec(memory_space=pltpu.SMEM),
            pl.BlockSpec((r8, ATTN_WIDTH), lambda i: (i, Q_BLK_1024)),
            pl.BlockSpec((r8, KV_WIDTH), lambda i: (i, K_BLK_256)),
            pl.BlockSpec((r8, KV_WIDTH), lambda i: (i, V_BLK_256)),
            cache_spec, cache_spec,
            tab_spec, tab_spec, tab_spec,
        ],
        out_specs=[pl.BlockSpec((r8, ATTN_WIDTH), lambda i: (i, 0)), cache_spec, cache_spec],
        out_shape=[jax.ShapeDtypeStruct((m, ATTN_WIDTH), F32), kv_out, kv_out],
        compiler_params=_cparams("parallel"),
        name="attn_sample",
    )(sink, proj, proj, proj, k_cache, v_cache, *tables)


def _lower_bound(lbr_ref, layer, hs):
    rows = [lbr_ref[r:r + 1, hs] for r in range(DEPTH)]
    mx = functools.reduce(jnp.maximum, rows)
    es = [jnp.exp(r - mx) for r in rows]
    tot = functools.reduce(lambda a, b: a + b, es)
    lbp = [e / tot for e in es]
    acc = lbp[0]
    for r in range(1, layer + 1):
        acc = acc + lbp[r]
    return acc - lbp[0]


def _hgrn_gates(hq, hf, lb):
    q = hq * _sigmoid(hq) * (HG_DK ** -0.5)
    log_lb = jnp.log(jnp.maximum(lb, LB_TINY))
    log_sig = jnp.minimum(hf, 0.0) - jnp.log1p(jnp.exp(-jnp.abs(hf)))
    b = jnp.log1p(-lb) + log_sig
    log_f = jnp.maximum(log_lb, b) + jnp.log1p(jnp.exp(-jnp.abs(log_lb - b)))
    k = (1.0 - lb) * _sigmoid(-hf)
    return q, log_f, k


def _hgrn_finish(o, g_norm, hg):
    return _rms(o, g_norm) * (hg * _sigmoid(hg))


def _hgrn_prompt_kernel(layer, hq_ref, hf_ref, hi_ref, hg_ref, lbr_ref, gn_ref,
                        o_ref, so_ref, s_ref, q_ref, k_ref, g_ref):
    ci = pl.program_id(1)
    nchunk = pl.num_programs(1)
    c, r = HG_CHUNK, HG_SUB

    @pl.when(ci == 0)
    def _():
        s_ref[...] = jnp.zeros_like(s_ref)

    row = lax.broadcasted_iota(jnp.int32, (c, HG_DK), 0)
    sub_row = lax.broadcasted_iota(jnp.int32, (r, HG_DK), 0)
    for h in range(HG_HEADS):
        hs = slice(h * HG_DK, (h + 1) * HG_DK)
        lb = _lower_bound(lbr_ref, layer, hs)
        q, log_f, k = _hgrn_gates(hq_ref[0, :, hs], hf_ref[0, :, hs], lb)
        gcum = log_f
        sh = 1
        while sh < c:
            gcum = gcum + jnp.where(row >= sh, pltpu.roll(gcum, sh, 0), 0.0)
            sh *= 2
        q_ref[...] = q
        k_ref[...] = k
        g_ref[...] = gcum
        state = s_ref[h]
        g_last = g_ref[c - 1:c, :]

        o_inter = _dot((q * jnp.exp(gcum)).astype(BF16), state.astype(BF16))
        for i in range(c // r):
            rs = slice(i * r, (i + 1) * r)
            g_i = g_ref[rs, :]
            q_i = q_ref[rs, :]
            o_i = o_inter[rs]
            if i > 0:
                g0 = g_ref[i * r - 1:i * r, :]
                qs = (q_i * jnp.exp(g_i - g0)).astype(BF16)
                ks = (k_ref[0:i * r, :] * jnp.exp(g0 - g_ref[0:i * r, :])).astype(BF16)
                a = _dot_nt(qs, ks)
                o_i = o_i + _dot(a.astype(BF16), hi_ref[0, 0:i * r, hs].astype(BF16))
            for s in range(r):
                t = i * r + s
                dec = jnp.exp(jnp.where(sub_row >= s, g_i - g_ref[t:t + 1, :], NEG_BIG))
                col = jnp.sum(q_i * (dec * k_ref[t:t + 1, :]), axis=-1, keepdims=True)
                o_i = o_i + col * hi_ref[0, t:t + 1, hs]
            o_ref[0, rs, hs] = _hgrn_finish(o_i, gn_ref[h:h + 1, :], hg_ref[0, rs, hs]).astype(o_ref.dtype)

        ks_all = (k * jnp.exp(g_last - gcum)).astype(BF16)
        decay = jnp.transpose(jnp.broadcast_to(jnp.exp(g_last), (HG_DK, HG_DK)))
        s_ref[h] = state * decay + _dot_tn(ks_all, hi_ref[0, :, hs].astype(BF16))

    @pl.when(ci == nchunk - 1)
    def _():
        so_ref[0] = s_ref[...]


def hgrn_prompt(proj, lb_raw, g_norm, layer):
    b, l, _ = proj.shape
    c = HG_CHUNK
    col = lambda blk: pl.BlockSpec((1, c, HG_WIDTH), lambda i, n: (i, n, blk))
    return pl.pallas_call(
        functools.partial(_hgrn_prompt_kernel, layer),
        grid=(b, l // c),
        in_specs=[
            col(HQ_BLK_512), col(HF_BLK_512), col(HI_BLK_512), col(HG_BLK_512),
            pl.BlockSpec((DEPTH, HG_WIDTH), lambda i, n: (0, 0)),
            pl.BlockSpec((HG_HEADS, HG_DV), lambda i, n: (0, 0)),
        ],
        out_specs=[
            pl.BlockSpec((1, c, HG_WIDTH), lambda i, n: (i, n, 0)),
            pl.BlockSpec((1, HG_HEADS, HG_DK, HG_DV), lambda i, n: (i, 0, 0, 0)),
        ],
        out_shape=[jax.ShapeDtypeStruct((b, l, HG_WIDTH), BF16),
                   jax.ShapeDtypeStruct((b, HG_HEADS, HG_DK, HG_DV), F32)],
        scratch_shapes=[pltpu.VMEM((HG_HEADS, HG_DK, HG_DV), F32),
                        pltpu.VMEM((c, HG_DK), F32), pltpu.VMEM((c, HG_DK), F32), pltpu.VMEM((c, HG_DK), F32)],
        compiler_params=_cparams("parallel", "arbitrary"),
        name="hgrn_prompt",
    )(proj, proj, proj, proj, lb_raw, g_norm)


def _hgrn_sample_kernel(layer, hq_ref, hf_ref, hi_ref, hg_ref, lbr_ref, gn_ref, s_ref,
                        o_ref, so_ref, q_ref, k_ref, g_ref):
    r8, tt = SAMPLE_ROWS, SAMPLE_T
    nseq = r8 // tt
    row = lax.broadcasted_iota(jnp.int32, (r8, HG_DK), 0)
    t_idx = row % tt
    seq = row // tt

    def pick(ref, s):
        out = ref[s:s + 1, :]
        for e in range(1, nseq):
            out = jnp.where(seq == e, ref[e * tt + s:e * tt + s + 1, :], out)
        return out

    for h in range(HG_HEADS):
        hs = slice(h * HG_DK, (h + 1) * HG_DK)
        lb = _lower_bound(lbr_ref, layer, hs)
        q, log_f, k = _hgrn_gates(hq_ref[:, hs], hf_ref[:, hs], lb)
        gcum = log_f
        sh = 1
        while sh < tt:
            gcum = gcum + jnp.where(t_idx >= sh, pltpu.roll(gcum, sh, 0), 0.0)
            sh *= 2
        q_ref[...] = q
        k_ref[...] = k
        g_ref[...] = gcum
        v = hi_ref[:, hs]
        g_last = pick(g_ref, tt - 1)
        qs = (q * jnp.exp(gcum)).astype(BF16)
        ks_all = k * jnp.exp(g_last - gcum)

        o = jnp.zeros((r8, HG_DV), F32)
        for e in range(nseq):
            state = s_ref[e, h]
            o = jnp.where(seq == e, _dot(qs, state.astype(BF16)), o)
            g_last_e = g_ref[e * tt + tt - 1:e * tt + tt, :]
            decay = jnp.transpose(jnp.broadcast_to(jnp.exp(g_last_e), (HG_DK, HG_DK)))
            ks_e = jnp.where(seq == e, ks_all, 0.0).astype(BF16)
            so_ref[e, h] = state * decay + _dot_tn(ks_e, v.astype(BF16))
        for s in range(tt):
            dec = jnp.exp(jnp.where(t_idx >= s, gcum - pick(g_ref, s), NEG_BIG))
            col = jnp.sum(q * (dec * pick(k_ref, s)), axis=-1, keepdims=True)
            o = o + col * pick(hi_ref.at[:, hs], s)
        o_ref[:, hs] = _hgrn_finish(o, gn_ref[h:h + 1, :], hg_ref[:, hs])


def hgrn_sample(proj, state, lb_raw, g_norm, layer):
    m = proj.shape[0]
    r8 = SAMPLE_ROWS
    nseq = r8 // SAMPLE_T
    col = lambda blk: pl.BlockSpec((r8, HG_WIDTH), lambda i: (i, blk))
    st_spec = pl.BlockSpec((nseq, HG_HEADS, HG_DK, HG_DV), lambda i: (i, 0, 0, 0))
    return pl.pallas_call(
        functools.partial(_hgrn_sample_kernel, layer),
        grid=(m // r8,),
        in_specs=[
            col(HQ_BLK_512), col(HF_BLK_512), col(HI_BLK_512), col(HG_BLK_512),
            pl.BlockSpec((DEPTH, HG_WIDTH), lambda i: (0, 0)),
            pl.BlockSpec((HG_HEADS, HG_DV), lambda i: (0, 0)),
            st_spec,
        ],
        out_specs=[pl.BlockSpec((r8, HG_WIDTH), lambda i: (i, 0)), st_spec],
        out_shape=[jax.ShapeDtypeStruct((m, HG_WIDTH), F32), jax.ShapeDtypeStruct(state.shape, F32)],
        scratch_shapes=[pltpu.VMEM((r8, HG_DK), F32), pltpu.VMEM((r8, HG_DK), F32), pltpu.VMEM((r8, HG_DK), F32)],
        compiler_params=_cparams("parallel"),
        name="hgrn_sample",
    )(proj, proj, proj, proj, lb_raw, g_norm, state)


def _conv_prompt_kernel(cb_ref, cc_ref, ch_ref, w_ref, y_ref, so_ref, carry_ref):
    n = pl.program_id(1)
    nt = pl.num_programs(1)
    tc = cc_ref.shape[1]

    @pl.when(n == 0)
    def _():
        carry_ref[...] = jnp.zeros_like(carry_ref)

    u = cc_ref[0] * ch_ref[0]
    row = lax.broadcasted_iota(jnp.int32, u.shape, 0)
    last1 = carry_ref[7:8, :]
    last2 = carry_ref[6:7, :]
    um1 = jnp.where(row == 0, last1, pltpu.roll(u, 1, 0))
    um2 = jnp.where(row == 0, last2, jnp.where(row == 1, last1, pltpu.roll(u, 2, 0)))
    y = w_ref[0:1, :] * um2 + w_ref[1:2, :] * um1 + w_ref[2:3, :] * u
    y_ref[0] = (cb_ref[0] * y).astype(y_ref.dtype)
    carry_ref[...] = u[tc - 8:tc]

    @pl.when(n == nt - 1)
    def _():
        so_ref[0] = carry_ref[8 - (CONV_K - 1):8, :]


def conv_prompt(proj, w, tc):
    b, l, _ = proj.shape
    col = lambda blk: pl.BlockSpec((1, tc, CONV_WIDTH), lambda i, n: (i, n, blk))
    return pl.pallas_call(
        _conv_prompt_kernel,
        grid=(b, l // tc),
        in_specs=[col(CB_BLK_512), col(CC_BLK_512), col(CH_BLK_512),
                  pl.BlockSpec((CONV_K, CONV_WIDTH), lambda i, n: (0, 0))],
        out_specs=[pl.BlockSpec((1, tc, CONV_WIDTH), lambda i, n: (i, n, 0)),
                   pl.BlockSpec((1, CONV_K - 1, CONV_WIDTH), lambda i, n: (i, 0, 0))],
        out_shape=[jax.ShapeDtypeStruct((b, l, CONV_WIDTH), BF16),
                   jax.ShapeDtypeStruct((b, CONV_K - 1, CONV_WIDTH), F32)],
        scratch_shapes=[pltpu.VMEM((8, CONV_WIDTH), F32)],
        compiler_params=_cparams("parallel", "arbitrary"),
        name="conv_prompt",
    )(proj, proj, proj, w)


def _conv_sample_kernel(cb_ref, cc_ref, ch_ref, w_ref, st_ref, y_ref, so_ref, u_ref):
    r8, tt = SAMPLE_ROWS, SAMPLE_T
    nseq = r8 // tt
    u = cc_ref[...] * ch_ref[...]
    u_ref[...] = u
    row = lax.broadcasted_iota(jnp.int32, u.shape, 0)
    t_idx = row % tt
    seq = row // tt
    last1 = st_ref[0, 1:2, :]
    last2 = st_ref[0, 0:1, :]
    for e in range(1, nseq):
        last1 = jnp.where(seq == e, st_ref[e, 1:2, :], last1)
        last2 = jnp.where(seq == e, st_ref[e, 0:1, :], last2)
    um1 = jnp.where(t_idx == 0, last1, pltpu.roll(u, 1, 0))
    um2 = jnp.where(t_idx == 0, last2, jnp.where(t_idx == 1, last1, pltpu.roll(u, 2, 0)))
    y = w_ref[0:1, :] * um2 + w_ref[1:2, :] * um1 + w_ref[2:3, :] * u
    y_ref[...] = cb_ref[...] * y
    for e in range(nseq):
        so_ref[e] = u_ref[(e + 1) * tt - (CONV_K - 1):(e + 1) * tt, :]


def conv_sample(proj, state, w):
    m = proj.shape[0]
    r8 = SAMPLE_ROWS
    nseq = r8 // SAMPLE_T
    col = lambda blk: pl.BlockSpec((r8, CONV_WIDTH), lambda i: (i, blk))
    st_spec = pl.BlockSpec((nseq, CONV_K - 1, CONV_WIDTH), lambda i: (i, 0, 0))
    return pl.pallas_call(
        _conv_sample_kernel,
        grid=(m // r8,),
        in_specs=[col(CB_BLK_512), col(CC_BLK_512), col(CH_BLK_512),
                  pl.BlockSpec((CONV_K, CONV_WIDTH), lambda i: (0, 0)), st_spec],
        out_specs=[pl.BlockSpec((r8, CONV_WIDTH), lambda i: (i, 0)), st_spec],
        out_shape=[jax.ShapeDtypeStruct((m, CONV_WIDTH), F32), jax.ShapeDtypeStruct(state.shape, F32)],
        scratch_shapes=[pltpu.VMEM((r8, CONV_WIDTH), F32)],
        compiler_params=_cparams("parallel"),
        name="conv_sample",
    )(proj, proj, proj, w, state)


def kernel(x_prompt, x_sample, cache_attn_k, cache_attn_v, state_hgrn, state_conv, w_in, attn_sink,
           hgrn_lower_bounds, hgrn_norm, conv_w, w_out, norm_mix, norm_ffn, w_gate_up, w_down, norm_final):
    bp, lp, d = x_prompt.shape
    bs, ls, _ = x_sample.shape
    assert ls == SAMPLE_T and (bs * ls) % SAMPLE_ROWS == 0
    lb = cache_attn_k.shape[2]
    mp, ms = bp * lp, bs * ls

    w_in_b = w_in.astype(BF16)
    w_out_b = w_out.astype(BF16)
    w_gu_b = w_gate_up.astype(BF16)
    w_down_b = w_down.astype(BF16)

    tab_p = _rope_tables(jnp.arange(lp, dtype=jnp.int32))
    pos_s = PAST_LEN + jnp.arange(SAMPLE_ROWS, dtype=jnp.int32) % SAMPLE_T
    tab_s = _rope_tables(pos_s)

    xp = x_prompt.reshape(mp, d)
    xs = x_sample.reshape(ms, d)
    kc = cache_attn_k.reshape(DEPTH, bs, lb, KV_WIDTH)
    vc = cache_attn_v.reshape(DEPTH, bs, lb, KV_WIDTH)

    outs = {name: [] for name in ("kp", "vp", "sp", "cp", "ks", "vs", "ss", "cs")}
    for l in range(DEPTH):
        proj_p = norm_matmul(xp, norm_mix[l], w_in_b[l], 512, 1024)
        proj_s = norm_matmul(xs, norm_mix[l], w_in_b[l], ms, 1024)
        proj_p3 = proj_p.reshape(bp, lp, IN_WIDTH)

        a_p, k_p, v_p = attn_prompt(proj_p3, attn_sink[l], tab_p)
        a_s, k_s, v_s = attn_sample(proj_s, kc[l], vc[l], attn_sink[l], tab_s)
        o_p, s_p = hgrn_prompt(proj_p3, hgrn_lower_bounds, hgrn_norm[l], l)
        o_s, s_s = hgrn_sample(proj_s, state_hgrn[l], hgrn_lower_bounds, hgrn_norm[l], l)
        c_p, cst_p = conv_prompt(proj_p3, conv_w[l], 512)
        c_s, cst_s = conv_sample(proj_s, state_conv[l], conv_w[l])

        xp = out_proj(xp, a_p.reshape(mp, ATTN_WIDTH), o_p.reshape(mp, HG_WIDTH),
                      c_p.reshape(mp, CONV_WIDTH), w_out_b[l], 512)
        xs = out_proj(xs, a_s, o_s, c_s, w_out_b[l], ms)
        xp = ffn(xp, norm_ffn[l], w_gu_b[l], w_down_b[l], 512, 512)
        xs = ffn(xs, norm_ffn[l], w_gu_b[l], w_down_b[l], ms, 512)

        outs["kp"].append(k_p.reshape(bp, WINDOW, N_KV_HEADS, HEAD_DIM))
        outs["vp"].append(v_p.reshape(bp, WINDOW, N_KV_HEADS, HEAD_DIM))
        outs["sp"].append(s_p)
        outs["cp"].append(cst_p)
        outs["ks"].append(k_s.reshape(bs, lb, N_KV_HEADS, HEAD_DIM))
        outs["vs"].append(v_s.reshape(bs, lb, N_KV_HEADS, HEAD_DIM))
        outs["ss"].append(s_s)
        outs["cs"].append(cst_s)

    y_p = final_norm(xp, norm_final, 512).reshape(bp, lp, d)
    y_s = final_norm(xs, norm_final, ms).reshape(bs, ls, d)
    st = lambda name: jnp.stack(outs[name])
    return (y_p, y_s, st("kp"), st("vp"), st("sp"), st("cp"), st("ks"), st("vs"), st("ss"), st("cs"))
```

```python
import functools

import jax
import jax.numpy as jnp
import numpy as np
from jax import lax
from jax.experimental import pallas as pl
from jax.experimental.pallas import tpu as pltpu

F32 = jnp.float32
BF16 = jnp.bfloat16

D_MODEL = 2048
DEPTH = 4
PAST_LEN = 16384
HEAD_DIM = 64
N_Q_HEADS = 16
N_KV_HEADS = 4
Q_PER_KV = 4
ATTN_WIDTH = 1024
KV_WIDTH = 256
WINDOW = 128
ROPE_THETA = 500000.0
ROT_DIM = 16
ATTN_SCALE = HEAD_DIM ** -0.5
NEG_BIG = -1e30
HG_HEADS = 4
HG_DK = 128
HG_DV = 128
HG_WIDTH = 512
HG_CHUNK = 64
LB_TINY = 1e-30
CONV_K = 3
CONV_WIDTH = 512
D_FF = 5632
IN_WIDTH = 5120
NORM_EPS = 1e-6
SAMPLE_T = 4
SAMPLE_ROWS = 8

Q_BLK_1024 = 0
K_BLK_256 = 4
V_BLK_256 = 5
HQ_BLK_512, HF_BLK_512, HI_BLK_512, HG_BLK_512 = 3, 4, 5, 6
CB_BLK_512, CC_BLK_512, CH_BLK_512 = 7, 8, 9

VMEM_LIMIT = 56 * 1024 * 1024


def _cparams(*sem):
    return pltpu.CompilerParams(dimension_semantics=sem, vmem_limit_bytes=VMEM_LIMIT)


def _rms(x, g):
    ms = jnp.mean(x * x, axis=-1, keepdims=True)
    return x * lax.rsqrt(ms + NORM_EPS) * g


def _sigmoid(x):
    return 1.0 / (1.0 + jnp.exp(-x))


def _dot(a, b):
    return jnp.dot(a, b, preferred_element_type=F32)


def _dot_nt(a, b):
    return lax.dot_general(a, b, (((1,), (1,)), ((), ())), preferred_element_type=F32)


def _dot_tn(a, b):
    return lax.dot_general(a, b, (((0,), (0,)), ((), ())), preferred_element_type=F32)


def _norm_matmul_kernel(x_ref, g_ref, w_ref, o_ref, h_ref):
    @pl.when(pl.program_id(1) == 0)
    def _():
        h_ref[...] = _rms(x_ref[...], g_ref[...]).astype(BF16)

    o_ref[...] = _dot(h_ref[...], w_ref[...])


def norm_matmul(x, g, w, layer, tm, tn):
    m, k = x.shape
    n = w.shape[2]
    return pl.pallas_call(
        _norm_matmul_kernel,
        grid=(m // tm, n // tn),
        in_specs=[
            pl.BlockSpec((tm, k), lambda i, j: (i, 0)),
            pl.BlockSpec((None, 1, k), lambda i, j: (layer, 0, 0)),
            pl.BlockSpec((None, k, tn), lambda i, j: (layer, 0, j)),
        ],
        out_specs=pl.BlockSpec((tm, tn), lambda i, j: (i, j)),
        out_shape=jax.ShapeDtypeStruct((m, n), F32),
        scratch_shapes=[pltpu.VMEM((tm, k), BF16)],
        compiler_params=_cparams("parallel", "arbitrary"),
        name="norm_matmul",
    )(x, g, w)


def _out_proj_kernel(x_ref, a_ref, o_ref, c_ref, wa_ref, wo_ref, wc_ref, y_ref):
    acc = x_ref[...] + _dot(a_ref[...].astype(BF16), wa_ref[...])
    acc = acc + _dot(o_ref[...].astype(BF16), wo_ref[...])
    y_ref[...] = acc + _dot(c_ref[...].astype(BF16), wc_ref[...])


def out_proj(x, a, o, c, w_out, layer, tm):
    m, d = x.shape
    return pl.pallas_call(
        _out_proj_kernel,
        grid=(m // tm,),
        in_specs=[
            pl.BlockSpec((tm, d), lambda i: (i, 0)),
            pl.BlockSpec((tm, ATTN_WIDTH), lambda i: (i, 0)),
            pl.BlockSpec((tm, HG_WIDTH), lambda i: (i, 0)),
            pl.BlockSpec((tm, CONV_WIDTH), lambda i: (i, 0)),
            pl.BlockSpec((None, ATTN_WIDTH, d), lambda i: (layer, 0, 0)),
            pl.BlockSpec((None, HG_WIDTH, d), lambda i: (layer, ATTN_WIDTH // HG_WIDTH, 0)),
            pl.BlockSpec((None, CONV_WIDTH, d), lambda i: (layer, (ATTN_WIDTH + HG_WIDTH) // CONV_WIDTH, 0)),
        ],
        out_specs=pl.BlockSpec((tm, d), lambda i: (i, 0)),
        out_shape=jax.ShapeDtypeStruct((m, d), F32),
        compiler_params=_cparams("parallel"),
        name="out_proj",
    )(x, a, o, c, w_out, w_out, w_out)


def _ffn_kernel(x_ref, g_ref, wg_ref, wu_ref, wd_ref, y_ref, h_ref):
    @pl.when(pl.program_id(1) == 0)
    def _():
        x = x_ref[...]
        h_ref[...] = _rms(x, g_ref[...]).astype(BF16)
        y_ref[...] = x

    h = h_ref[...]
    gate = _dot(h, wg_ref[...])
    up = _dot(h, wu_ref[...])
    act = (gate * _sigmoid(gate)) * up
    y_ref[...] += _dot(act.astype(BF16), wd_ref[...])


def ffn(x, g, w_gate_up, w_down, layer, tm, tf):
    m, d = x.shape
    nf = D_FF // tf
    return pl.pallas_call(
        _ffn_kernel,
        grid=(m // tm, nf),
        in_specs=[
            pl.BlockSpec((tm, d), lambda i, f: (i, 0)),
            pl.BlockSpec((None, 1, d), lambda i, f: (layer, 0, 0)),
            pl.BlockSpec((None, d, tf), lambda i, f: (layer, 0, f)),
            pl.BlockSpec((None, d, tf), lambda i, f: (layer, 0, f + nf)),
            pl.BlockSpec((None, tf, d), lambda i, f: (layer, f, 0)),
        ],
        out_specs=pl.BlockSpec((tm, d), lambda i, f: (i, 0)),
        out_shape=jax.ShapeDtypeStruct((m, d), F32),
        scratch_shapes=[pltpu.VMEM((tm, d), BF16)],
        compiler_params=_cparams("parallel", "arbitrary"),
        name="ffn",
    )(x, g, w_gate_up, w_gate_up, w_down)


def _final_norm_kernel(x_ref, g_ref, y_ref):
    y_ref[...] = _rms(x_ref[...], g_ref[...])


def final_norm(x, g, tm):
    m, d = x.shape
    return pl.pallas_call(
        _final_norm_kernel,
        grid=(m // tm,),
        in_specs=[pl.BlockSpec((tm, d), lambda i: (i, 0)), pl.BlockSpec((1, d), lambda i: (0, 0))],
        out_specs=pl.BlockSpec((tm, d), lambda i: (i, 0)),
        out_shape=jax.ShapeDtypeStruct((m, d), F32),
        compiler_params=_cparams("parallel"),
        name="final_norm",
    )(x, g.reshape(1, d))


def _rope_tables(pos):
    half = ROT_DIM // 2
    inv = ROPE_THETA ** (-jnp.arange(half, dtype=F32) * 2.0 / ROT_DIM)
    ang = pos.astype(F32)[:, None] * inv[None, :]
    cos, sin = jnp.cos(ang), jnp.sin(ang)
    n = pos.shape[0]
    ones = jnp.ones((n, HEAD_DIM - ROT_DIM), F32)
    zeros = jnp.zeros((n, HEAD_DIM - ROT_DIM), F32)
    zh = jnp.zeros((n, half), F32)
    c = jnp.concatenate([cos, cos, ones], axis=-1)
    s_lo = jnp.concatenate([-sin, zh, zeros], axis=-1)
    s_hi = jnp.concatenate([zh, sin, zeros], axis=-1)
    rep = lambda t: jnp.concatenate([t, t], axis=-1)
    return rep(c), rep(s_lo), rep(s_hi)


def _rope(x, c, s_lo, s_hi):
    half = ROT_DIM // 2
    outs = []
    for j in range(x.shape[1] // 128):
        xc = x[:, j * 128:(j + 1) * 128]
        outs.append(xc * c + pltpu.roll(xc, 128 - half, 1) * s_lo + pltpu.roll(xc, half, 1) * s_hi)
    return outs[0] if len(outs) == 1 else jnp.concatenate(outs, axis=1)


def _softmax_sink_pv(s, sk, v):
    m = jnp.maximum(jnp.max(s, axis=-1, keepdims=True), sk)
    p = jnp.exp(s - m)
    denom = jnp.sum(p, axis=-1, keepdims=True) + jnp.exp(sk - m)
    return _dot(p.astype(BF16), v) / denom


def _attn_prompt_kernel(layer, sink_ref, q_ref, k_ref, v_ref, c_ref, slo_ref, shi_ref,
                        a_ref, ko_ref, vo_ref, kprev_ref, vprev_ref):
    n = pl.program_id(1)
    nblk = pl.num_programs(1)
    blk = WINDOW

    @pl.when(n == 0)
    def _():
        kprev_ref[...] = jnp.zeros_like(kprev_ref)
        vprev_ref[...] = jnp.zeros_like(vprev_ref)

    c, s_lo, s_hi = c_ref[...], slo_ref[...], shi_ref[...]
    q = _rope(q_ref[0], c, s_lo, s_hi)
    k = _rope(k_ref[0], c, s_lo, s_hi)
    v = v_ref[0]

    rows = Q_PER_KV * blk
    qi = lax.broadcasted_iota(jnp.int32, (rows, 2 * blk), 0) % blk
    kj = lax.broadcasted_iota(jnp.int32, (rows, 2 * blk), 1)
    mask = (kj >= qi) & (kj <= qi + WINDOW) & ((kj >= blk) | (n > 0))

    for h in range(N_KV_HEADS):
        hs = slice(h * HEAD_DIM, (h + 1) * HEAD_DIM)
        kk = jnp.concatenate([kprev_ref[:, hs], k[:, hs]], axis=0).astype(BF16)
        vv = jnp.concatenate([vprev_ref[:, hs], v[:, hs]], axis=0).astype(BF16)
        qh = jnp.concatenate(
            [q[:, (Q_PER_KV * h + g) * HEAD_DIM:(Q_PER_KV * h + g + 1) * HEAD_DIM] for g in range(Q_PER_KV)],
            axis=0).astype(BF16)
        s = _dot_nt(qh, kk) * ATTN_SCALE
        s = jnp.where(mask, s, NEG_BIG)
        sk = jnp.concatenate(
            [jnp.full((blk, 1), sink_ref[layer, Q_PER_KV * h + g], F32) for g in range(Q_PER_KV)], axis=0)
        o = _softmax_sink_pv(s, sk, vv)
        for g in range(Q_PER_KV):
            col = (Q_PER_KV * h + g) * HEAD_DIM
            a_ref[0, :, col:col + HEAD_DIM] = o[g * blk:(g + 1) * blk].astype(a_ref.dtype)

    kprev_ref[...] = k
    vprev_ref[...] = v

    @pl.when(n == nblk - 1)
    def _():
        ko_ref[0] = k
        vo_ref[0] = v


def attn_prompt(proj, sink, tables, layer):
    b, l, _ = proj.shape
    blk = WINDOW
    tab_spec = pl.BlockSpec((blk, 128), lambda i, n: (n, 0))
    kv_out = jax.ShapeDtypeStruct((b, blk, KV_WIDTH), F32)
    return pl.pallas_call(
        functools.partial(_attn_prompt_kernel, layer),
        grid=(b, l // blk),
        in_specs=[
            pl.BlockSpec(memory_space=pltpu.SMEM),
            pl.BlockSpec((1, blk, ATTN_WIDTH), lambda i, n: (i, n, Q_BLK_1024)),
            pl.BlockSpec((1, blk, KV_WIDTH), lambda i, n: (i, n, K_BLK_256)),
            pl.BlockSpec((1, blk, KV_WIDTH), lambda i, n: (i, n, V_BLK_256)),
            tab_spec, tab_spec, tab_spec,
        ],
        out_specs=[
            pl.BlockSpec((1, blk, ATTN_WIDTH), lambda i, n: (i, n, 0)),
            pl.BlockSpec((1, blk, KV_WIDTH), lambda i, n: (i, 0, 0)),
            pl.BlockSpec((1, blk, KV_WIDTH), lambda i, n: (i, 0, 0)),
        ],
        out_shape=[jax.ShapeDtypeStruct((b, l, ATTN_WIDTH), BF16), kv_out, kv_out],
        scratch_shapes=[pltpu.VMEM((blk, KV_WIDTH), F32), pltpu.VMEM((blk, KV_WIDTH), F32)],
        compiler_params=_cparams("parallel", "arbitrary"),
        name="attn_prompt",
    )(sink, proj, proj, proj, *tables)


def _attn_sample_kernel(layer, sink_ref, q_ref, k_ref, v_ref, kc_ref, vc_ref, c_ref, slo_ref, shi_ref,
                        a_ref, ko_ref, vo_ref):
    r8 = SAMPLE_ROWS
    nseq = r8 // SAMPLE_T
    lb = kc_ref.shape[1]
    c, s_lo, s_hi = c_ref[...], slo_ref[...], shi_ref[...]
    q = _rope(q_ref[...], c, s_lo, s_hi)
    k = _rope(k_ref[...], c, s_lo, s_hi)
    v = v_ref[...]
    pad = jnp.zeros((lb - r8, KV_WIDTH), F32)
    k_pad = jnp.concatenate([k, pad], axis=0)
    v_pad = jnp.concatenate([v, pad], axis=0)

    rows = Q_PER_KV * r8
    ri = lax.broadcasted_iota(jnp.int32, (rows, 2 * lb), 0) % r8
    t = ri % SAMPLE_T
    kj = lax.broadcasted_iota(jnp.int32, (rows, 2 * lb), 1)
    cj = kj - lb
    row_seq = lax.broadcasted_iota(jnp.int32, (rows, HEAD_DIM), 0) % r8 // SAMPLE_T

    outs = [None] * N_Q_HEADS
    for h in range(N_KV_HEADS):
        hs = slice(h * HEAD_DIM, (h + 1) * HEAD_DIM)
        qh = jnp.concatenate(
            [q[:, (Q_PER_KV * h + g) * HEAD_DIM:(Q_PER_KV * h + g + 1) * HEAD_DIM] for g in range(Q_PER_KV)],
            axis=0).astype(BF16)
        sk = jnp.concatenate(
            [jnp.full((r8, 1), sink_ref[layer, Q_PER_KV * h + g], F32) for g in range(Q_PER_KV)], axis=0)
        o = None
        for e in range(nseq):
            mask = ((kj < lb) & (kj >= t)) | (
                (cj >= e * SAMPLE_T) & (cj < (e + 1) * SAMPLE_T) & (cj - e * SAMPLE_T <= t))
            kk = jnp.concatenate([kc_ref[e, :, hs], k_pad[:, hs]], axis=0).astype(BF16)
            vv = jnp.concatenate([vc_ref[e, :, hs], v_pad[:, hs]], axis=0).astype(BF16)
            s = jnp.where(mask, _dot_nt(qh, kk) * ATTN_SCALE, NEG_BIG)
            oe = _softmax_sink_pv(s, sk, vv)
            o = oe if o is None else jnp.where(row_seq == e, oe, o)
        for g in range(Q_PER_KV):
            outs[Q_PER_KV * h + g] = o[g * r8:(g + 1) * r8]
    a_ref[...] = jnp.concatenate(outs, axis=1)

    row = lax.broadcasted_iota(jnp.int32, (lb, KV_WIDTH), 0)
    for e in range(nseq):
        shift = (lb - SAMPLE_T - e * SAMPLE_T) % lb
        for new_pad, cache_ref, out_ref in ((k_pad, kc_ref, ko_ref), (v_pad, vc_ref, vo_ref)):
            new_rows = new_pad if shift == 0 else pltpu.roll(new_pad, shift, 0)
            old_rows = pltpu.roll(cache_ref[e], lb - SAMPLE_T, 0)
            out_ref[e] = jnp.where(row >= lb - SAMPLE_T, new_rows, old_rows)


def attn_sample(proj, k_cache, v_cache, sink, tables, layer):
    m = proj.shape[0]
    r8 = SAMPLE_ROWS
    nseq = r8 // SAMPLE_T
    _, b, lb, _ = k_cache.shape
    tab_spec = pl.BlockSpec((r8, 128), lambda i: (0, 0))
    cache_in = pl.BlockSpec((None, nseq, lb, KV_WIDTH), lambda i: (layer, i, 0, 0))
    cache_spec = pl.BlockSpec((nseq, lb, KV_WIDTH), lambda i: (i, 0, 0))
    kv_out = jax.ShapeDtypeStruct((b, lb, KV_WIDTH), F32)
    return pl.pallas_call(
        functools.partial(_attn_sample_kernel, layer),
        grid=(m // r8,),
        in_specs=[
            pl.BlockSpec(memory_space=pltpu.SMEM),
            pl.BlockSpec((r8, ATTN_WIDTH), lambda i: (i, Q_BLK_1024)),
            pl.BlockSpec((r8, KV_WIDTH), lambda i: (i, K_BLK_256)),
            pl.BlockSpec((r8, KV_WIDTH), lambda i: (i, V_BLK_256)),
            cache_in, cache_in,
            tab_spec, tab_spec, tab_spec,
        ],
        out_specs=[pl.BlockSpec((r8, ATTN_WIDTH), lambda i: (i, 0)), cache_spec, cache_spec],
        out_shape=[jax.ShapeDtypeStruct((m, ATTN_WIDTH), F32), kv_out, kv_out],
        compiler_params=_cparams("parallel"),
        name="attn_sample",
    )(sink, proj, proj, proj, k_cache, v_cache, *tables)


def _lower_bound(lbr_ref, layer, hs):
    rows = [lbr_ref[r:r + 1, hs] for r in range(DEPTH)]
    mx = functools.reduce(jnp.maximum, rows)
    es = [jnp.exp(r - mx) for r in rows]
    tot = functools.reduce(lambda a, b: a + b, es)
    lbp = [e / tot for e in es]
    acc = lbp[0]
    for r in range(1, layer + 1):
        acc = acc + lbp[r]
    return acc - lbp[0]


def _hgrn_gates(hq, hf, lb):
    q = hq * _sigmoid(hq) * (HG_DK ** -0.5)
    e = jnp.exp(-jnp.abs(hf))
    r = 1.0 / (1.0 + e)
    pos = hf >= 0.0
    sig_pos = jnp.where(pos, r, e * r)
    sig_neg = jnp.where(pos, e * r, r)
    log_f = jnp.log(jnp.maximum(lb, LB_TINY) + (1.0 - lb) * sig_pos)
    k = (1.0 - lb) * sig_neg
    return q, log_f, k


def _hgrn_finish(o, g_norm, hg):
    return _rms(o, g_norm) * (hg * _sigmoid(hg))


def _hgrn_levels():
    s = HG_CHUNK // 2
    while s >= 1:
        yield s
        s //= 2


def _hgrn_level_masks():
    t = np.arange(HG_CHUNK)[:, None]
    s = np.arange(HG_CHUNK)[None, :]
    return np.stack([((t % (2 * h) >= h) & (t // (2 * h) == s // (2 * h)) & (s % (2 * h) < h))
                     for h in _hgrn_levels()]).astype(np.float32)


def _hgrn_pair_reference(g_ref, gcum, half, sub8):
    c = HG_CHUNK
    pair = 2 * half
    bcast = lambda r: jnp.broadcast_to(g_ref[r:r + 1, :], (8, HG_DK))
    if pair >= 8:
        return jnp.concatenate([bcast((8 * v // pair) * pair + half - 1) for v in range(c // 8)], axis=0)
    if pair == 4:
        lo = jnp.concatenate([bcast(8 * v + 1) for v in range(c // 8)], axis=0)
        hi = jnp.concatenate([bcast(8 * v + 5) for v in range(c // 8)], axis=0)
        return jnp.where(sub8 < 4, lo, hi)
    return jnp.where(sub8 % 2 == 1, pltpu.roll(gcum, 1, 0), gcum)


def _hgrn_intra(q, k, gcum, g_ref, m_ref, row):
    sub8 = row % 8
    a = jnp.zeros((HG_CHUNK, HG_CHUNK), F32)
    for lvl, half in enumerate(_hgrn_levels()):
        ref = _hgrn_pair_reference(g_ref, gcum, half, sub8)
        right = row % (2 * half) >= half
        scaled = jnp.where(right, q, k) * jnp.exp(jnp.where(right, gcum - ref, ref - gcum))
        scaled = scaled.astype(BF16)
        a = a + _dot_nt(scaled, scaled) * m_ref[lvl]
    return a


def _hgrn_prompt_kernel(layer, hq_ref, hf_ref, hi_ref, hg_ref, lbr_ref, gn_ref, m_ref,
                        o_ref, so_ref, s_ref, g_ref):
    step = pl.program_id(1)
    nstep = pl.num_programs(1)
    c = HG_CHUNK

    @pl.when(step == 0)
    def _():
        s_ref[...] = jnp.zeros_like(s_ref)

    row = lax.broadcasted_iota(jnp.int32, (c, HG_DK), 0)
    for h in range(HG_HEADS):
        hs = slice(h * HG_DK, (h + 1) * HG_DK)
        lb = _lower_bound(lbr_ref, layer, hs)
        state = s_ref[h]
        for cc in range(hq_ref.shape[1] // c):
            rs = slice(cc * c, (cc + 1) * c)
            gbuf = g_ref.at[h * (hq_ref.shape[1] // c) + cc]
            q, log_f, k = _hgrn_gates(hq_ref[0, rs, hs], hf_ref[0, rs, hs], lb)
            v = hi_ref[0, rs, hs]
            gcum = log_f
            sh = 1
            while sh < c:
                gcum = gcum + jnp.where(row >= sh, pltpu.roll(gcum, sh, 0), 0.0)
                sh *= 2
            gbuf[...] = gcum
            g_last = gbuf[c - 1:c, :]

            o = _dot((q * jnp.exp(gcum)).astype(BF16), state.astype(BF16))
            a = _hgrn_intra(q, k, gcum, gbuf, m_ref, row)
            o = o + _dot(a.astype(BF16), v.astype(BF16))
            o = o + jnp.sum(q * k, axis=-1, keepdims=True) * v
            o_ref[0, rs, hs] = _hgrn_finish(o, gn_ref[h:h + 1, :], hg_ref[0, rs, hs]).astype(o_ref.dtype)

            ks_all = (k * jnp.exp(g_last - gcum)).astype(BF16)
            decay = jnp.transpose(jnp.broadcast_to(jnp.exp(g_last), (HG_DK, HG_DK)))
            state = state * decay + _dot_tn(ks_all, v.astype(BF16))
        s_ref[h] = state

    @pl.when(step == nstep - 1)
    def _():
        so_ref[0] = s_ref[...]


def hgrn_prompt(proj, lb_raw, g_norm, layer, rows):
    b, l, _ = proj.shape
    c = HG_CHUNK
    masks = jnp.asarray(_hgrn_level_masks())
    col = lambda blk: pl.BlockSpec((1, rows, HG_WIDTH), lambda i, n: (i, n, blk))
    return pl.pallas_call(
        functools.partial(_hgrn_prompt_kernel, layer),
        grid=(b, l // rows),
        in_specs=[
            col(HQ_BLK_512), col(HF_BLK_512), col(HI_BLK_512), col(HG_BLK_512),
            pl.BlockSpec((DEPTH, HG_WIDTH), lambda i, n: (0, 0)),
            pl.BlockSpec((None, HG_HEADS, HG_DV), lambda i, n: (layer, 0, 0)),
            pl.BlockSpec(masks.shape, lambda i, n: (0, 0, 0)),
        ],
        out_specs=[
            pl.BlockSpec((1, rows, HG_WIDTH), lambda i, n: (i, n, 0)),
            pl.BlockSpec((1, HG_HEADS, HG_DK, HG_DV), lambda i, n: (i, 0, 0, 0)),
        ],
        out_shape=[jax.ShapeDtypeStruct((b, l, HG_WIDTH), BF16),
                   jax.ShapeDtypeStruct((b, HG_HEADS, HG_DK, HG_DV), F32)],
        scratch_shapes=[pltpu.VMEM((HG_HEADS, HG_DK, HG_DV), F32),
                        pltpu.VMEM((HG_HEADS * (rows // c), c, HG_DK), F32)],
        compiler_params=_cparams("parallel", "arbitrary"),
        name="hgrn_prompt",
    )(proj, proj, proj, proj, lb_raw, g_norm, masks)


def _hgrn_sample_kernel(layer, hq_ref, hf_ref, hi_ref, hg_ref, lbr_ref, gn_ref, s_ref,
                        o_ref, so_ref, q_ref, k_ref, g_ref):
    r8, tt = SAMPLE_ROWS, SAMPLE_T
    nseq = r8 // tt
    row = lax.broadcasted_iota(jnp.int32, (r8, HG_DK), 0)
    t_idx = row % tt
    seq = row // tt

    def pick(ref, s):
        out = ref[s:s + 1, :]
        for e in range(1, nseq):
            out = jnp.where(seq == e, ref[e * tt + s:e * tt + s + 1, :], out)
        return out

    for h in range(HG_HEADS):
        hs = slice(h * HG_DK, (h + 1) * HG_DK)
        lb = _lower_bound(lbr_ref, layer, hs)
        q, log_f, k = _hgrn_gates(hq_ref[:, hs], hf_ref[:, hs], lb)
        gcum = log_f
        sh = 1
        while sh < tt:
            gcum = gcum + jnp.where(t_idx >= sh, pltpu.roll(gcum, sh, 0), 0.0)
            sh *= 2
        q_ref[...] = q
        k_ref[...] = k
        g_ref[...] = gcum
        v = hi_ref[:, hs]
        g_last = pick(g_ref, tt - 1)
        qs = (q * jnp.exp(gcum)).astype(BF16)
        ks_all = k * jnp.exp(g_last - gcum)

        o = jnp.zeros((r8, HG_DV), F32)
        for e in range(nseq):
            state = s_ref[e, h]
            o = jnp.where(seq == e, _dot(qs, state.astype(BF16)), o)
            g_last_e = g_ref[e * tt + tt - 1:e * tt + tt, :]
            decay = jnp.transpose(jnp.broadcast_to(jnp.exp(g_last_e), (HG_DK, HG_DK)))
            ks_e = jnp.where(seq == e, ks_all, 0.0).astype(BF16)
            so_ref[e, h] = state * decay + _dot_tn(ks_e, v.astype(BF16))
        for s in range(tt):
            dec = jnp.exp(jnp.where(t_idx >= s, gcum - pick(g_ref, s), NEG_BIG))
            col = jnp.sum(q * (dec * pick(k_ref, s)), axis=-1, keepdims=True)
            o = o + col * pick(hi_ref.at[:, hs], s)
        o_ref[:, hs] = _hgrn_finish(o, gn_ref[h:h + 1, :], hg_ref[:, hs])


def hgrn_sample(proj, state, lb_raw, g_norm, layer):
    m = proj.shape[0]
    r8 = SAMPLE_ROWS
    nseq = r8 // SAMPLE_T
    col = lambda blk: pl.BlockSpec((r8, HG_WIDTH), lambda i: (i, blk))
    st_in = pl.BlockSpec((None, nseq, HG_HEADS, HG_DK, HG_DV), lambda i: (layer, i, 0, 0, 0))
    st_spec = pl.BlockSpec((nseq, HG_HEADS, HG_DK, HG_DV), lambda i: (i, 0, 0, 0))
    return pl.pallas_call(
        functools.partial(_hgrn_sample_kernel, layer),
        grid=(m // r8,),
        in_specs=[
            col(HQ_BLK_512), col(HF_BLK_512), col(HI_BLK_512), col(HG_BLK_512),
            pl.BlockSpec((DEPTH, HG_WIDTH), lambda i: (0, 0)),
            pl.BlockSpec((None, HG_HEADS, HG_DV), lambda i: (layer, 0, 0)),
            st_in,
        ],
        out_specs=[pl.BlockSpec((r8, HG_WIDTH), lambda i: (i, 0)), st_spec],
        out_shape=[jax.ShapeDtypeStruct((m, HG_WIDTH), F32), jax.ShapeDtypeStruct(state.shape[1:], F32)],
        scratch_shapes=[pltpu.VMEM((r8, HG_DK), F32), pltpu.VMEM((r8, HG_DK), F32), pltpu.VMEM((r8, HG_DK), F32)],
        compiler_params=_cparams("parallel"),
        name="hgrn_sample",
    )(proj, proj, proj, proj, lb_raw, g_norm, state)


def _conv_prompt_kernel(cb_ref, cc_ref, ch_ref, w_ref, y_ref, so_ref, carry_ref):
    n = pl.program_id(1)
    nt = pl.num_programs(1)
    tc = cc_ref.shape[1]

    @pl.when(n == 0)
    def _():
        carry_ref[...] = jnp.zeros_like(carry_ref)

    u = cc_ref[0] * ch_ref[0]
    row = lax.broadcasted_iota(jnp.int32, u.shape, 0)
    last1 = carry_ref[7:8, :]
    last2 = carry_ref[6:7, :]
    um1 = jnp.where(row == 0, last1, pltpu.roll(u, 1, 0))
    um2 = jnp.where(row == 0, last2, jnp.where(row == 1, last1, pltpu.roll(u, 2, 0)))
    y = w_ref[0:1, :] * um2 + w_ref[1:2, :] * um1 + w_ref[2:3, :] * u
    y_ref[0] = (cb_ref[0] * y).astype(y_ref.dtype)
    carry_ref[...] = u[tc - 8:tc]

    @pl.when(n == nt - 1)
    def _():
        so_ref[0] = carry_ref[8 - (CONV_K - 1):8, :]


def conv_prompt(proj, w, layer, tc):
    b, l, _ = proj.shape
    col = lambda blk: pl.BlockSpec((1, tc, CONV_WIDTH), lambda i, n: (i, n, blk))
    return pl.pallas_call(
        _conv_prompt_kernel,
        grid=(b, l // tc),
        in_specs=[col(CB_BLK_512), col(CC_BLK_512), col(CH_BLK_512),
                  pl.BlockSpec((None, CONV_K, CONV_WIDTH), lambda i, n: (layer, 0, 0))],
        out_specs=[pl.BlockSpec((1, tc, CONV_WIDTH), lambda i, n: (i, n, 0)),
                   pl.BlockSpec((1, CONV_K - 1, CONV_WIDTH), lambda i, n: (i, 0, 0))],
        out_shape=[jax.ShapeDtypeStruct((b, l, CONV_WIDTH), BF16),
                   jax.ShapeDtypeStruct((b, CONV_K - 1, CONV_WIDTH), F32)],
        scratch_shapes=[pltpu.VMEM((8, CONV_WIDTH), F32)],
        compiler_params=_cparams("parallel", "arbitrary"),
        name="conv_prompt",
    )(proj, proj, proj, w)


def _conv_sample_kernel(cb_ref, cc_ref, ch_ref, w_ref, st_ref, y_ref, so_ref, u_ref):
    r8, tt = SAMPLE_ROWS, SAMPLE_T
    nseq = r8 // tt
    u = cc_ref[...] * ch_ref[...]
    u_ref[...] = u
    row = lax.broadcasted_iota(jnp.int32, u.shape, 0)
    t_idx = row % tt
    seq = row // tt
    last1 = st_ref[0, 1:2, :]
    last2 = st_ref[0, 0:1, :]
    for e in range(1, nseq):
        last1 = jnp.where(seq == e, st_ref[e, 1:2, :], last1)
        last2 = jnp.where(seq == e, st_ref[e, 0:1, :], last2)
    um1 = jnp.where(t_idx == 0, last1, pltpu.roll(u, 1, 0))
    um2 = jnp.where(t_idx == 0, last2, jnp.where(t_idx == 1, last1, pltpu.roll(u, 2, 0)))
    y = w_ref[0:1, :] * um2 + w_ref[1:2, :] * um1 + w_ref[2:3, :] * u
    y_ref[...] = cb_ref[...] * y
    for e in range(nseq):
        so_ref[e] = u_ref[(e + 1) * tt - (CONV_K - 1):(e + 1) * tt, :]


def conv_sample(proj, state, w, layer):
    m = proj.shape[0]
    r8 = SAMPLE_ROWS
    nseq = r8 // SAMPLE_T
    col = lambda blk: pl.BlockSpec((r8, CONV_WIDTH), lambda i: (i, blk))
    st_in = pl.BlockSpec((None, nseq, CONV_K - 1, CONV_WIDTH), lambda i: (layer, i, 0, 0))
    st_spec = pl.BlockSpec((nseq, CONV_K - 1, CONV_WIDTH), lambda i: (i, 0, 0))
    return pl.pallas_call(
        _conv_sample_kernel,
        grid=(m // r8,),
        in_specs=[col(CB_BLK_512), col(CC_BLK_512), col(CH_BLK_512),
                  pl.BlockSpec((None, CONV_K, CONV_WIDTH), lambda i: (layer, 0, 0)), st_in],
        out_specs=[pl.BlockSpec((r8, CONV_WIDTH), lambda i: (i, 0)), st_spec],
        out_shape=[jax.ShapeDtypeStruct((m, CONV_WIDTH), F32), jax.ShapeDtypeStruct(state.shape[1:], F32)],
        scratch_shapes=[pltpu.VMEM((r8, CONV_WIDTH), F32)],
        compiler_params=_cparams("parallel"),
        name="conv_sample",
    )(proj, proj, proj, w, state)


def kernel(x_prompt, x_sample, cache_attn_k, cache_attn_v, state_hgrn, state_conv, w_in, attn_sink,
           hgrn_lower_bounds, hgrn_norm, conv_w, w_out, norm_mix, norm_ffn, w_gate_up, w_down, norm_final):
    bp, lp, d = x_prompt.shape
    bs, ls, _ = x_sample.shape
    assert ls == SAMPLE_T and (bs * ls) % SAMPLE_ROWS == 0
    lb = cache_attn_k.shape[2]
    mp, ms = bp * lp, bs * ls

    w_in_b = w_in.astype(BF16)
    w_out_b = w_out.astype(BF16)
    w_gu_b = w_gate_up.astype(BF16)
    w_down_b = w_down.astype(BF16)

    tab_p = _rope_tables(jnp.arange(lp, dtype=jnp.int32))
    pos_s = PAST_LEN + jnp.arange(SAMPLE_ROWS, dtype=jnp.int32) % SAMPLE_T
    tab_s = _rope_tables(pos_s)

    xp = x_prompt.reshape(mp, d)
    xs = x_sample.reshape(ms, d)
    kc = cache_attn_k.reshape(DEPTH, bs, lb, KV_WIDTH)
    vc = cache_attn_v.reshape(DEPTH, bs, lb, KV_WIDTH)

    g_mix = norm_mix.reshape(DEPTH, 1, d)
    g_ffn = norm_ffn.reshape(DEPTH, 1, d)

    outs = {name: [] for name in ("kp", "vp", "sp", "cp", "ks", "vs", "ss", "cs")}
    for l in range(DEPTH):
        proj_p = norm_matmul(xp, g_mix, w_in_b, l, 1024, 1024)
        proj_s = norm_matmul(xs, g_mix, w_in_b, l, ms, 1024)
        proj_p3 = proj_p.reshape(bp, lp, IN_WIDTH)

        a_p, k_p, v_p = attn_prompt(proj_p3, attn_sink, tab_p, l)
        a_s, k_s, v_s = attn_sample(proj_s, kc, vc, attn_sink, tab_s, l)
        o_p, s_p = hgrn_prompt(proj_p3, hgrn_lower_bounds, hgrn_norm, l, 2 * HG_CHUNK)
        o_s, s_s = hgrn_sample(proj_s, state_hgrn, hgrn_lower_bounds, hgrn_norm, l)
        c_p, cst_p = conv_prompt(proj_p3, conv_w, l, 512)
        c_s, cst_s = conv_sample(proj_s, state_conv, conv_w, l)

        xp = out_proj(xp, a_p.reshape(mp, ATTN_WIDTH), o_p.reshape(mp, HG_WIDTH),
                      c_p.reshape(mp, CONV_WIDTH), w_out_b, l, 512)
        xs = out_proj(xs, a_s, o_s, c_s, w_out_b, l, ms)
        xp = ffn(xp, g_ffn, w_gu_b, w_down_b, l, 512, 512)
        xs = ffn(xs, g_ffn, w_gu_b, w_down_b, l, ms, 512)

        outs["kp"].append(k_p.reshape(bp, WINDOW, N_KV_HEADS, HEAD_DIM))
        outs["vp"].append(v_p.reshape(bp, WINDOW, N_KV_HEADS, HEAD_DIM))
        outs["sp"].append(s_p)
        outs["cp"].append(cst_p)
        outs["ks"].append(k_s.reshape(bs, lb, N_KV_HEADS, HEAD_DIM))
        outs["vs"].append(v_s.reshape(bs, lb, N_KV_HEADS, HEAD_DIM))
        outs["ss"].append(s_s)
        outs["cs"].append(cst_s)

    y_p = final_norm(xp, norm_final, 512).reshape(bp, lp, d)
    y_s = final_norm(xs, norm_final, ms).reshape(bs, ls, d)
    st = lambda name: jnp.stack(outs[name])
    return (y_p, y_s, st("kp"), st("vp"), st("sp"), st("cp"), st("ks"), st("vs"), st("ss"), st("cs"))
```

```python
import functools

import jax
import jax.numpy as jnp
import numpy as np
from jax import lax
from jax.experimental import pallas as pl
from jax.experimental.pallas import tpu as pltpu

F32 = jnp.float32
BF16 = jnp.bfloat16

D_MODEL = 2048
DEPTH = 4
PAST_LEN = 16384
HEAD_DIM = 64
N_Q_HEADS = 16
N_KV_HEADS = 4
Q_PER_KV = 4
ATTN_WIDTH = 1024
KV_WIDTH = 256
WINDOW = 128
ROPE_THETA = 500000.0
ROT_DIM = 16
ATTN_SCALE = HEAD_DIM ** -0.5
NEG_BIG = -1e30
HG_HEADS = 4
HG_DK = 128
HG_DV = 128
HG_WIDTH = 512
HG_CHUNK = 64
LB_TINY = 1e-30
CONV_K = 3
CONV_WIDTH = 512
D_FF = 5632
IN_WIDTH = 5120
NORM_EPS = 1e-6
SAMPLE_T = 4
SAMPLE_ROWS = 8

Q_BLK_1024 = 0
K_BLK_256 = 4
V_BLK_256 = 5
HQ_BLK_512, HF_BLK_512, HI_BLK_512, HG_BLK_512 = 3, 4, 5, 6
CB_BLK_512, CC_BLK_512, CH_BLK_512 = 7, 8, 9

VMEM_LIMIT = 56 * 1024 * 1024


def _cparams(*sem):
    return pltpu.CompilerParams(dimension_semantics=sem, vmem_limit_bytes=VMEM_LIMIT)


def _rms(x, g):
    ms = jnp.mean(x * x, axis=-1, keepdims=True)
    return x * lax.rsqrt(ms + NORM_EPS) * g


def _sigmoid(x):
    return 1.0 / (1.0 + jnp.exp(-x))


def _dot(a, b):
    return jnp.dot(a, b, preferred_element_type=F32)


def _dot_nt(a, b):
    return lax.dot_general(a, b, (((1,), (1,)), ((), ())), preferred_element_type=F32)


def _dot_tn(a, b):
    return lax.dot_general(a, b, (((0,), (0,)), ((), ())), preferred_element_type=F32)


def _norm_matmul_kernel(x_ref, g_ref, w_ref, o_ref, h_ref):
    @pl.when(pl.program_id(1) == 0)
    def _():
        h_ref[...] = _rms(x_ref[...], g_ref[...]).astype(BF16)

    o_ref[...] = _dot(h_ref[...], w_ref[...])


def norm_matmul(x, g, w, layer, tm, tn):
    m, k = x.shape
    n = w.shape[2]
    return pl.pallas_call(
        _norm_matmul_kernel,
        grid=(m // tm, n // tn),
        in_specs=[
            pl.BlockSpec((tm, k), lambda i, j: (i, 0)),
            pl.BlockSpec((None, 1, k), lambda i, j: (layer, 0, 0)),
            pl.BlockSpec((None, k, tn), lambda i, j: (layer, 0, j)),
        ],
        out_specs=pl.BlockSpec((tm, tn), lambda i, j: (i, j)),
        out_shape=jax.ShapeDtypeStruct((m, n), F32),
        scratch_shapes=[pltpu.VMEM((tm, k), BF16)],
        compiler_params=_cparams("parallel", "arbitrary"),
        name="norm_matmul",
    )(x, g, w)


def _out_proj_kernel(x_ref, a_ref, o_ref, c_ref, wa_ref, wo_ref, wc_ref, y_ref):
    acc = x_ref[...] + _dot(a_ref[...].astype(BF16), wa_ref[...])
    acc = acc + _dot(o_ref[...].astype(BF16), wo_ref[...])
    y_ref[...] = acc + _dot(c_ref[...].astype(BF16), wc_ref[...])


def out_proj(x, a, o, c, w_out, layer, tm):
    m, d = x.shape
    return pl.pallas_call(
        _out_proj_kernel,
        grid=(m // tm,),
        in_specs=[
            pl.BlockSpec((tm, d), lambda i: (i, 0)),
            pl.BlockSpec((tm, ATTN_WIDTH), lambda i: (i, 0)),
            pl.BlockSpec((tm, HG_WIDTH), lambda i: (i, 0)),
            pl.BlockSpec((tm, CONV_WIDTH), lambda i: (i, 0)),
            pl.BlockSpec((None, ATTN_WIDTH, d), lambda i: (layer, 0, 0)),
            pl.BlockSpec((None, HG_WIDTH, d), lambda i: (layer, ATTN_WIDTH // HG_WIDTH, 0)),
            pl.BlockSpec((None, CONV_WIDTH, d), lambda i: (layer, (ATTN_WIDTH + HG_WIDTH) // CONV_WIDTH, 0)),
        ],
        out_specs=pl.BlockSpec((tm, d), lambda i: (i, 0)),
        out_shape=jax.ShapeDtypeStruct((m, d), F32),
        compiler_params=_cparams("parallel"),
        name="out_proj",
    )(x, a, o, c, w_out, w_out, w_out)


def _ffn_kernel(last_layer, x_ref, g_ref, wg_ref, wu_ref, wd_ref, gf_ref, y_ref, h_ref):
    f = pl.program_id(1)

    @pl.when(f == 0)
    def _():
        x = x_ref[...]
        h_ref[...] = _rms(x, g_ref[...]).astype(BF16)
        y_ref[...] = x

    h = h_ref[...]
    gate = _dot(h, wg_ref[...])
    up = _dot(h, wu_ref[...])
    act = (gate * _sigmoid(gate)) * up
    y_ref[...] += _dot(act.astype(BF16), wd_ref[...])

    if last_layer:
        @pl.when(f == pl.num_programs(1) - 1)
        def _():
            y_ref[...] = _rms(y_ref[...], gf_ref[...])


def ffn(x, g, w_gate_up, w_down, g_final, layer, tm, tf):
    m, d = x.shape
    nf = D_FF // tf
    return pl.pallas_call(
        functools.partial(_ffn_kernel, layer == DEPTH - 1),
        grid=(m // tm, nf),
        in_specs=[
            pl.BlockSpec((tm, d), lambda i, f: (i, 0)),
            pl.BlockSpec((None, 1, d), lambda i, f: (layer, 0, 0)),
            pl.BlockSpec((None, d, tf), lambda i, f: (layer, 0, f)),
            pl.BlockSpec((None, d, tf), lambda i, f: (layer, 0, f + nf)),
            pl.BlockSpec((None, tf, d), lambda i, f: (layer, f, 0)),
            pl.BlockSpec((1, d), lambda i, f: (0, 0)),
        ],
        out_specs=pl.BlockSpec((tm, d), lambda i, f: (i, 0)),
        out_shape=jax.ShapeDtypeStruct((m, d), F32),
        scratch_shapes=[pltpu.VMEM((tm, d), BF16)],
        compiler_params=_cparams("parallel", "arbitrary"),
        name="ffn",
    )(x, g, w_gate_up, w_gate_up, w_down, g_final)


def _rope_tables(pos):
    half = ROT_DIM // 2
    inv = ROPE_THETA ** (-jnp.arange(half, dtype=F32) * 2.0 / ROT_DIM)
    ang = pos.astype(F32)[:, None] * inv[None, :]
    cos, sin = jnp.cos(ang), jnp.sin(ang)
    n = pos.shape[0]
    ones = jnp.ones((n, HEAD_DIM - ROT_DIM), F32)
    zeros = jnp.zeros((n, HEAD_DIM - ROT_DIM), F32)
    zh = jnp.zeros((n, half), F32)
    c = jnp.concatenate([cos, cos, ones], axis=-1)
    s_lo = jnp.concatenate([-sin, zh, zeros], axis=-1)
    s_hi = jnp.concatenate([zh, sin, zeros], axis=-1)
    rep = lambda t: jnp.concatenate([t, t], axis=-1)
    return rep(c), rep(s_lo), rep(s_hi)


def _rope(x, c, s_lo, s_hi):
    half = ROT_DIM // 2
    outs = []
    for j in range(x.shape[1] // 128):
        xc = x[:, j * 128:(j + 1) * 128]
        outs.append(xc * c + pltpu.roll(xc, 128 - half, 1) * s_lo + pltpu.roll(xc, half, 1) * s_hi)
    return outs[0] if len(outs) == 1 else jnp.concatenate(outs, axis=1)


def _rope_tables_t(pos):
    half = ROT_DIM // 2
    inv = ROPE_THETA ** (-jnp.arange(half, dtype=F32) * 2.0 / ROT_DIM)
    ang = inv[:, None] * pos.astype(F32)[None, :]
    return jnp.cos(ang), jnp.sin(ang)


def _rope_t(xt, cos_t, sin_t):
    half = ROT_DIM // 2
    pieces = []
    for base in range(0, xt.shape[0], HEAD_DIM):
        x1 = xt[base:base + half]
        x2 = xt[base + half:base + ROT_DIM]
        pieces += [x1 * cos_t - x2 * sin_t, x2 * cos_t + x1 * sin_t, xt[base + ROT_DIM:base + HEAD_DIM]]
    return jnp.concatenate(pieces, axis=0)


def _softmax_sink_pv(s, sk, v):
    m = jnp.maximum(jnp.max(s, axis=-1, keepdims=True), sk)
    p = jnp.exp(s - m)
    denom = jnp.sum(p, axis=-1, keepdims=True) + jnp.exp(sk - m)
    return _dot(p.astype(BF16), v) / denom


def _attn_prompt_kernel(layer, sink_ref, q_ref, k_ref, v_ref, c_ref, slo_ref, shi_ref, ct_ref, st_ref,
                        a_ref, ko_ref, vo_ref, kprev_ref, vtprev_ref):
    n = pl.program_id(1)
    nblk = pl.num_programs(1)
    blk = WINDOW
    lanes = Q_PER_KV * blk

    k = _rope(k_ref[0], c_ref[...], slo_ref[...], shi_ref[...])
    v = v_ref[0]
    vt = v.T
    qt = (_rope_t(q_ref[0].T, ct_ref[...], st_ref[...]) * ATTN_SCALE).astype(BF16)

    key = lax.broadcasted_iota(jnp.int32, (blk, lanes), 0)
    qry = lax.broadcasted_iota(jnp.int32, (blk, lanes), 1) % blk
    newer = key > qry
    zero_rows = jnp.zeros((HEAD_DIM, lanes), BF16)

    def head_out(h, first):
        pair = slice((h // 2) * 2 * HEAD_DIM, (h // 2 + 1) * 2 * HEAD_DIM)
        rows = slice(h * HEAD_DIM, (h + 1) * HEAD_DIM)
        qh = jnp.concatenate(
            [qt[(Q_PER_KV * h + g) * HEAD_DIM:(Q_PER_KV * h + g + 1) * HEAD_DIM] for g in range(Q_PER_KV)], axis=1)
        qz = jnp.concatenate([qh, zero_rows] if h % 2 == 0 else [zero_rows, qh], axis=0)
        sink = jnp.concatenate(
            [jnp.full((1, blk), sink_ref[layer, Q_PER_KV * h + g], F32) for g in range(Q_PER_KV)], axis=1)
        s_cur = _dot(k[:, pair].astype(BF16), qz)
        vt_cur = vt[rows]
        if first:
            f = jnp.where(newer, NEG_BIG, s_cur)
            m = jnp.maximum(jnp.max(f, axis=0, keepdims=True), sink)
            p = jnp.exp(f - m)
            denom = jnp.sum(p, axis=0, keepdims=True) + jnp.exp(sink - m)
            o = _dot(vt_cur.astype(BF16), p.astype(BF16))
        else:
            s_prev = _dot(kprev_ref[:, pair].astype(BF16), qz)
            f = jnp.where(newer, s_prev, s_cur)
            d = jnp.sum(jnp.where(key == qry, s_prev, 0.0), axis=0, keepdims=True)
            m = jnp.maximum(jnp.maximum(jnp.max(f, axis=0, keepdims=True), d), sink)
            p = jnp.exp(f - m)
            pd = jnp.exp(d - m)
            denom = jnp.sum(p, axis=0, keepdims=True) + pd + jnp.exp(sink - m)
            pp = jnp.concatenate([jnp.where(newer, p, 0.0), jnp.where(newer, 0.0, p)], axis=0).astype(BF16)
            vt_prev = vtprev_ref[rows, :]
            o = _dot(jnp.concatenate([vt_prev, vt_cur], axis=1).astype(BF16), pp)
            o = o + pd * jnp.concatenate([vt_prev] * Q_PER_KV, axis=1)
        return o / denom

    def run(first):
        outs = []
        for h in range(N_KV_HEADS):
            o = head_out(h, first)
            outs += [o[:, g * blk:(g + 1) * blk] for g in range(Q_PER_KV)]
        a_ref[0] = jnp.concatenate(outs, axis=0).T.astype(a_ref.dtype)

    @pl.when(n == 0)
    def _():
        run(True)

    @pl.when(n > 0)
    def _():
        run(False)

    kprev_ref[...] = k
    vtprev_ref[...] = vt

    @pl.when(n == nblk - 1)
    def _():
        ko_ref[0] = k
        vo_ref[0] = v


def attn_prompt(proj, sink, tables, tables_t, layer):
    b, l, _ = proj.shape
    blk = WINDOW
    tab_spec = pl.BlockSpec((blk, 128), lambda i, n: (n, 0))
    tab_t_spec = pl.BlockSpec((ROT_DIM // 2, blk), lambda i, n: (0, n))
    kv_out = jax.ShapeDtypeStruct((b, blk, KV_WIDTH), F32)
    return pl.pallas_call(
        functools.partial(_attn_prompt_kernel, layer),
        grid=(b, l // blk),
        in_specs=[
            pl.BlockSpec(memory_space=pltpu.SMEM),
            pl.BlockSpec((1, blk, ATTN_WIDTH), lambda i, n: (i, n, Q_BLK_1024)),
            pl.BlockSpec((1, blk, KV_WIDTH), lambda i, n: (i, n, K_BLK_256)),
            pl.BlockSpec((1, blk, KV_WIDTH), lambda i, n: (i, n, V_BLK_256)),
            tab_spec, tab_spec, tab_spec, tab_t_spec, tab_t_spec,
        ],
        out_specs=[
            pl.BlockSpec((1, blk, ATTN_WIDTH), lambda i, n: (i, n, 0)),
            pl.BlockSpec((1, blk, KV_WIDTH), lambda i, n: (i, 0, 0)),
            pl.BlockSpec((1, blk, KV_WIDTH), lambda i, n: (i, 0, 0)),
        ],
        out_shape=[jax.ShapeDtypeStruct((b, l, ATTN_WIDTH), BF16), kv_out, kv_out],
        scratch_shapes=[pltpu.VMEM((blk, KV_WIDTH), F32), pltpu.VMEM((KV_WIDTH, blk), F32)],
        compiler_params=_cparams("parallel", "arbitrary"),
        name="attn_prompt",
    )(sink, proj, proj, proj, *tables, *tables_t)


def _attn_sample_kernel(layer, sink_ref, q_ref, k_ref, v_ref, kc_ref, vc_ref, c_ref, slo_ref, shi_ref,
                        k_carried_ref, v_carried_ref, a_ref, ko_ref, vo_ref):
    del k_carried_ref, v_carried_ref
    r8 = SAMPLE_ROWS
    nseq = r8 // SAMPLE_T
    lb = kc_ref.shape[1]
    c, s_lo, s_hi = c_ref[...], slo_ref[...], shi_ref[...]
    q = _rope(q_ref[...], c, s_lo, s_hi)
    k = _rope(k_ref[...], c, s_lo, s_hi)
    v = v_ref[...]
    pad = jnp.zeros((lb - r8, KV_WIDTH), F32)
    k_pad = jnp.concatenate([k, pad], axis=0)
    v_pad = jnp.concatenate([v, pad], axis=0)

    rows = Q_PER_KV * r8
    ri = lax.broadcasted_iota(jnp.int32, (rows, 2 * lb), 0) % r8
    t = ri % SAMPLE_T
    kj = lax.broadcasted_iota(jnp.int32, (rows, 2 * lb), 1)
    cj = kj - lb
    row_seq = lax.broadcasted_iota(jnp.int32, (rows, HEAD_DIM), 0) % r8 // SAMPLE_T

    outs = [None] * N_Q_HEADS
    for h in range(N_KV_HEADS):
        hs = slice(h * HEAD_DIM, (h + 1) * HEAD_DIM)
        qh = jnp.concatenate(
            [q[:, (Q_PER_KV * h + g) * HEAD_DIM:(Q_PER_KV * h + g + 1) * HEAD_DIM] for g in range(Q_PER_KV)],
            axis=0).astype(BF16)
        sk = jnp.concatenate(
            [jnp.full((r8, 1), sink_ref[layer, Q_PER_KV * h + g], F32) for g in range(Q_PER_KV)], axis=0)
        o = None
        for e in range(nseq):
            mask = ((kj < lb) & (kj >= t)) | (
                (cj >= e * SAMPLE_T) & (cj < (e + 1) * SAMPLE_T) & (cj - e * SAMPLE_T <= t))
            kk = jnp.concatenate([kc_ref[e, :, hs], k_pad[:, hs]], axis=0).astype(BF16)
            vv = jnp.concatenate([vc_ref[e, :, hs], v_pad[:, hs]], axis=0).astype(BF16)
            s = jnp.where(mask, _dot_nt(qh, kk) * ATTN_SCALE, NEG_BIG)
            oe = _softmax_sink_pv(s, sk, vv)
            o = oe if o is None else jnp.where(row_seq == e, oe, o)
        for g in range(Q_PER_KV):
            outs[Q_PER_KV * h + g] = o[g * r8:(g + 1) * r8]
    a_ref[...] = jnp.concatenate(outs, axis=1)

    row = lax.broadcasted_iota(jnp.int32, (lb, KV_WIDTH), 0)
    for e in range(nseq):
        shift = (lb - SAMPLE_T - e * SAMPLE_T) % lb
        for new_pad, cache_ref, out_ref in ((k_pad, kc_ref, ko_ref), (v_pad, vc_ref, vo_ref)):
            new_rows = new_pad if shift == 0 else pltpu.roll(new_pad, shift, 0)
            old_rows = pltpu.roll(cache_ref[e], lb - SAMPLE_T, 0)
            out_ref[e] = jnp.where(row >= lb - SAMPLE_T, new_rows, old_rows)


def attn_sample(proj, k_cache, v_cache, k_new, v_new, sink, tables, layer):
    m = proj.shape[0]
    r8 = SAMPLE_ROWS
    nseq = r8 // SAMPLE_T
    _, b, lb, _ = k_cache.shape
    tab_spec = pl.BlockSpec((r8, 128), lambda i: (0, 0))
    cache_spec = pl.BlockSpec((None, nseq, lb, KV_WIDTH), lambda i: (layer, i, 0, 0))
    carried = pl.BlockSpec(memory_space=pl.ANY)
    kv_out = jax.ShapeDtypeStruct(k_new.shape, F32)
    return pl.pallas_call(
        functools.partial(_attn_sample_kernel, layer),
        grid=(m // r8,),
        in_specs=[
            pl.BlockSpec(memory_space=pltpu.SMEM),
            pl.BlockSpec((r8, ATTN_WIDTH), lambda i: (i, Q_BLK_1024)),
            pl.BlockSpec((r8, KV_WIDTH), lambda i: (i, K_BLK_256)),
            pl.BlockSpec((r8, KV_WIDTH), lambda i: (i, V_BLK_256)),
            cache_spec, cache_spec,
            tab_spec, tab_spec, tab_spec,
            carried, carried,
        ],
        out_specs=[pl.BlockSpec((r8, ATTN_WIDTH), lambda i: (i, 0)), cache_spec, cache_spec],
        out_shape=[jax.ShapeDtypeStruct((m, ATTN_WIDTH), F32), kv_out, kv_out],
        input_output_aliases={9: 1, 10: 2},
        compiler_params=_cparams("parallel"),
        name="attn_sample",
    )(sink, proj, proj, proj, k_cache, v_cache, *tables, k_new, v_new)


def _lower_bound(lbr_ref, layer, hs):
    rows = [lbr_ref[r:r + 1, hs] for r in range(DEPTH)]
    mx = functools.reduce(jnp.maximum, rows)
    es = [jnp.exp(r - mx) for r in rows]
    tot = functools.reduce(lambda a, b: a + b, es)
    lbp = [e / tot for e in es]
    acc = lbp[0]
    for r in range(1, layer + 1):
        acc = acc + lbp[r]
    return acc - lbp[0]


def _hgrn_gates(hq, hf, lb):
    q = hq * _sigmoid(hq) * (HG_DK ** -0.5)
    e = jnp.exp(-jnp.abs(hf))
    r = 1.0 / (1.0 + e)
    pos = hf >= 0.0
    sig_pos = jnp.where(pos, r, e * r)
    sig_neg = jnp.where(pos, e * r, r)
    log_f = jnp.log(jnp.maximum(lb, LB_TINY) + (1.0 - lb) * sig_pos)
    k = (1.0 - lb) * sig_neg
    return q, log_f, k


def _hgrn_finish(o, g_norm, hg):
    return _rms(o, g_norm) * (hg * _sigmoid(hg))


def _hgrn_levels():
    s = HG_CHUNK // 2
    while s >= 1:
        yield s
        s //= 2


def _hgrn_level_masks():
    t = np.arange(HG_CHUNK)[:, None]
    s = np.arange(HG_CHUNK)[None, :]
    return np.stack([((t % (2 * h) >= h) & (t // (2 * h) == s // (2 * h)) & (s % (2 * h) < h))
                     for h in _hgrn_levels()]).astype(np.float32)


def _hgrn_pair_reference(g_ref, gcum, half, sub8):
    c = HG_CHUNK
    pair = 2 * half
    bcast = lambda r: jnp.broadcast_to(g_ref[r:r + 1, :], (8, HG_DK))
    if pair >= 8:
        return jnp.concatenate([bcast((8 * v // pair) * pair + half - 1) for v in range(c // 8)], axis=0)
    if pair == 4:
        lo = jnp.concatenate([bcast(8 * v + 1) for v in range(c // 8)], axis=0)
        hi = jnp.concatenate([bcast(8 * v + 5) for v in range(c // 8)], axis=0)
        return jnp.where(sub8 < 4, lo, hi)
    return jnp.where(sub8 % 2 == 1, pltpu.roll(gcum, 1, 0), gcum)


def _hgrn_intra(q, k, gcum, g_ref, m_ref, row):
    sub8 = row % 8
    a = jnp.zeros((HG_CHUNK, HG_CHUNK), F32)
    for lvl, half in enumerate(_hgrn_levels()):
        ref = _hgrn_pair_reference(g_ref, gcum, half, sub8)
        right = row % (2 * half) >= half
        scaled = jnp.where(right, q, k) * jnp.exp(jnp.where(right, gcum - ref, ref - gcum))
        scaled = scaled.astype(BF16)
        a = a + _dot_nt(scaled, scaled) * m_ref[lvl]
    return a


def _hgrn_prompt_kernel(layer, hq_ref, hf_ref, hi_ref, hg_ref, lbr_ref, gn_ref, m_ref,
                        o_ref, so_ref, s_ref, g_ref):
    step = pl.program_id(1)
    nstep = pl.num_programs(1)
    c = HG_CHUNK

    @pl.when(step == 0)
    def _():
        s_ref[...] = jnp.zeros_like(s_ref)

    row = lax.broadcasted_iota(jnp.int32, (c, HG_DK), 0)
    for h in range(HG_HEADS):
        hs = slice(h * HG_DK, (h + 1) * HG_DK)
        lb = _lower_bound(lbr_ref, layer, hs)
        state = s_ref[h]
        for cc in range(hq_ref.shape[1] // c):
            rs = slice(cc * c, (cc + 1) * c)
            gbuf = g_ref.at[h * (hq_ref.shape[1] // c) + cc]
            q, log_f, k = _hgrn_gates(hq_ref[0, rs, hs], hf_ref[0, rs, hs], lb)
            v = hi_ref[0, rs, hs]
            gcum = log_f
            sh = 1
            while sh < c:
                gcum = gcum + jnp.where(row >= sh, pltpu.roll(gcum, sh, 0), 0.0)
                sh *= 2
            gbuf[...] = gcum
            g_last = gbuf[c - 1:c, :]

            o = _dot((q * jnp.exp(gcum)).astype(BF16), state.astype(BF16))
            a = _hgrn_intra(q, k, gcum, gbuf, m_ref, row)
            o = o + _dot(a.astype(BF16), v.astype(BF16))
            o = o + jnp.sum(q * k, axis=-1, keepdims=True) * v
            o_ref[0, rs, hs] = _hgrn_finish(o, gn_ref[h:h + 1, :], hg_ref[0, rs, hs]).astype(o_ref.dtype)

            ks_all = (k * jnp.exp(g_last - gcum)).astype(BF16)
            decay = jnp.transpose(jnp.broadcast_to(jnp.exp(g_last), (HG_DK, HG_DK)))
            state = state * decay + _dot_tn(ks_all, v.astype(BF16))
        s_ref[h] = state

    @pl.when(step == nstep - 1)
    def _():
        so_ref[0] = s_ref[...]


def hgrn_prompt(proj, lb_raw, g_norm, layer, rows):
    b, l, _ = proj.shape
    c = HG_CHUNK
    masks = jnp.asarray(_hgrn_level_masks())
    col = lambda blk: pl.BlockSpec((1, rows, HG_WIDTH), lambda i, n: (i, n, blk))
    return pl.pallas_call(
        functools.partial(_hgrn_prompt_kernel, layer),
        grid=(b, l // rows),
        in_specs=[
            col(HQ_BLK_512), col(HF_BLK_512), col(HI_BLK_512), col(HG_BLK_512),
            pl.BlockSpec((DEPTH, HG_WIDTH), lambda i, n: (0, 0)),
            pl.BlockSpec((None, HG_HEADS, HG_DV), lambda i, n: (layer, 0, 0)),
            pl.BlockSpec(masks.shape, lambda i, n: (0, 0, 0)),
        ],
        out_specs=[
            pl.BlockSpec((1, rows, HG_WIDTH), lambda i, n: (i, n, 0)),
            pl.BlockSpec((1, HG_HEADS, HG_DK, HG_DV), lambda i, n: (i, 0, 0, 0)),
        ],
        out_shape=[jax.ShapeDtypeStruct((b, l, HG_WIDTH), BF16),
                   jax.ShapeDtypeStruct((b, HG_HEADS, HG_DK, HG_DV), F32)],
        scratch_shapes=[pltpu.VMEM((HG_HEADS, HG_DK, HG_DV), F32),
                        pltpu.VMEM((HG_HEADS * (rows // c), c, HG_DK), F32)],
        compiler_params=_cparams("parallel", "arbitrary"),
        name="hgrn_prompt",
    )(proj, proj, proj, proj, lb_raw, g_norm, masks)


def _hgrn_sample_kernel(layer, hq_ref, hf_ref, hi_ref, hg_ref, lbr_ref, gn_ref, s_ref, carried_ref,
                        o_ref, so_ref, q_ref, k_ref, g_ref):
    del carried_ref
    r8, tt = SAMPLE_ROWS, SAMPLE_T
    nseq = r8 // tt
    row = lax.broadcasted_iota(jnp.int32, (r8, HG_DK), 0)
    t_idx = row % tt
    seq = row // tt

    def pick(ref, s):
        out = ref[s:s + 1, :]
        for e in range(1, nseq):
            out = jnp.where(seq == e, ref[e * tt + s:e * tt + s + 1, :], out)
        return out

    for h in range(HG_HEADS):
        hs = slice(h * HG_DK, (h + 1) * HG_DK)
        lb = _lower_bound(lbr_ref, layer, hs)
        q, log_f, k = _hgrn_gates(hq_ref[:, hs], hf_ref[:, hs], lb)
        gcum = log_f
        sh = 1
        while sh < tt:
            gcum = gcum + jnp.where(t_idx >= sh, pltpu.roll(gcum, sh, 0), 0.0)
            sh *= 2
        q_ref[...] = q
        k_ref[...] = k
        g_ref[...] = gcum
        v = hi_ref[:, hs]
        g_last = pick(g_ref, tt - 1)
        qs = (q * jnp.exp(gcum)).astype(BF16)
        ks_all = k * jnp.exp(g_last - gcum)

        o = jnp.zeros((r8, HG_DV), F32)
        for e in range(nseq):
            state = s_ref[e, h]
            o = jnp.where(seq == e, _dot(qs, state.astype(BF16)), o)
            g_last_e = g_ref[e * tt + tt - 1:e * tt + tt, :]
            decay = jnp.transpose(jnp.broadcast_to(jnp.exp(g_last_e), (HG_DK, HG_DK)))
            ks_e = jnp.where(seq == e, ks_all, 0.0).astype(BF16)
            so_ref[e, h] = state * decay + _dot_tn(ks_e, v.astype(BF16))
        for s in range(tt):
            dec = jnp.exp(jnp.where(t_idx >= s, gcum - pick(g_ref, s), NEG_BIG))
            col = jnp.sum(q * (dec * pick(k_ref, s)), axis=-1, keepdims=True)
            o = o + col * pick(hi_ref.at[:, hs], s)
        o_ref[:, hs] = _hgrn_finish(o, gn_ref[h:h + 1, :], hg_ref[:, hs])


def hgrn_sample(proj, state, state_new, lb_raw, g_norm, layer):
    m = proj.shape[0]
    r8 = SAMPLE_ROWS
    nseq = r8 // SAMPLE_T
    col = lambda blk: pl.BlockSpec((r8, HG_WIDTH), lambda i: (i, blk))
    st_spec = pl.BlockSpec((None, nseq, HG_HEADS, HG_DK, HG_DV), lambda i: (layer, i, 0, 0, 0))
    return pl.pallas_call(
        functools.partial(_hgrn_sample_kernel, layer),
        grid=(m // r8,),
        in_specs=[
            col(HQ_BLK_512), col(HF_BLK_512), col(HI_BLK_512), col(HG_BLK_512),
            pl.BlockSpec((DEPTH, HG_WIDTH), lambda i: (0, 0)),
            pl.BlockSpec((None, HG_HEADS, HG_DV), lambda i: (layer, 0, 0)),
            st_spec,
            pl.BlockSpec(memory_space=pl.ANY),
        ],
        out_specs=[pl.BlockSpec((r8, HG_WIDTH), lambda i: (i, 0)), st_spec],
        out_shape=[jax.ShapeDtypeStruct((m, HG_WIDTH), F32), jax.ShapeDtypeStruct(state_new.shape, F32)],
        input_output_aliases={7: 1},
        scratch_shapes=[pltpu.VMEM((r8, HG_DK), F32), pltpu.VMEM((r8, HG_DK), F32), pltpu.VMEM((r8, HG_DK), F32)],
        compiler_params=_cparams("parallel"),
        name="hgrn_sample",
    )(proj, proj, proj, proj, lb_raw, g_norm, state, state_new)


def _conv_prompt_kernel(cb_ref, cc_ref, ch_ref, w_ref, y_ref, so_ref, carry_ref):
    n = pl.program_id(1)
    nt = pl.num_programs(1)
    tc = cc_ref.shape[1]

    @pl.when(n == 0)
    def _():
        carry_ref[...] = jnp.zeros_like(carry_ref)

    u = cc_ref[0] * ch_ref[0]
    row = lax.broadcasted_iota(jnp.int32, u.shape, 0)
    last1 = carry_ref[7:8, :]
    last2 = carry_ref[6:7, :]
    um1 = jnp.where(row == 0, last1, pltpu.roll(u, 1, 0))
    um2 = jnp.where(row == 0, last2, jnp.where(row == 1, last1, pltpu.roll(u, 2, 0)))
    y = w_ref[0:1, :] * um2 + w_ref[1:2, :] * um1 + w_ref[2:3, :] * u
    y_ref[0] = (cb_ref[0] * y).astype(y_ref.dtype)
    carry_ref[...] = u[tc - 8:tc]

    @pl.when(n == nt - 1)
    def _():
        so_ref[0] = carry_ref[8 - (CONV_K - 1):8, :]


def conv_prompt(proj, w, layer, tc):
    b, l, _ = proj.shape
    col = lambda blk: pl.BlockSpec((1, tc, CONV_WIDTH), lambda i, n: (i, n, blk))
    return pl.pallas_call(
        _conv_prompt_kernel,
        grid=(b, l // tc),
        in_specs=[col(CB_BLK_512), col(CC_BLK_512), col(CH_BLK_512),
                  pl.BlockSpec((None, CONV_K, CONV_WIDTH), lambda i, n: (layer, 0, 0))],
        out_specs=[pl.BlockSpec((1, tc, CONV_WIDTH), lambda i, n: (i, n, 0)),
                   pl.BlockSpec((1, CONV_K - 1, CONV_WIDTH), lambda i, n: (i, 0, 0))],
        out_shape=[jax.ShapeDtypeStruct((b, l, CONV_WIDTH), BF16),
                   jax.ShapeDtypeStruct((b, CONV_K - 1, CONV_WIDTH), F32)],
        scratch_shapes=[pltpu.VMEM((8, CONV_WIDTH), F32)],
        compiler_params=_cparams("parallel", "arbitrary"),
        name="conv_prompt",
    )(proj, proj, proj, w)


def _conv_sample_kernel(cb_ref, cc_ref, ch_ref, w_ref, st_ref, y_ref, so_ref, u_ref):
    r8, tt = SAMPLE_ROWS, SAMPLE_T
    nseq = r8 // tt
    u = cc_ref[...] * ch_ref[...]
    u_ref[...] = u
    row = lax.broadcasted_iota(jnp.int32, u.shape, 0)
    t_idx = row % tt
    seq = row // tt
    last1 = st_ref[0, 1:2, :]
    last2 = st_ref[0, 0:1, :]
    for e in range(1, nseq):
        last1 = jnp.where(seq == e, st_ref[e, 1:2, :], last1)
        last2 = jnp.where(seq == e, st_ref[e, 0:1, :], last2)
    um1 = jnp.where(t_idx == 0, last1, pltpu.roll(u, 1, 0))
    um2 = jnp.where(t_idx == 0, last2, jnp.where(t_idx == 1, last1, pltpu.roll(u, 2, 0)))
    y = w_ref[0:1, :] * um2 + w_ref[1:2, :] * um1 + w_ref[2:3, :] * u
    y_ref[...] = cb_ref[...] * y
    for e in range(nseq):
        so_ref[e] = u_ref[(e + 1) * tt - (CONV_K - 1):(e + 1) * tt, :]


def conv_sample(proj, state, w, layer):
    m = proj.shape[0]
    r8 = SAMPLE_ROWS
    nseq = r8 // SAMPLE_T
    col = lambda blk: pl.BlockSpec((r8, CONV_WIDTH), lambda i: (i, blk))
    st_in = pl.BlockSpec((None, nseq, CONV_K - 1, CONV_WIDTH), lambda i: (layer, i, 0, 0))
    st_spec = pl.BlockSpec((nseq, CONV_K - 1, CONV_WIDTH), lambda i: (i, 0, 0))
    return pl.pallas_call(
        _conv_sample_kernel,
        grid=(m // r8,),
        in_specs=[col(CB_BLK_512), col(CC_BLK_512), col(CH_BLK_512),
                  pl.BlockSpec((None, CONV_K, CONV_WIDTH), lambda i: (layer, 0, 0)), st_in],
        out_specs=[pl.BlockSpec((r8, CONV_WIDTH), lambda i: (i, 0)), st_spec],
        out_shape=[jax.ShapeDtypeStruct((m, CONV_WIDTH), F32), jax.ShapeDtypeStruct(state.shape[1:], F32)],
        scratch_shapes=[pltpu.VMEM((r8, CONV_WIDTH), F32)],
        compiler_params=_cparams("parallel"),
        name="conv_sample",
    )(proj, proj, proj, w, state)


def kernel(x_prompt, x_sample, cache_attn_k, cache_attn_v, state_hgrn, state_conv, w_in, attn_sink,
           hgrn_lower_bounds, hgrn_norm, conv_w, w_out, norm_mix, norm_ffn, w_gate_up, w_down, norm_final):
    bp, lp, d = x_prompt.shape
    bs, ls, _ = x_sample.shape
    assert ls == SAMPLE_T and (bs * ls) % SAMPLE_ROWS == 0
    lb = cache_attn_k.shape[2]
    mp, ms = bp * lp, bs * ls

    w_in_b = w_in.astype(BF16)
    w_out_b = w_out.astype(BF16)
    w_gu_b = w_gate_up.astype(BF16)
    w_down_b = w_down.astype(BF16)

    pos_p = jnp.arange(lp, dtype=jnp.int32)
    tab_p = _rope_tables(pos_p)
    tab_pt = _rope_tables_t(pos_p)
    pos_s = PAST_LEN + jnp.arange(SAMPLE_ROWS, dtype=jnp.int32) % SAMPLE_T
    tab_s = _rope_tables(pos_s)

    xp = x_prompt.reshape(mp, d)
    xs = x_sample.reshape(ms, d)
    kc = cache_attn_k.reshape(DEPTH, bs, lb, KV_WIDTH)
    vc = cache_attn_v.reshape(DEPTH, bs, lb, KV_WIDTH)

    g_mix = norm_mix.reshape(DEPTH, 1, d)
    g_ffn = norm_ffn.reshape(DEPTH, 1, d)
    g_final = norm_final.reshape(1, d)

    k_s = jnp.zeros(kc.shape, F32)
    v_s = jnp.zeros(vc.shape, F32)
    s_s = jnp.zeros(state_hgrn.shape, F32)

    outs = {name: [] for name in ("kp", "vp", "sp", "cp", "cs")}
    for l in range(DEPTH):
        proj_p = norm_matmul(xp, g_mix, w_in_b, l, 1024, 1024)
        proj_s = norm_matmul(xs, g_mix, w_in_b, l, ms, 1024)
        proj_p3 = proj_p.reshape(bp, lp, IN_WIDTH)

        a_p, k_p, v_p = attn_prompt(proj_p3, attn_sink, tab_p, tab_pt, l)
        a_s, k_s, v_s = attn_sample(proj_s, kc, vc, k_s, v_s, attn_sink, tab_s, l)
        o_p, s_p = hgrn_prompt(proj_p3, hgrn_lower_bounds, hgrn_norm, l, 2 * HG_CHUNK)
        o_s, s_s = hgrn_sample(proj_s, state_hgrn, s_s, hgrn_lower_bounds, hgrn_norm, l)
        c_p, cst_p = conv_prompt(proj_p3, conv_w, l, 512)
        c_s, cst_s = conv_sample(proj_s, state_conv, conv_w, l)

        xp = out_proj(xp, a_p.reshape(mp, ATTN_WIDTH), o_p.reshape(mp, HG_WIDTH),
                      c_p.reshape(mp, CONV_WIDTH), w_out_b, l, 512)
        xs = out_proj(xs, a_s, o_s, c_s, w_out_b, l, ms)
        xp = ffn(xp, g_ffn, w_gu_b, w_down_b, g_final, l, 512, 512)
        xs = ffn(xs, g_ffn, w_gu_b, w_down_b, g_final, l, ms, 512)

        outs["kp"].append(k_p.reshape(bp, WINDOW, N_KV_HEADS, HEAD_DIM))
        outs["vp"].append(v_p.reshape(bp, WINDOW, N_KV_HEADS, HEAD_DIM))
        outs["sp"].append(s_p)
        outs["cp"].append(cst_p)
        outs["cs"].append(cst_s)

    st = lambda name: jnp.stack(outs[name])
    kv_shape = (DEPTH, bs, lb, N_KV_HEADS, HEAD_DIM)
    return (xp.reshape(bp, lp, d), xs.reshape(bs, ls, d), st("kp"), st("vp"), st("sp"), st("cp"),
            k_s.reshape(kv_shape), v_s.reshape(kv_shape), s_s, st("cs"))
```

```python
import functools

import jax
import jax.numpy as jnp
import numpy as np
from jax import lax
from jax.experimental import pallas as pl
from jax.experimental.pallas import tpu as pltpu

F32 = jnp.float32
BF16 = jnp.bfloat16

D_MODEL = 2048
DEPTH = 4
PAST_LEN = 16384
HEAD_DIM = 64
N_Q_HEADS = 16
N_KV_HEADS = 4
Q_PER_KV = 4
ATTN_WIDTH = 1024
KV_WIDTH = 256
WINDOW = 128
ROPE_THETA = 500000.0
ROT_DIM = 16
ATTN_SCALE = HEAD_DIM ** -0.5
NEG_BIG = -1e30
HG_HEADS = 4
HG_DK = 128
HG_DV = 128
HG_WIDTH = 512
HG_CHUNK = 64
LB_TINY = 1e-30
CONV_K = 3
CONV_WIDTH = 512
D_FF = 5632
IN_WIDTH = 5120
NORM_EPS = 1e-6
SAMPLE_T = 4
SAMPLE_ROWS = 8

Q_BLK_1024 = 0
K_BLK_256 = 4
V_BLK_256 = 5
HQ_BLK_512, HF_BLK_512, HI_BLK_512, HG_BLK_512 = 3, 4, 5, 6
CB_BLK_512, CC_BLK_512, CH_BLK_512 = 7, 8, 9

VMEM_LIMIT = 56 * 1024 * 1024


def _cparams(*sem):
    return pltpu.CompilerParams(dimension_semantics=sem, vmem_limit_bytes=VMEM_LIMIT)


def _rms(x, g):
    ms = jnp.mean(x * x, axis=-1, keepdims=True)
    return x * lax.rsqrt(ms + NORM_EPS) * g


def _sigmoid(x):
    return 1.0 / (1.0 + jnp.exp(-x))


def _dot(a, b):
    return jnp.dot(a, b, preferred_element_type=F32)


def _dot_nt(a, b):
    return lax.dot_general(a, b, (((1,), (1,)), ((), ())), preferred_element_type=F32)


def _dot_tn(a, b):
    return lax.dot_general(a, b, (((0,), (0,)), ((), ())), preferred_element_type=F32)


def _in_proj_sample_kernel(x_ref, g_ref, w_ref, o_ref, wb_ref, h_ref):
    @pl.when(pl.program_id(0) == 0)
    def _():
        h_ref[...] = _rms(x_ref[...], g_ref[...]).astype(BF16)

    wb = w_ref[...].astype(BF16)
    wb_ref[...] = wb
    o_ref[...] = _dot(h_ref[...], wb)


def in_proj_sample(x, g, w, layer, tn):
    m, k = x.shape
    n = w.shape[2]
    return pl.pallas_call(
        _in_proj_sample_kernel,
        grid=(n // tn,),
        in_specs=[
            pl.BlockSpec((m, k), lambda j: (0, 0)),
            pl.BlockSpec((None, 1, k), lambda j: (layer, 0, 0)),
            pl.BlockSpec((None, k, tn), lambda j: (layer, 0, j)),
        ],
        out_specs=[pl.BlockSpec((m, tn), lambda j: (0, j)), pl.BlockSpec((k, tn), lambda j: (0, j))],
        out_shape=[jax.ShapeDtypeStruct((m, n), F32), jax.ShapeDtypeStruct((k, n), BF16)],
        scratch_shapes=[pltpu.VMEM((m, k), BF16)],
        compiler_params=_cparams("arbitrary"),
        name="in_proj_sample",
    )(x, g, w)


def _in_proj_kernel(x_ref, g_ref, w_ref, o_ref, h_ref):
    @pl.when(pl.program_id(1) == 0)
    def _():
        h_ref[...] = _rms(x_ref[...], g_ref[...]).astype(BF16)

    o_ref[...] = _dot(h_ref[...], w_ref[...])


def in_proj(x, g, wb, layer, tm, tn):
    m, k = x.shape
    n = wb.shape[1]
    return pl.pallas_call(
        _in_proj_kernel,
        grid=(m // tm, n // tn),
        in_specs=[
            pl.BlockSpec((tm, k), lambda i, j: (i, 0)),
            pl.BlockSpec((None, 1, k), lambda i, j: (layer, 0, 0)),
            pl.BlockSpec((k, tn), lambda i, j: (0, j)),
        ],
        out_specs=pl.BlockSpec((tm, tn), lambda i, j: (i, j)),
        out_shape=jax.ShapeDtypeStruct((m, n), F32),
        scratch_shapes=[pltpu.VMEM((tm, k), BF16)],
        compiler_params=_cparams("parallel", "arbitrary"),
        name="in_proj",
    )(x, g, wb)


def _mix_dot(a, o, c, w):
    acc = _dot(a.astype(BF16), w[0:ATTN_WIDTH])
    acc = acc + _dot(o.astype(BF16), w[ATTN_WIDTH:ATTN_WIDTH + HG_WIDTH])
    return acc + _dot(c.astype(BF16), w[ATTN_WIDTH + HG_WIDTH:])


def _out_proj_sample_kernel(x_ref, a_ref, o_ref, c_ref, w_ref, y_ref, wb_ref):
    wb = w_ref[...].astype(BF16)
    wb_ref[...] = wb
    y_ref[...] = x_ref[...] + _mix_dot(a_ref[...], o_ref[...], c_ref[...], wb)


def out_proj_sample(x, a, o, c, w_out, layer, tn):
    m, d = x.shape
    kw = w_out.shape[1]
    full = lambda width: pl.BlockSpec((m, width), lambda j: (0, 0))
    return pl.pallas_call(
        _out_proj_sample_kernel,
        grid=(d // tn,),
        in_specs=[
            pl.BlockSpec((m, tn), lambda j: (0, j)),
            full(ATTN_WIDTH), full(HG_WIDTH), full(CONV_WIDTH),
            pl.BlockSpec((None, kw, tn), lambda j: (layer, 0, j)),
        ],
        out_specs=[pl.BlockSpec((m, tn), lambda j: (0, j)), pl.BlockSpec((kw, tn), lambda j: (0, j))],
        out_shape=[jax.ShapeDtypeStruct((m, d), F32), jax.ShapeDtypeStruct((kw, d), BF16)],
        compiler_params=_cparams("parallel"),
        name="out_proj_sample",
    )(x, a, o, c, w_out)


def _conv_taps(u, prev1, prev2, w_ref):
    return w_ref[0:1, :] * prev2 + w_ref[1:2, :] * prev1 + w_ref[2:3, :] * u


def _out_proj_kernel(tiles_per_seq, x_ref, a_ref, o_ref, cb_ref, cc_ref, ch_ref, cw_ref, w_ref,
                     y_ref, cs_ref, carry_ref):
    i = pl.program_id(0)
    tm = cc_ref.shape[0]

    @pl.when(i % tiles_per_seq == 0)
    def _():
        carry_ref[...] = jnp.zeros_like(carry_ref)

    u = cc_ref[...] * ch_ref[...]
    row = lax.broadcasted_iota(jnp.int32, u.shape, 0)
    last1 = carry_ref[7:8, :]
    last2 = carry_ref[6:7, :]
    um1 = jnp.where(row == 0, last1, pltpu.roll(u, 1, 0))
    um2 = jnp.where(row == 0, last2, jnp.where(row == 1, last1, pltpu.roll(u, 2, 0)))
    c = cb_ref[...] * _conv_taps(u, um1, um2, cw_ref)
    carry_ref[...] = u[tm - 8:tm]

    y_ref[...] = x_ref[...] + _mix_dot(a_ref[...], o_ref[...], c, w_ref)

    @pl.when(i % tiles_per_seq == tiles_per_seq - 1)
    def _():
        cs_ref[0] = carry_ref[8 - (CONV_K - 1):8, :]


def out_proj(x, a, o, proj, conv_w, wb, layer, seq_len, tm):
    m, d = x.shape
    tiles_per_seq = seq_len // tm
    col = lambda blk: pl.BlockSpec((tm, CONV_WIDTH), lambda i: (i, blk))
    return pl.pallas_call(
        functools.partial(_out_proj_kernel, tiles_per_seq),
        grid=(m // tm,),
        in_specs=[
            pl.BlockSpec((tm, d), lambda i: (i, 0)),
            pl.BlockSpec((tm, ATTN_WIDTH), lambda i: (i, 0)),
            pl.BlockSpec((tm, HG_WIDTH), lambda i: (i, 0)),
            col(CB_BLK_512), col(CC_BLK_512), col(CH_BLK_512),
            pl.BlockSpec((None, CONV_K, CONV_WIDTH), lambda i: (layer, 0, 0)),
            pl.BlockSpec(wb.shape, lambda i: (0, 0)),
        ],
        out_specs=[pl.BlockSpec((tm, d), lambda i: (i, 0)),
                   pl.BlockSpec((1, CONV_K - 1, CONV_WIDTH), lambda i: (i // tiles_per_seq, 0, 0))],
        out_shape=[jax.ShapeDtypeStruct((m, d), F32),
                   jax.ShapeDtypeStruct((m // seq_len, CONV_K - 1, CONV_WIDTH), F32)],
        scratch_shapes=[pltpu.VMEM((8, CONV_WIDTH), F32)],
        compiler_params=_cparams("arbitrary"),
        name="out_proj",
    )(x, a, o, proj, proj, proj, conv_w, wb)


def _ffn_step(h_ref, wg, wu, wd, y_ref):
    h = h_ref[...]
    gate = _dot(h, wg)
    up = _dot(h, wu)
    act = (gate * _sigmoid(gate)) * up
    y_ref[...] += _dot(act.astype(BF16), wd)


def _ffn_begin(x_ref, g_ref, y_ref, h_ref):
    x = x_ref[...]
    h_ref[...] = _rms(x, g_ref[...]).astype(BF16)
    y_ref[...] = x


def _ffn_sample_kernel(last_layer, x_ref, g_ref, wg_ref, wu_ref, wd_ref, gf_ref,
                       y_ref, wgb_ref, wub_ref, wdb_ref, h_ref):
    f = pl.program_id(0)

    @pl.when(f == 0)
    def _():
        _ffn_begin(x_ref, g_ref, y_ref, h_ref)

    wg = wg_ref[...].astype(BF16)
    wu = wu_ref[...].astype(BF16)
    wd = wd_ref[...].astype(BF16)
    wgb_ref[...] = wg
    wub_ref[...] = wu
    wdb_ref[...] = wd
    _ffn_step(h_ref, wg, wu, wd, y_ref)

    if last_layer:
        @pl.when(f == pl.num_programs(0) - 1)
        def _():
            y_ref[...] = _rms(y_ref[...], gf_ref[...])


def ffn_sample(x, g, w_gate_up, w_down, g_final, layer, tf):
    m, d = x.shape
    nf = D_FF // tf
    return pl.pallas_call(
        functools.partial(_ffn_sample_kernel, layer == DEPTH - 1),
        grid=(nf,),
        in_specs=[
            pl.BlockSpec((m, d), lambda f: (0, 0)),
            pl.BlockSpec((None, 1, d), lambda f: (layer, 0, 0)),
            pl.BlockSpec((None, d, tf), lambda f: (layer, 0, f)),
            pl.BlockSpec((None, d, tf), lambda f: (layer, 0, f + nf)),
            pl.BlockSpec((None, tf, d), lambda f: (layer, f, 0)),
            pl.BlockSpec((1, d), lambda f: (0, 0)),
        ],
        out_specs=[
            pl.BlockSpec((m, d), lambda f: (0, 0)),
            pl.BlockSpec((d, tf), lambda f: (0, f)),
            pl.BlockSpec((d, tf), lambda f: (0, f)),
            pl.BlockSpec((tf, d), lambda f: (f, 0)),
        ],
        out_shape=[jax.ShapeDtypeStruct((m, d), F32), jax.ShapeDtypeStruct((d, D_FF), BF16),
                   jax.ShapeDtypeStruct((d, D_FF), BF16), jax.ShapeDtypeStruct((D_FF, d), BF16)],
        scratch_shapes=[pltpu.VMEM((m, d), BF16)],
        compiler_params=_cparams("arbitrary"),
        name="ffn_sample",
    )(x, g, w_gate_up, w_gate_up, w_down, g_final)


def _ffn_kernel(last_layer, x_ref, g_ref, wg_ref, wu_ref, wd_ref, gf_ref, y_ref, h_ref):
    f = pl.program_id(1)

    @pl.when(f == 0)
    def _():
        _ffn_begin(x_ref, g_ref, y_ref, h_ref)

    _ffn_step(h_ref, wg_ref[...], wu_ref[...], wd_ref[...], y_ref)

    if last_layer:
        @pl.when(f == pl.num_programs(1) - 1)
        def _():
            y_ref[...] = _rms(y_ref[...], gf_ref[...])


def ffn(x, g, wgb, wub, wdb, g_final, layer, tm, tf):
    m, d = x.shape
    return pl.pallas_call(
        functools.partial(_ffn_kernel, layer == DEPTH - 1),
        grid=(m // tm, D_FF // tf),
        in_specs=[
            pl.BlockSpec((tm, d), lambda i, f: (i, 0)),
            pl.BlockSpec((None, 1, d), lambda i, f: (layer, 0, 0)),
            pl.BlockSpec((d, tf), lambda i, f: (0, f)),
            pl.BlockSpec((d, tf), lambda i, f: (0, f)),
            pl.BlockSpec((tf, d), lambda i, f: (f, 0)),
            pl.BlockSpec((1, d), lambda i, f: (0, 0)),
        ],
        out_specs=pl.BlockSpec((tm, d), lambda i, f: (i, 0)),
        out_shape=jax.ShapeDtypeStruct((m, d), F32),
        scratch_shapes=[pltpu.VMEM((tm, d), BF16)],
        compiler_params=_cparams("parallel", "arbitrary"),
        name="ffn",
    )(x, g, wgb, wub, wdb, g_final)


def _rope_tables(pos):
    half = ROT_DIM // 2
    inv = ROPE_THETA ** (-jnp.arange(half, dtype=F32) * 2.0 / ROT_DIM)
    ang = pos.astype(F32)[:, None] * inv[None, :]
    cos, sin = jnp.cos(ang), jnp.sin(ang)
    n = pos.shape[0]
    ones = jnp.ones((n, HEAD_DIM - ROT_DIM), F32)
    zeros = jnp.zeros((n, HEAD_DIM - ROT_DIM), F32)
    zh = jnp.zeros((n, half), F32)
    c = jnp.concatenate([cos, cos, ones], axis=-1)
    s_lo = jnp.concatenate([-sin, zh, zeros], axis=-1)
    s_hi = jnp.concatenate([zh, sin, zeros], axis=-1)
    rep = lambda t: jnp.concatenate([t, t], axis=-1)
    return rep(c), rep(s_lo), rep(s_hi)


def _rope(x, c, s_lo, s_hi):
    half = ROT_DIM // 2
    outs = []
    for j in range(x.shape[1] // 128):
        xc = x[:, j * 128:(j + 1) * 128]
        outs.append(xc * c + pltpu.roll(xc, 128 - half, 1) * s_lo + pltpu.roll(xc, half, 1) * s_hi)
    return outs[0] if len(outs) == 1 else jnp.concatenate(outs, axis=1)


def _rope_tables_t(pos):
    half = ROT_DIM // 2
    inv = ROPE_THETA ** (-jnp.arange(half, dtype=F32) * 2.0 / ROT_DIM)
    ang = inv[:, None] * pos.astype(F32)[None, :]
    return jnp.cos(ang), jnp.sin(ang)


def _rope_t(xt, cos_t, sin_t):
    half = ROT_DIM // 2
    pieces = []
    for base in range(0, xt.shape[0], HEAD_DIM):
        x1 = xt[base:base + half]
        x2 = xt[base + half:base + ROT_DIM]
        pieces += [x1 * cos_t - x2 * sin_t, x2 * cos_t + x1 * sin_t, xt[base + ROT_DIM:base + HEAD_DIM]]
    return jnp.concatenate(pieces, axis=0)


def _softmax_sink_pv(s, sk, v):
    m = jnp.maximum(jnp.max(s, axis=-1, keepdims=True), sk)
    p = jnp.exp(s - m)
    denom = jnp.sum(p, axis=-1, keepdims=True) + jnp.exp(sk - m)
    return _dot(p.astype(BF16), v) / denom


def _attn_prompt_kernel(layer, sink_ref, q_ref, k_ref, v_ref, c_ref, slo_ref, shi_ref, ct_ref, st_ref,
                        a_ref, ko_ref, vo_ref, kprev_ref, vtprev_ref):
    n = pl.program_id(1)
    nblk = pl.num_programs(1)
    blk = WINDOW
    lanes = Q_PER_KV * blk

    k = _rope(k_ref[0], c_ref[...], slo_ref[...], shi_ref[...])
    v = v_ref[0]
    vt = v.T
    qt = (_rope_t(q_ref[0].T, ct_ref[...], st_ref[...]) * ATTN_SCALE).astype(BF16)

    key = lax.broadcasted_iota(jnp.int32, (blk, lanes), 0)
    qry = lax.broadcasted_iota(jnp.int32, (blk, lanes), 1) % blk
    newer = key > qry
    zero_rows = jnp.zeros((HEAD_DIM, lanes), BF16)

    def head_out(h, first):
        pair = slice((h // 2) * 2 * HEAD_DIM, (h // 2 + 1) * 2 * HEAD_DIM)
        rows = slice(h * HEAD_DIM, (h + 1) * HEAD_DIM)
        qh = jnp.concatenate(
            [qt[(Q_PER_KV * h + g) * HEAD_DIM:(Q_PER_KV * h + g + 1) * HEAD_DIM] for g in range(Q_PER_KV)], axis=1)
        qz = jnp.concatenate([qh, zero_rows] if h % 2 == 0 else [zero_rows, qh], axis=0)
        sink = jnp.concatenate(
            [jnp.full((1, blk), sink_ref[layer, Q_PER_KV * h + g], F32) for g in range(Q_PER_KV)], axis=1)
        s_cur = _dot(k[:, pair].astype(BF16), qz)
        vt_cur = vt[rows]
        if first:
            f = jnp.where(newer, NEG_BIG, s_cur)
            m = jnp.maximum(jnp.max(f, axis=0, keepdims=True), sink)
            p = jnp.exp(f - m)
            denom = jnp.sum(p, axis=0, keepdims=True) + jnp.exp(sink - m)
            o = _dot(vt_cur.astype(BF16), p.astype(BF16))
        else:
            s_prev = _dot(kprev_ref[:, pair].astype(BF16), qz)
            f = jnp.where(newer, s_prev, s_cur)
            d = jnp.sum(jnp.where(key == qry, s_prev, 0.0), axis=0, keepdims=True)
            m = jnp.maximum(jnp.maximum(jnp.max(f, axis=0, keepdims=True), d), sink)
            p = jnp.exp(f - m)
            pd = jnp.exp(d - m)
            denom = jnp.sum(p, axis=0, keepdims=True) + pd + jnp.exp(sink - m)
            pp = jnp.concatenate([jnp.where(newer, p, 0.0), jnp.where(newer, 0.0, p)], axis=0).astype(BF16)
            vt_prev = vtprev_ref[rows, :]
            o = _dot(jnp.concatenate([vt_prev, vt_cur], axis=1).astype(BF16), pp)
            o = o + pd * jnp.concatenate([vt_prev] * Q_PER_KV, axis=1)
        return o / denom

    def run(first):
        outs = []
        for h in range(N_KV_HEADS):
            o = head_out(h, first)
            outs += [o[:, g * blk:(g + 1) * blk] for g in range(Q_PER_KV)]
        a_ref[0] = jnp.concatenate(outs, axis=0).T.astype(a_ref.dtype)

    @pl.when(n == 0)
    def _():
        run(True)

    @pl.when(n > 0)
    def _():
        run(False)

    kprev_ref[...] = k
    vtprev_ref[...] = vt

    @pl.when(n == nblk - 1)
    def _():
        ko_ref[0] = k
        vo_ref[0] = v


def attn_prompt(proj, sink, tables, tables_t, layer):
    b, l, _ = proj.shape
    blk = WINDOW
    tab_spec = pl.BlockSpec((blk, 128), lambda i, n: (n, 0))
    tab_t_spec = pl.BlockSpec((ROT_DIM // 2, blk), lambda i, n: (0, n))
    kv_out = jax.ShapeDtypeStruct((b, blk, KV_WIDTH), F32)
    return pl.pallas_call(
        functools.partial(_attn_prompt_kernel, layer),
        grid=(b, l // blk),
        in_specs=[
            pl.BlockSpec(memory_space=pltpu.SMEM),
            pl.BlockSpec((1, blk, ATTN_WIDTH), lambda i, n: (i, n, Q_BLK_1024)),
            pl.BlockSpec((1, blk, KV_WIDTH), lambda i, n: (i, n, K_BLK_256)),
            pl.BlockSpec((1, blk, KV_WIDTH), lambda i, n: (i, n, V_BLK_256)),
            tab_spec, tab_spec, tab_spec, tab_t_spec, tab_t_spec,
        ],
        out_specs=[
            pl.BlockSpec((1, blk, ATTN_WIDTH), lambda i, n: (i, n, 0)),
            pl.BlockSpec((1, blk, KV_WIDTH), lambda i, n: (i, 0, 0)),
            pl.BlockSpec((1, blk, KV_WIDTH), lambda i, n: (i, 0, 0)),
        ],
        out_shape=[jax.ShapeDtypeStruct((b, l, ATTN_WIDTH), BF16), kv_out, kv_out],
        scratch_shapes=[pltpu.VMEM((blk, KV_WIDTH), F32), pltpu.VMEM((KV_WIDTH, blk), F32)],
        compiler_params=_cparams("parallel", "arbitrary"),
        name="attn_prompt",
    )(sink, proj, proj, proj, *tables, *tables_t)


def _attn_sample_kernel(layer, sink_ref, q_ref, k_ref, v_ref, kc_ref, vc_ref, c_ref, slo_ref, shi_ref,
                        k_carried_ref, v_carried_ref, a_ref, ko_ref, vo_ref):
    del k_carried_ref, v_carried_ref
    r8 = SAMPLE_ROWS
    nseq = r8 // SAMPLE_T
    lb = kc_ref.shape[1]
    c, s_lo, s_hi = c_ref[...], slo_ref[...], shi_ref[...]
    q = _rope(q_ref[...], c, s_lo, s_hi)
    k = _rope(k_ref[...], c, s_lo, s_hi)
    v = v_ref[...]
    pad = jnp.zeros((lb - r8, KV_WIDTH), F32)
    k_pad = jnp.concatenate([k, pad], axis=0)
    v_pad = jnp.concatenate([v, pad], axis=0)

    rows = Q_PER_KV * r8
    ri = lax.broadcasted_iota(jnp.int32, (rows, 2 * lb), 0) % r8
    t = ri % SAMPLE_T
    kj = lax.broadcasted_iota(jnp.int32, (rows, 2 * lb), 1)
    cj = kj - lb
    row_seq = lax.broadcasted_iota(jnp.int32, (rows, HEAD_DIM), 0) % r8 // SAMPLE_T

    outs = [None] * N_Q_HEADS
    for h in range(N_KV_HEADS):
        hs = slice(h * HEAD_DIM, (h + 1) * HEAD_DIM)
        qh = jnp.concatenate(
            [q[:, (Q_PER_KV * h + g) * HEAD_DIM:(Q_PER_KV * h + g + 1) * HEAD_DIM] for g in range(Q_PER_KV)],
            axis=0).astype(BF16)
        sk = jnp.concatenate(
            [jnp.full((r8, 1), sink_ref[layer, Q_PER_KV * h + g], F32) for g in range(Q_PER_KV)], axis=0)
        o = None
        for e in range(nseq):
            mask = ((kj < lb) & (kj >= t)) | (
                (cj >= e * SAMPLE_T) & (cj < (e + 1) * SAMPLE_T) & (cj - e * SAMPLE_T <= t))
            kk = jnp.concatenate([kc_ref[e, :, hs], k_pad[:, hs]], axis=0).astype(BF16)
            vv = jnp.concatenate([vc_ref[e, :, hs], v_pad[:, hs]], axis=0).astype(BF16)
            s = jnp.where(mask, _dot_nt(qh, kk) * ATTN_SCALE, NEG_BIG)
            oe = _softmax_sink_pv(s, sk, vv)
            o = oe if o is None else jnp.where(row_seq == e, oe, o)
        for g in range(Q_PER_KV):
            outs[Q_PER_KV * h + g] = o[g * r8:(g + 1) * r8]
    a_ref[...] = jnp.concatenate(outs, axis=1)

    row = lax.broadcasted_iota(jnp.int32, (lb, KV_WIDTH), 0)
    for e in range(nseq):
        shift = (lb - SAMPLE_T - e * SAMPLE_T) % lb
        for new_pad, cache_ref, out_ref in ((k_pad, kc_ref, ko_ref), (v_pad, vc_ref, vo_ref)):
            new_rows = new_pad if shift == 0 else pltpu.roll(new_pad, shift, 0)
            old_rows = pltpu.roll(cache_ref[e], lb - SAMPLE_T, 0)
            out_ref[e] = jnp.where(row >= lb - SAMPLE_T, new_rows, old_rows)


def attn_sample(proj, k_cache, v_cache, k_new, v_new, sink, tables, layer):
    m = proj.shape[0]
    r8 = SAMPLE_ROWS
    nseq = r8 // SAMPLE_T
    _, b, lb, _ = k_cache.shape
    tab_spec = pl.BlockSpec((r8, 128), lambda i: (0, 0))
    cache_spec = pl.BlockSpec((None, nseq, lb, KV_WIDTH), lambda i: (layer, i, 0, 0))
    carried = pl.BlockSpec(memory_space=pl.ANY)
    kv_out = jax.ShapeDtypeStruct(k_new.shape, F32)
    return pl.pallas_call(
        functools.partial(_attn_sample_kernel, layer),
        grid=(m // r8,),
        in_specs=[
            pl.BlockSpec(memory_space=pltpu.SMEM),
            pl.BlockSpec((r8, ATTN_WIDTH), lambda i: (i, Q_BLK_1024)),
            pl.BlockSpec((r8, KV_WIDTH), lambda i: (i, K_BLK_256)),
            pl.BlockSpec((r8, KV_WIDTH), lambda i: (i, V_BLK_256)),
            cache_spec, cache_spec,
            tab_spec, tab_spec, tab_spec,
            carried, carried,
        ],
        out_specs=[pl.BlockSpec((r8, ATTN_WIDTH), lambda i: (i, 0)), cache_spec, cache_spec],
        out_shape=[jax.ShapeDtypeStruct((m, ATTN_WIDTH), F32), kv_out, kv_out],
        input_output_aliases={9: 1, 10: 2},
        compiler_params=_cparams("parallel"),
        name="attn_sample",
    )(sink, proj, proj, proj, k_cache, v_cache, *tables, k_new, v_new)


def _lower_bound(lbr_ref, layer, hs):
    rows = [lbr_ref[r:r + 1, hs] for r in range(DEPTH)]
    mx = functools.reduce(jnp.maximum, rows)
    es = [jnp.exp(r - mx) for r in rows]
    tot = functools.reduce(lambda a, b: a + b, es)
    lbp = [e / tot for e in es]
    acc = lbp[0]
    for r in range(1, layer + 1):
        acc = acc + lbp[r]
    return acc - lbp[0]


def _hgrn_gates(hq, hf, lb):
    q = hq * _sigmoid(hq) * (HG_DK ** -0.5)
    e = jnp.exp(-jnp.abs(hf))
    r = 1.0 / (1.0 + e)
    pos = hf >= 0.0
    sig_pos = jnp.where(pos, r, e * r)
    sig_neg = jnp.where(pos, e * r, r)
    log_f = jnp.log(jnp.maximum(lb, LB_TINY) + (1.0 - lb) * sig_pos)
    k = (1.0 - lb) * sig_neg
    return q, log_f, k


def _hgrn_finish(o, g_norm, hg):
    return _rms(o, g_norm) * (hg * _sigmoid(hg))


def _hgrn_levels():
    s = HG_CHUNK // 2
    while s >= 1:
        yield s
        s //= 2


def _hgrn_level_masks():
    t = np.arange(HG_CHUNK)[:, None]
    s = np.arange(HG_CHUNK)[None, :]
    return np.stack([((t % (2 * h) >= h) & (t // (2 * h) == s // (2 * h)) & (s % (2 * h) < h))
                     for h in _hgrn_levels()]).astype(np.float32)


def _hgrn_pair_reference(g_ref, gcum, half, sub8):
    c = HG_CHUNK
    pair = 2 * half
    bcast = lambda r: jnp.broadcast_to(g_ref[r:r + 1, :], (8, HG_DK))
    if pair >= 8:
        return jnp.concatenate([bcast((8 * v // pair) * pair + half - 1) for v in range(c // 8)], axis=0)
    if pair == 4:
        lo = jnp.concatenate([bcast(8 * v + 1) for v in range(c // 8)], axis=0)
        hi = jnp.concatenate([bcast(8 * v + 5) for v in range(c // 8)], axis=0)
        return jnp.where(sub8 < 4, lo, hi)
    return jnp.where(sub8 % 2 == 1, pltpu.roll(gcum, 1, 0), gcum)


def _hgrn_intra(q, k, gcum, g_ref, m_ref, row):
    sub8 = row % 8
    a = jnp.zeros((HG_CHUNK, HG_CHUNK), F32)
    for lvl, half in enumerate(_hgrn_levels()):
        ref = _hgrn_pair_reference(g_ref, gcum, half, sub8)
        right = row % (2 * half) >= half
        scaled = jnp.where(right, q, k) * jnp.exp(jnp.where(right, gcum - ref, ref - gcum))
        scaled = scaled.astype(BF16)
        a = a + _dot_nt(scaled, scaled) * m_ref[lvl]
    return a


def _hgrn_prompt_kernel(layer, hq_ref, hf_ref, hi_ref, hg_ref, lbr_ref, gn_ref, m_ref,
                        o_ref, so_ref, s_ref, g_ref):
    step = pl.program_id(1)
    nstep = pl.num_programs(1)
    c = HG_CHUNK

    @pl.when(step == 0)
    def _():
        s_ref[...] = jnp.zeros_like(s_ref)

    row = lax.broadcasted_iota(jnp.int32, (c, HG_DK), 0)
    for h in range(HG_HEADS):
        hs = slice(h * HG_DK, (h + 1) * HG_DK)
        lb = _lower_bound(lbr_ref, layer, hs)
        state = s_ref[h]
        for cc in range(hq_ref.shape[1] // c):
            rs = slice(cc * c, (cc + 1) * c)
            gbuf = g_ref.at[h * (hq_ref.shape[1] // c) + cc]
            q, log_f, k = _hgrn_gates(hq_ref[0, rs, hs], hf_ref[0, rs, hs], lb)
            v = hi_ref[0, rs, hs]
            gcum = log_f
            sh = 1
            while sh < c:
                gcum = gcum + jnp.where(row >= sh, pltpu.roll(gcum, sh, 0), 0.0)
                sh *= 2
            gbuf[...] = gcum
            g_last = gbuf[c - 1:c, :]

            o = _dot((q * jnp.exp(gcum)).astype(BF16), state.astype(BF16))
            a = _hgrn_intra(q, k, gcum, gbuf, m_ref, row)
            o = o + _dot(a.astype(BF16), v.astype(BF16))
            o = o + jnp.sum(q * k, axis=-1, keepdims=True) * v
            o_ref[0, rs, hs] = _hgrn_finish(o, gn_ref[h:h + 1, :], hg_ref[0, rs, hs]).astype(o_ref.dtype)

            ks_all = (k * jnp.exp(g_last - gcum)).astype(BF16)
            decay = jnp.transpose(jnp.broadcast_to(jnp.exp(g_last), (HG_DK, HG_DK)))
            state = state * decay + _dot_tn(ks_all, v.astype(BF16))
        s_ref[h] = state

    @pl.when(step == nstep - 1)
    def _():
        so_ref[0] = s_ref[...]


def hgrn_prompt(proj, lb_raw, g_norm, layer, rows):
    b, l, _ = proj.shape
    c = HG_CHUNK
    masks = jnp.asarray(_hgrn_level_masks())
    col = lambda blk: pl.BlockSpec((1, rows, HG_WIDTH), lambda i, n: (i, n, blk))
    return pl.pallas_call(
        functools.partial(_hgrn_prompt_kernel, layer),
        grid=(b, l // rows),
        in_specs=[
            col(HQ_BLK_512), col(HF_BLK_512), col(HI_BLK_512), col(HG_BLK_512),
            pl.BlockSpec((DEPTH, HG_WIDTH), lambda i, n: (0, 0)),
            pl.BlockSpec((None, HG_HEADS, HG_DV), lambda i, n: (layer, 0, 0)),
            pl.BlockSpec(masks.shape, lambda i, n: (0, 0, 0)),
        ],
        out_specs=[
            pl.BlockSpec((1, rows, HG_WIDTH), lambda i, n: (i, n, 0)),
            pl.BlockSpec((1, HG_HEADS, HG_DK, HG_DV), lambda i, n: (i, 0, 0, 0)),
        ],
        out_shape=[jax.ShapeDtypeStruct((b, l, HG_WIDTH), BF16),
                   jax.ShapeDtypeStruct((b, HG_HEADS, HG_DK, HG_DV), F32)],
        scratch_shapes=[pltpu.VMEM((HG_HEADS, HG_DK, HG_DV), F32),
                        pltpu.VMEM((HG_HEADS * (rows // c), c, HG_DK), F32)],
        compiler_params=_cparams("parallel", "arbitrary"),
        name="hgrn_prompt",
    )(proj, proj, proj, proj, lb_raw, g_norm, masks)


def _hgrn_sample_kernel(layer, hq_ref, hf_ref, hi_ref, hg_ref, lbr_ref, gn_ref, s_ref, carried_ref,
                        o_ref, so_ref, q_ref, k_ref, g_ref):
    del carried_ref
    r8, tt = SAMPLE_ROWS, SAMPLE_T
    nseq = r8 // tt
    row = lax.broadcasted_iota(jnp.int32, (r8, HG_DK), 0)
    t_idx = row % tt
    seq = row // tt

    def pick(ref, s):
        out = ref[s:s + 1, :]
        for e in range(1, nseq):
            out = jnp.where(seq == e, ref[e * tt + s:e * tt + s + 1, :], out)
        return out

    for h in range(HG_HEADS):
        hs = slice(h * HG_DK, (h + 1) * HG_DK)
        lb = _lower_bound(lbr_ref, layer, hs)
        q, log_f, k = _hgrn_gates(hq_ref[:, hs], hf_ref[:, hs], lb)
        gcum = log_f
        sh = 1
        while sh < tt:
            gcum = gcum + jnp.where(t_idx >= sh, pltpu.roll(gcum, sh, 0), 0.0)
            sh *= 2
        q_ref[...] = q
        k_ref[...] = k
        g_ref[...] = gcum
        v = hi_ref[:, hs]
        g_last = pick(g_ref, tt - 1)
        qs = (q * jnp.exp(gcum)).astype(BF16)
        ks_all = k * jnp.exp(g_last - gcum)

        o = jnp.zeros((r8, HG_DV), F32)
        for e in range(nseq):
            state = s_ref[e, h]
            o = jnp.where(seq == e, _dot(qs, state.astype(BF16)), o)
            g_last_e = g_ref[e * tt + tt - 1:e * tt + tt, :]
            decay = jnp.transpose(jnp.broadcast_to(jnp.exp(g_last_e), (HG_DK, HG_DK)))
            ks_e = jnp.where(seq == e, ks_all, 0.0).astype(BF16)
            so_ref[e, h] = state * decay + _dot_tn(ks_e, v.astype(BF16))
        for s in range(tt):
            dec = jnp.exp(jnp.where(t_idx >= s, gcum - pick(g_ref, s), NEG_BIG))
            col = jnp.sum(q * (dec * pick(k_ref, s)), axis=-1, keepdims=True)
            o = o + col * pick(hi_ref.at[:, hs], s)
        o_ref[:, hs] = _hgrn_finish(o, gn_ref[h:h + 1, :], hg_ref[:, hs])


def hgrn_sample(proj, state, state_new, lb_raw, g_norm, layer):
    m = proj.shape[0]
    r8 = SAMPLE_ROWS
    nseq = r8 // SAMPLE_T
    col = lambda blk: pl.BlockSpec((r8, HG_WIDTH), lambda i: (i, blk))
    st_spec = pl.BlockSpec((None, nseq, HG_HEADS, HG_DK, HG_DV), lambda i: (layer, i, 0, 0, 0))
    return pl.pallas_call(
        functools.partial(_hgrn_sample_kernel, layer),
        grid=(m // r8,),
        in_specs=[
            col(HQ_BLK_512), col(HF_BLK_512), col(HI_BLK_512), col(HG_BLK_512),
            pl.BlockSpec((DEPTH, HG_WIDTH), lambda i: (0, 0)),
            pl.BlockSpec((None, HG_HEADS, HG_DV), lambda i: (layer, 0, 0)),
            st_spec,
            pl.BlockSpec(memory_space=pl.ANY),
        ],
        out_specs=[pl.BlockSpec((r8, HG_WIDTH), lambda i: (i, 0)), st_spec],
        out_shape=[jax.ShapeDtypeStruct((m, HG_WIDTH), F32), jax.ShapeDtypeStruct(state_new.shape, F32)],
        input_output_aliases={7: 1},
        scratch_shapes=[pltpu.VMEM((r8, HG_DK), F32), pltpu.VMEM((r8, HG_DK), F32), pltpu.VMEM((r8, HG_DK), F32)],
        compiler_params=_cparams("parallel"),
        name="hgrn_sample",
    )(proj, proj, proj, proj, lb_raw, g_norm, state, state_new)


def _conv_sample_kernel(cb_ref, cc_ref, ch_ref, w_ref, st_ref, y_ref, so_ref, u_ref):
    r8, tt = SAMPLE_ROWS, SAMPLE_T
    nseq = r8 // tt
    u = cc_ref[...] * ch_ref[...]
    u_ref[...] = u
    row = lax.broadcasted_iota(jnp.int32, u.shape, 0)
    t_idx = row % tt
    seq = row // tt
    last1 = st_ref[0, 1:2, :]
    last2 = st_ref[0, 0:1, :]
    for e in range(1, nseq):
        last1 = jnp.where(seq == e, st_ref[e, 1:2, :], last1)
        last2 = jnp.where(seq == e, st_ref[e, 0:1, :], last2)
    um1 = jnp.where(t_idx == 0, last1, pltpu.roll(u, 1, 0))
    um2 = jnp.where(t_idx == 0, last2, jnp.where(t_idx == 1, last1, pltpu.roll(u, 2, 0)))
    y_ref[...] = cb_ref[...] * _conv_taps(u, um1, um2, w_ref)
    for e in range(nseq):
        so_ref[e] = u_ref[(e + 1) * tt - (CONV_K - 1):(e + 1) * tt, :]


def conv_sample(proj, state, w, layer):
    m = proj.shape[0]
    r8 = SAMPLE_ROWS
    nseq = r8 // SAMPLE_T
    col = lambda blk: pl.BlockSpec((r8, CONV_WIDTH), lambda i: (i, blk))
    st_in = pl.BlockSpec((None, nseq, CONV_K - 1, CONV_WIDTH), lambda i: (layer, i, 0, 0))
    st_spec = pl.BlockSpec((nseq, CONV_K - 1, CONV_WIDTH), lambda i: (i, 0, 0))
    return pl.pallas_call(
        _conv_sample_kernel,
        grid=(m // r8,),
        in_specs=[col(CB_BLK_512), col(CC_BLK_512), col(CH_BLK_512),
                  pl.BlockSpec((None, CONV_K, CONV_WIDTH), lambda i: (layer, 0, 0)), st_in],
        out_specs=[pl.BlockSpec((r8, CONV_WIDTH), lambda i: (i, 0)), st_spec],
        out_shape=[jax.ShapeDtypeStruct((m, CONV_WIDTH), F32), jax.ShapeDtypeStruct(state.shape[1:], F32)],
        scratch_shapes=[pltpu.VMEM((r8, CONV_WIDTH), F32)],
        compiler_params=_cparams("parallel"),
        name="conv_sample",
    )(proj, proj, proj, w, state)


def kernel(x_prompt, x_sample, cache_attn_k, cache_attn_v, state_hgrn, state_conv, w_in, attn_sink,
           hgrn_lower_bounds, hgrn_norm, conv_w, w_out, norm_mix, norm_ffn, w_gate_up, w_down, norm_final):
    bp, lp, d = x_prompt.shape
    bs, ls, _ = x_sample.shape
    assert ls == SAMPLE_T and (bs * ls) % SAMPLE_ROWS == 0
    lb = cache_attn_k.shape[2]
    mp, ms = bp * lp, bs * ls

    pos_p = jnp.arange(lp, dtype=jnp.int32)
    tab_p = _rope_tables(pos_p)
    tab_pt = _rope_tables_t(pos_p)
    pos_s = PAST_LEN + jnp.arange(SAMPLE_ROWS, dtype=jnp.int32) % SAMPLE_T
    tab_s = _rope_tables(pos_s)

    xp = x_prompt.reshape(mp, d)
    xs = x_sample.reshape(ms, d)
    kc = cache_attn_k.reshape(DEPTH, bs, lb, KV_WIDTH)
    vc = cache_attn_v.reshape(DEPTH, bs, lb, KV_WIDTH)

    g_mix = norm_mix.reshape(DEPTH, 1, d)
    g_ffn = norm_ffn.reshape(DEPTH, 1, d)
    g_final = norm_final.reshape(1, d)

    k_s = jnp.zeros(kc.shape, F32)
    v_s = jnp.zeros(vc.shape, F32)
    s_s = jnp.zeros(state_hgrn.shape, F32)

    outs = {name: [] for name in ("kp", "vp", "sp", "cp", "cs")}
    for l in range(DEPTH):
        proj_s, w_in_b = in_proj_sample(xs, g_mix, w_in, l, 1024)
        a_s, k_s, v_s = attn_sample(proj_s, kc, vc, k_s, v_s, attn_sink, tab_s, l)
        o_s, s_s = hgrn_sample(proj_s, state_hgrn, s_s, hgrn_lower_bounds, hgrn_norm, l)
        c_s, cst_s = conv_sample(proj_s, state_conv, conv_w, l)
        xs, w_out_b = out_proj_sample(xs, a_s, o_s, c_s, w_out, l, 512)
        xs, w_gate_b, w_up_b, w_down_b = ffn_sample(xs, g_ffn, w_gate_up, w_down, g_final, l, 512)

        proj_p = in_proj(xp, g_mix, w_in_b, l, 1024, 1024)
        proj_p3 = proj_p.reshape(bp, lp, IN_WIDTH)
        a_p, k_p, v_p = attn_prompt(proj_p3, attn_sink, tab_p, tab_pt, l)
        o_p, s_p = hgrn_prompt(proj_p3, hgrn_lower_bounds, hgrn_norm, l, 2 * HG_CHUNK)
        xp, cst_p = out_proj(xp, a_p.reshape(mp, ATTN_WIDTH), o_p.reshape(mp, HG_WIDTH), proj_p,
                             conv_w, w_out_b, l, lp, 512)
        xp = ffn(xp, g_ffn, w_gate_b, w_up_b, w_down_b, g_final, l, 512, 512)

        outs["kp"].append(k_p.reshape(bp, WINDOW, N_KV_HEADS, HEAD_DIM))
        outs["vp"].append(v_p.reshape(bp, WINDOW, N_KV_HEADS, HEAD_DIM))
        outs["sp"].append(s_p)
        outs["cp"].append(cst_p)
        outs["cs"].append(cst_s)

    st = lambda name: jnp.stack(outs[name])
    kv_shape = (DEPTH, bs, lb, N_KV_HEADS, HEAD_DIM)
    return (xp.reshape(bp, lp, d), xs.reshape(bs, ls, d), st("kp"), st("vp"), st("sp"), st("cp"),
            k_s.reshape(kv_shape), v_s.reshape(kv_shape), s_s, st("cs"))
```

```python
import functools

import jax
import jax.numpy as jnp
import numpy as np
from jax import lax
from jax.experimental import pallas as pl
from jax.experimental.pallas import tpu as pltpu

F32 = jnp.float32
BF16 = jnp.bfloat16

D_MODEL = 2048
DEPTH = 4
PAST_LEN = 16384
HEAD_DIM = 64
N_Q_HEADS = 16
N_KV_HEADS = 4
Q_PER_KV = 4
ATTN_WIDTH = 1024
KV_WIDTH = 256
WINDOW = 128
ROPE_THETA = 500000.0
ROT_DIM = 16
ATTN_SCALE = HEAD_DIM ** -0.5
NEG_BIG = -1e30
HG_HEADS = 4
HG_DK = 128
HG_DV = 128
HG_WIDTH = 512
HG_CHUNK = 64
LB_TINY = 1e-30
CONV_K = 3
CONV_WIDTH = 512
D_FF = 5632
IN_WIDTH = 5120
NORM_EPS = 1e-6
SAMPLE_T = 4
SAMPLE_ROWS = 8

Q_BLK_1024 = 0
K_BLK_256 = 4
V_BLK_256 = 5
HQ_BLK_512, HF_BLK_512, HI_BLK_512, HG_BLK_512 = 3, 4, 5, 6
CB_BLK_512, CC_BLK_512, CH_BLK_512 = 7, 8, 9

VMEM_LIMIT = 56 * 1024 * 1024


def _cparams(*sem):
    return pltpu.CompilerParams(dimension_semantics=sem, vmem_limit_bytes=VMEM_LIMIT)


def _rms(x, g):
    ms = jnp.mean(x * x, axis=-1, keepdims=True)
    return x * lax.rsqrt(ms + NORM_EPS) * g


def _sigmoid(x):
    return 1.0 / (1.0 + jnp.exp(-x))


def _dot(a, b):
    return jnp.dot(a, b, preferred_element_type=F32)


def _dot_nt(a, b):
    return lax.dot_general(a, b, (((1,), (1,)), ((), ())), preferred_element_type=F32)


def _dot_tn(a, b):
    return lax.dot_general(a, b, (((0,), (0,)), ((), ())), preferred_element_type=F32)


def _in_proj_sample_kernel(x_ref, g_ref, w_ref, o_ref, wb_ref, h_ref):
    @pl.when(pl.program_id(0) == 0)
    def _():
        h_ref[...] = _rms(x_ref[...], g_ref[...]).astype(BF16)

    wb = w_ref[...].astype(BF16)
    wb_ref[...] = wb
    o_ref[...] = _dot(h_ref[...], wb)


def in_proj_sample(x, g, w, layer, tn):
    m, k = x.shape
    n = w.shape[2]
    return pl.pallas_call(
        _in_proj_sample_kernel,
        grid=(n // tn,),
        in_specs=[
            pl.BlockSpec((m, k), lambda j: (0, 0)),
            pl.BlockSpec((None, 1, k), lambda j: (layer, 0, 0)),
            pl.BlockSpec((None, k, tn), lambda j: (layer, 0, j)),
        ],
        out_specs=[pl.BlockSpec((m, tn), lambda j: (0, j)), pl.BlockSpec((k, tn), lambda j: (0, j))],
        out_shape=[jax.ShapeDtypeStruct((m, n), F32), jax.ShapeDtypeStruct((k, n), BF16)],
        scratch_shapes=[pltpu.VMEM((m, k), BF16)],
        compiler_params=_cparams("arbitrary"),
        name="in_proj_sample",
    )(x, g, w)


def _in_proj_kernel(x_ref, g_ref, w_ref, o_ref, h_ref):
    @pl.when(pl.program_id(1) == 0)
    def _():
        h_ref[...] = _rms(x_ref[...], g_ref[...]).astype(BF16)

    o_ref[...] = _dot(h_ref[...], w_ref[...])


def in_proj(x, g, wb, layer, tm, tn):
    m, k = x.shape
    n = wb.shape[1]
    return pl.pallas_call(
        _in_proj_kernel,
        grid=(m // tm, n // tn),
        in_specs=[
            pl.BlockSpec((tm, k), lambda i, j: (i, 0)),
            pl.BlockSpec((None, 1, k), lambda i, j: (layer, 0, 0)),
            pl.BlockSpec((k, tn), lambda i, j: (0, j)),
        ],
        out_specs=pl.BlockSpec((tm, tn), lambda i, j: (i, j)),
        out_shape=jax.ShapeDtypeStruct((m, n), F32),
        scratch_shapes=[pltpu.VMEM((tm, k), BF16)],
        compiler_params=_cparams("parallel", "arbitrary"),
        name="in_proj",
    )(x, g, wb)


def _mix_dot(a, o, c, w):
    acc = _dot(a.astype(BF16), w[0:ATTN_WIDTH])
    acc = acc + _dot(o.astype(BF16), w[ATTN_WIDTH:ATTN_WIDTH + HG_WIDTH])
    return acc + _dot(c.astype(BF16), w[ATTN_WIDTH + HG_WIDTH:])


def _out_proj_sample_kernel(x_ref, a_ref, o_ref, c_ref, w_ref, y_ref, wb_ref):
    wb = w_ref[...].astype(BF16)
    wb_ref[...] = wb
    y_ref[...] = x_ref[...] + _mix_dot(a_ref[...], o_ref[...], c_ref[...], wb)


def out_proj_sample(x, a, o, c, w_out, layer, tn):
    m, d = x.shape
    kw = w_out.shape[1]
    full = lambda width: pl.BlockSpec((m, width), lambda j: (0, 0))
    return pl.pallas_call(
        _out_proj_sample_kernel,
        grid=(d // tn,),
        in_specs=[
            pl.BlockSpec((m, tn), lambda j: (0, j)),
            full(ATTN_WIDTH), full(HG_WIDTH), full(CONV_WIDTH),
            pl.BlockSpec((None, kw, tn), lambda j: (layer, 0, j)),
        ],
        out_specs=[pl.BlockSpec((m, tn), lambda j: (0, j)), pl.BlockSpec((kw, tn), lambda j: (0, j))],
        out_shape=[jax.ShapeDtypeStruct((m, d), F32), jax.ShapeDtypeStruct((kw, d), BF16)],
        compiler_params=_cparams("parallel"),
        name="out_proj_sample",
    )(x, a, o, c, w_out)


def _conv_taps(u, prev1, prev2, w_ref):
    return w_ref[0:1, :] * prev2 + w_ref[1:2, :] * prev1 + w_ref[2:3, :] * u


def _out_proj_kernel(tiles_per_seq, x_ref, a_ref, o_ref, cb_ref, cc_ref, ch_ref, cw_ref, w_ref,
                     y_ref, cs_ref, carry_ref):
    i = pl.program_id(0)
    tm = cc_ref.shape[0]

    @pl.when(i % tiles_per_seq == 0)
    def _():
        carry_ref[...] = jnp.zeros_like(carry_ref)

    u = cc_ref[...] * ch_ref[...]
    row = lax.broadcasted_iota(jnp.int32, u.shape, 0)
    last1 = carry_ref[7:8, :]
    last2 = carry_ref[6:7, :]
    um1 = jnp.where(row == 0, last1, pltpu.roll(u, 1, 0))
    um2 = jnp.where(row == 0, last2, jnp.where(row == 1, last1, pltpu.roll(u, 2, 0)))
    c = cb_ref[...] * _conv_taps(u, um1, um2, cw_ref)
    carry_ref[...] = u[tm - 8:tm]

    y_ref[...] = x_ref[...] + _mix_dot(a_ref[...], o_ref[...], c, w_ref)

    @pl.when(i % tiles_per_seq == tiles_per_seq - 1)
    def _():
        cs_ref[0] = carry_ref[8 - (CONV_K - 1):8, :]


def out_proj(x, a, o, proj, conv_w, wb, layer, seq_len, tm):
    m, d = x.shape
    tiles_per_seq = seq_len // tm
    col = lambda blk: pl.BlockSpec((tm, CONV_WIDTH), lambda i: (i, blk))
    return pl.pallas_call(
        functools.partial(_out_proj_kernel, tiles_per_seq),
        grid=(m // tm,),
        in_specs=[
            pl.BlockSpec((tm, d), lambda i: (i, 0)),
            pl.BlockSpec((tm, ATTN_WIDTH), lambda i: (i, 0)),
            pl.BlockSpec((tm, HG_WIDTH), lambda i: (i, 0)),
            col(CB_BLK_512), col(CC_BLK_512), col(CH_BLK_512),
            pl.BlockSpec((None, CONV_K, CONV_WIDTH), lambda i: (layer, 0, 0)),
            pl.BlockSpec(wb.shape, lambda i: (0, 0)),
        ],
        out_specs=[pl.BlockSpec((tm, d), lambda i: (i, 0)),
                   pl.BlockSpec((1, CONV_K - 1, CONV_WIDTH), lambda i: (i // tiles_per_seq, 0, 0))],
        out_shape=[jax.ShapeDtypeStruct((m, d), F32),
                   jax.ShapeDtypeStruct((m // seq_len, CONV_K - 1, CONV_WIDTH), F32)],
        scratch_shapes=[pltpu.VMEM((8, CONV_WIDTH), F32)],
        compiler_params=_cparams("arbitrary"),
        name="out_proj",
    )(x, a, o, proj, proj, proj, conv_w, wb)


def _ffn_step(h_ref, wg, wu, wd, y_ref):
    h = h_ref[...]
    gate = _dot(h, wg)
    up = _dot(h, wu)
    act = (gate * _sigmoid(gate)) * up
    y_ref[...] += _dot(act.astype(BF16), wd)


def _ffn_begin(x_ref, g_ref, y_ref, h_ref):
    x = x_ref[...]
    h_ref[...] = _rms(x, g_ref[...]).astype(BF16)
    y_ref[...] = x


def _ffn_sample_kernel(last_layer, x_ref, g_ref, wg_ref, wu_ref, wd_ref, gf_ref,
                       y_ref, wgb_ref, wub_ref, wdb_ref, h_ref):
    f = pl.program_id(0)

    @pl.when(f == 0)
    def _():
        _ffn_begin(x_ref, g_ref, y_ref, h_ref)

    wg = wg_ref[...].astype(BF16)
    wu = wu_ref[...].astype(BF16)
    wd = wd_ref[...].astype(BF16)
    wgb_ref[...] = wg
    wub_ref[...] = wu
    wdb_ref[...] = wd
    _ffn_step(h_ref, wg, wu, wd, y_ref)

    if last_layer:
        @pl.when(f == pl.num_programs(0) - 1)
        def _():
            y_ref[...] = _rms(y_ref[...], gf_ref[...])


def ffn_sample(x, g, w_gate_up, w_down, g_final, layer, tf):
    m, d = x.shape
    nf = D_FF // tf
    return pl.pallas_call(
        functools.partial(_ffn_sample_kernel, layer == DEPTH - 1),
        grid=(nf,),
        in_specs=[
            pl.BlockSpec((m, d), lambda f: (0, 0)),
            pl.BlockSpec((None, 1, d), lambda f: (layer, 0, 0)),
            pl.BlockSpec((None, d, tf), lambda f: (layer, 0, f)),
            pl.BlockSpec((None, d, tf), lambda f: (layer, 0, f + nf)),
            pl.BlockSpec((None, tf, d), lambda f: (layer, f, 0)),
            pl.BlockSpec((1, d), lambda f: (0, 0)),
        ],
        out_specs=[
            pl.BlockSpec((m, d), lambda f: (0, 0)),
            pl.BlockSpec((d, tf), lambda f: (0, f)),
            pl.BlockSpec((d, tf), lambda f: (0, f)),
            pl.BlockSpec((tf, d), lambda f: (f, 0)),
        ],
        out_shape=[jax.ShapeDtypeStruct((m, d), F32), jax.ShapeDtypeStruct((d, D_FF), BF16),
                   jax.ShapeDtypeStruct((d, D_FF), BF16), jax.ShapeDtypeStruct((D_FF, d), BF16)],
        scratch_shapes=[pltpu.VMEM((m, d), BF16)],
        compiler_params=_cparams("arbitrary"),
        name="ffn_sample",
    )(x, g, w_gate_up, w_gate_up, w_down, g_final)


def _ffn_kernel(last_layer, x_ref, g_ref, wg_ref, wu_ref, wd_ref, gf_ref, y_ref, h_ref):
    f = pl.program_id(1)

    @pl.when(f == 0)
    def _():
        _ffn_begin(x_ref, g_ref, y_ref, h_ref)

    _ffn_step(h_ref, wg_ref[...], wu_ref[...], wd_ref[...], y_ref)

    if last_layer:
        @pl.when(f == pl.num_programs(1) - 1)
        def _():
            y_ref[...] = _rms(y_ref[...], gf_ref[...])


def ffn(x, g, wgb, wub, wdb, g_final, layer, tm, tf):
    m, d = x.shape
    return pl.pallas_call(
        functools.partial(_ffn_kernel, layer == DEPTH - 1),
        grid=(m // tm, D_FF // tf),
        in_specs=[
            pl.BlockSpec((tm, d), lambda i, f: (i, 0)),
            pl.BlockSpec((None, 1, d), lambda i, f: (layer, 0, 0)),
            pl.BlockSpec((d, tf), lambda i, f: (0, f)),
            pl.BlockSpec((d, tf), lambda i, f: (0, f)),
            pl.BlockSpec((tf, d), lambda i, f: (f, 0)),
            pl.BlockSpec((1, d), lambda i, f: (0, 0)),
        ],
        out_specs=pl.BlockSpec((tm, d), lambda i, f: (i, 0)),
        out_shape=jax.ShapeDtypeStruct((m, d), F32),
        scratch_shapes=[pltpu.VMEM((tm, d), BF16)],
        compiler_params=_cparams("parallel", "arbitrary"),
        name="ffn",
    )(x, g, wgb, wub, wdb, g_final)


def _rope_tables(pos):
    half = ROT_DIM // 2
    inv = ROPE_THETA ** (-jnp.arange(half, dtype=F32) * 2.0 / ROT_DIM)
    ang = pos.astype(F32)[:, None] * inv[None, :]
    cos, sin = jnp.cos(ang), jnp.sin(ang)
    n = pos.shape[0]
    ones = jnp.ones((n, HEAD_DIM - ROT_DIM), F32)
    zeros = jnp.zeros((n, HEAD_DIM - ROT_DIM), F32)
    zh = jnp.zeros((n, half), F32)
    c = jnp.concatenate([cos, cos, ones], axis=-1)
    s_lo = jnp.concatenate([-sin, zh, zeros], axis=-1)
    s_hi = jnp.concatenate([zh, sin, zeros], axis=-1)
    rep = lambda t: jnp.concatenate([t, t], axis=-1)
    return rep(c), rep(s_lo), rep(s_hi)


def _rope(x, c, s_lo, s_hi):
    half = ROT_DIM // 2
    outs = []
    for j in range(x.shape[1] // 128):
        xc = x[:, j * 128:(j + 1) * 128]
        outs.append(xc * c + pltpu.roll(xc, 128 - half, 1) * s_lo + pltpu.roll(xc, half, 1) * s_hi)
    return outs[0] if len(outs) == 1 else jnp.concatenate(outs, axis=1)


def _rope_tables_t(pos):
    half = ROT_DIM // 2
    inv = ROPE_THETA ** (-jnp.arange(half, dtype=F32) * 2.0 / ROT_DIM)
    ang = inv[:, None] * pos.astype(F32)[None, :]
    return jnp.cos(ang), jnp.sin(ang)


def _rope_t(xt, cos_t, sin_t):
    half = ROT_DIM // 2
    pieces = []
    for base in range(0, xt.shape[0], HEAD_DIM):
        x1 = xt[base:base + half]
        x2 = xt[base + half:base + ROT_DIM]
        pieces += [x1 * cos_t - x2 * sin_t, x2 * cos_t + x1 * sin_t, xt[base + ROT_DIM:base + HEAD_DIM]]
    return jnp.concatenate(pieces, axis=0)


def _softmax_sink_pv(s, sk, v):
    m = jnp.maximum(jnp.max(s, axis=-1, keepdims=True), sk)
    p = jnp.exp(s - m)
    denom = jnp.sum(p, axis=-1, keepdims=True) + jnp.exp(sk - m)
    return _dot(p.astype(BF16), v) / denom


def _attn_prompt_kernel(layer, sink_ref, q_ref, k_ref, v_ref, c_ref, slo_ref, shi_ref, ct_ref, st_ref,
                        a_ref, ko_ref, vo_ref, kprev_ref, vtprev_ref):
    n = pl.program_id(1)
    nblk = pl.num_programs(1)
    blk = WINDOW
    lanes = Q_PER_KV * blk

    k = _rope(k_ref[0], c_ref[...], slo_ref[...], shi_ref[...])
    v = v_ref[0]
    vt = v.T
    qt = (_rope_t(q_ref[0].T, ct_ref[...], st_ref[...]) * ATTN_SCALE).astype(BF16)

    key = lax.broadcasted_iota(jnp.int32, (blk, lanes), 0)
    qry = lax.broadcasted_iota(jnp.int32, (blk, lanes), 1) % blk
    newer = key > qry
    zero_rows = jnp.zeros((HEAD_DIM, lanes), BF16)

    def scores(h, first):
        pair = slice((h // 2) * 2 * HEAD_DIM, (h // 2 + 1) * 2 * HEAD_DIM)
        qh = jnp.concatenate(
            [qt[(Q_PER_KV * h + g) * HEAD_DIM:(Q_PER_KV * h + g + 1) * HEAD_DIM] for g in range(Q_PER_KV)], axis=1)
        qz = jnp.concatenate([qh, zero_rows] if h % 2 == 0 else [zero_rows, qh], axis=0)
        s_cur = _dot(k[:, pair].astype(BF16), qz)
        s_prev = None if first else _dot(kprev_ref[:, pair].astype(BF16), qz)
        return s_prev, s_cur

    def softmax(h, s_prev, s_cur):
        sink = jnp.concatenate(
            [jnp.full((1, blk), sink_ref[layer, Q_PER_KV * h + g], F32) for g in range(Q_PER_KV)], axis=1)
        if s_prev is None:
            f = jnp.where(newer, NEG_BIG, s_cur)
            m = jnp.maximum(jnp.max(f, axis=0, keepdims=True), sink)
            p = jnp.exp(f - m)
            denom = jnp.sum(p, axis=0, keepdims=True) + jnp.exp(sink - m)
            return p.astype(BF16), None, denom
        f = jnp.where(newer, s_prev, s_cur)
        d = jnp.sum(jnp.where(key == qry, s_prev, 0.0), axis=0, keepdims=True)
        m = jnp.maximum(jnp.maximum(jnp.max(f, axis=0, keepdims=True), d), sink)
        p = jnp.exp(f - m)
        pd = jnp.exp(d - m)
        denom = jnp.sum(p, axis=0, keepdims=True) + pd + jnp.exp(sink - m)
        pp = jnp.concatenate([jnp.where(newer, p, 0.0), jnp.where(newer, 0.0, p)], axis=0).astype(BF16)
        return pp, pd, denom

    def weighted_values(h, pp, pd, denom):
        rows = slice(h * HEAD_DIM, (h + 1) * HEAD_DIM)
        vt_cur = vt[rows]
        if pd is None:
            return _dot(vt_cur.astype(BF16), pp) / denom
        vt_prev = vtprev_ref[rows, :]
        o = _dot(jnp.concatenate([vt_prev, vt_cur], axis=1).astype(BF16), pp)
        return (o + pd * jnp.concatenate([vt_prev] * Q_PER_KV, axis=1)) / denom

    def run(first):
        heads = range(N_KV_HEADS)
        s = [scores(h, first) for h in heads]
        p = [softmax(h, *s[h]) for h in heads]
        outs = []
        for h in heads:
            o = weighted_values(h, *p[h])
            outs += [o[:, g * blk:(g + 1) * blk] for g in range(Q_PER_KV)]
        a_ref[0] = jnp.concatenate(outs, axis=0).T.astype(a_ref.dtype)

    @pl.when(n == 0)
    def _():
        run(True)

    @pl.when(n > 0)
    def _():
        run(False)

    kprev_ref[...] = k
    vtprev_ref[...] = vt

    @pl.when(n == nblk - 1)
    def _():
        ko_ref[0] = k
        vo_ref[0] = v


def attn_prompt(proj, sink, tables, tables_t, layer):
    b, l, _ = proj.shape
    blk = WINDOW
    tab_spec = pl.BlockSpec((blk, 128), lambda i, n: (n, 0))
    tab_t_spec = pl.BlockSpec((ROT_DIM // 2, blk), lambda i, n: (0, n))
    kv_out = jax.ShapeDtypeStruct((b, blk, KV_WIDTH), F32)
    return pl.pallas_call(
        functools.partial(_attn_prompt_kernel, layer),
        grid=(b, l // blk),
        in_specs=[
            pl.BlockSpec(memory_space=pltpu.SMEM),
            pl.BlockSpec((1, blk, ATTN_WIDTH), lambda i, n: (i, n, Q_BLK_1024)),
            pl.BlockSpec((1, blk, KV_WIDTH), lambda i, n: (i, n, K_BLK_256)),
            pl.BlockSpec((1, blk, KV_WIDTH), lambda i, n: (i, n, V_BLK_256)),
            tab_spec, tab_spec, tab_spec, tab_t_spec, tab_t_spec,
        ],
        out_specs=[
            pl.BlockSpec((1, blk, ATTN_WIDTH), lambda i, n: (i, n, 0)),
            pl.BlockSpec((1, blk, KV_WIDTH), lambda i, n: (i, 0, 0)),
            pl.BlockSpec((1, blk, KV_WIDTH), lambda i, n: (i, 0, 0)),
        ],
        out_shape=[jax.ShapeDtypeStruct((b, l, ATTN_WIDTH), BF16), kv_out, kv_out],
        scratch_shapes=[pltpu.VMEM((blk, KV_WIDTH), F32), pltpu.VMEM((KV_WIDTH, blk), F32)],
        compiler_params=_cparams("parallel", "arbitrary"),
        name="attn_prompt",
    )(sink, proj, proj, proj, *tables, *tables_t)


def _attn_sample_kernel(layer, sink_ref, q_ref, k_ref, v_ref, kc_ref, vc_ref, c_ref, slo_ref, shi_ref,
                        k_carried_ref, v_carried_ref, a_ref, ko_ref, vo_ref):
    del k_carried_ref, v_carried_ref
    r8 = SAMPLE_ROWS
    nseq = r8 // SAMPLE_T
    lb = kc_ref.shape[1]
    c, s_lo, s_hi = c_ref[...], slo_ref[...], shi_ref[...]
    q = _rope(q_ref[...], c, s_lo, s_hi)
    k = _rope(k_ref[...], c, s_lo, s_hi)
    v = v_ref[...]
    pad = jnp.zeros((lb - r8, KV_WIDTH), F32)
    k_pad = jnp.concatenate([k, pad], axis=0)
    v_pad = jnp.concatenate([v, pad], axis=0)

    zeros = jnp.zeros((r8, HEAD_DIM), F32)
    qx = jnp.concatenate(
        [jnp.concatenate([q[:, hg * HEAD_DIM:(hg + 1) * HEAD_DIM] if slot == hg // Q_PER_KV else zeros
                          for slot in range(N_KV_HEADS)], axis=1) for hg in range(N_Q_HEADS)],
        axis=0).astype(BF16)
    sk = jnp.concatenate([jnp.full((r8, 1), sink_ref[layer, hg], F32) for hg in range(N_Q_HEADS)], axis=0)

    rows = N_Q_HEADS * r8
    ri = lax.broadcasted_iota(jnp.int32, (rows, 2 * lb), 0) % r8
    t = ri % SAMPLE_T
    kj = lax.broadcasted_iota(jnp.int32, (rows, 2 * lb), 1)
    cj = kj - lb
    row_seq = lax.broadcasted_iota(jnp.int32, (rows, KV_WIDTH), 0) % r8 // SAMPLE_T

    s = []
    for e in range(nseq):
        kk = jnp.concatenate([kc_ref[e], k_pad], axis=0).astype(BF16)
        s.append(_dot_nt(qx, kk) * ATTN_SCALE)
    o = None
    for e in range(nseq):
        mask = ((kj < lb) & (kj >= t)) | (
            (cj >= e * SAMPLE_T) & (cj < (e + 1) * SAMPLE_T) & (cj - e * SAMPLE_T <= t))
        vv = jnp.concatenate([vc_ref[e], v_pad], axis=0).astype(BF16)
        oe = _softmax_sink_pv(jnp.where(mask, s[e], NEG_BIG), sk, vv)
        o = oe if o is None else jnp.where(row_seq == e, oe, o)
    a_ref[...] = jnp.concatenate(
        [o[hg * r8:(hg + 1) * r8, (hg // Q_PER_KV) * HEAD_DIM:(hg // Q_PER_KV + 1) * HEAD_DIM]
         for hg in range(N_Q_HEADS)], axis=1)

    row = lax.broadcasted_iota(jnp.int32, (lb, KV_WIDTH), 0)
    for e in range(nseq):
        shift = (lb - SAMPLE_T - e * SAMPLE_T) % lb
        for new_pad, cache_ref, out_ref in ((k_pad, kc_ref, ko_ref), (v_pad, vc_ref, vo_ref)):
            new_rows = new_pad if shift == 0 else pltpu.roll(new_pad, shift, 0)
            old_rows = pltpu.roll(cache_ref[e], lb - SAMPLE_T, 0)
            out_ref[e] = jnp.where(row >= lb - SAMPLE_T, new_rows, old_rows)


def attn_sample(proj, k_cache, v_cache, k_new, v_new, sink, tables, layer):
    m = proj.shape[0]
    r8 = SAMPLE_ROWS
    nseq = r8 // SAMPLE_T
    _, b, lb, _ = k_cache.shape
    tab_spec = pl.BlockSpec((r8, 128), lambda i: (0, 0))
    cache_spec = pl.BlockSpec((None, nseq, lb, KV_WIDTH), lambda i: (layer, i, 0, 0))
    carried = pl.BlockSpec(memory_space=pl.ANY)
    kv_out = jax.ShapeDtypeStruct(k_new.shape, F32)
    return pl.pallas_call(
        functools.partial(_attn_sample_kernel, layer),
        grid=(m // r8,),
        in_specs=[
            pl.BlockSpec(memory_space=pltpu.SMEM),
            pl.BlockSpec((r8, ATTN_WIDTH), lambda i: (i, Q_BLK_1024)),
            pl.BlockSpec((r8, KV_WIDTH), lambda i: (i, K_BLK_256)),
            pl.BlockSpec((r8, KV_WIDTH), lambda i: (i, V_BLK_256)),
            cache_spec, cache_spec,
            tab_spec, tab_spec, tab_spec,
            carried, carried,
        ],
        out_specs=[pl.BlockSpec((r8, ATTN_WIDTH), lambda i: (i, 0)), cache_spec, cache_spec],
        out_shape=[jax.ShapeDtypeStruct((m, ATTN_WIDTH), F32), kv_out, kv_out],
        input_output_aliases={9: 1, 10: 2},
        compiler_params=_cparams("parallel"),
        name="attn_sample",
    )(sink, proj, proj, proj, k_cache, v_cache, *tables, k_new, v_new)


def _lower_bound(lbr_ref, layer, hs):
    rows = [lbr_ref[r:r + 1, hs] for r in range(DEPTH)]
    mx = functools.reduce(jnp.maximum, rows)
    es = [jnp.exp(r - mx) for r in rows]
    tot = functools.reduce(lambda a, b: a + b, es)
    lbp = [e / tot for e in es]
    acc = lbp[0]
    for r in range(1, layer + 1):
        acc = acc + lbp[r]
    return acc - lbp[0]


def _hgrn_gates(hq, hf, lb):
    q = hq * _sigmoid(hq) * (HG_DK ** -0.5)
    e = jnp.exp(-jnp.abs(hf))
    r = 1.0 / (1.0 + e)
    pos = hf >= 0.0
    sig_pos = jnp.where(pos, r, e * r)
    sig_neg = jnp.where(pos, e * r, r)
    log_f = jnp.log(jnp.maximum(lb, LB_TINY) + (1.0 - lb) * sig_pos)
    k = (1.0 - lb) * sig_neg
    return q, log_f, k


def _hgrn_finish(o, g_norm, hg):
    return _rms(o, g_norm) * (hg * _sigmoid(hg))


def _hgrn_levels():
    s = HG_CHUNK // 2
    while s >= 1:
        yield s
        s //= 2


def _hgrn_level_masks():
    t = np.arange(HG_CHUNK)[:, None]
    s = np.arange(HG_CHUNK)[None, :]
    return np.stack([((t % (2 * h) >= h) & (t // (2 * h) == s // (2 * h)) & (s % (2 * h) < h))
                     for h in _hgrn_levels()]).astype(np.float32)


def _hgrn_pair_reference(g_ref, gcum, half, sub8):
    c = HG_CHUNK
    pair = 2 * half
    bcast = lambda r: jnp.broadcast_to(g_ref[r:r + 1, :], (8, HG_DK))
    if pair >= 8:
        return jnp.concatenate([bcast((8 * v // pair) * pair + half - 1) for v in range(c // 8)], axis=0)
    if pair == 4:
        lo = jnp.concatenate([bcast(8 * v + 1) for v in range(c // 8)], axis=0)
        hi = jnp.concatenate([bcast(8 * v + 5) for v in range(c // 8)], axis=0)
        return jnp.where(sub8 < 4, lo, hi)
    return jnp.where(sub8 % 2 == 1, pltpu.roll(gcum, 1, 0), gcum)


def _hgrn_level_operand(q, k, gcum, g_ref, half, row):
    ref = _hgrn_pair_reference(g_ref, gcum, half, row % 8)
    right = row % (2 * half) >= half
    return (jnp.where(right, q, k) * jnp.exp(-jnp.abs(gcum - ref))).astype(BF16)


def _hgrn_prompt_kernel(layer, hq_ref, hf_ref, hi_ref, hg_ref, lbr_ref, gn_ref, m_ref,
                        o_ref, so_ref, s_ref, g_ref):
    step = pl.program_id(1)
    nstep = pl.num_programs(1)
    c = HG_CHUNK
    nchunk = hq_ref.shape[1] // c
    heads = range(HG_HEADS)
    units = [(h, cc) for cc in range(nchunk) for h in heads]

    @pl.when(step == 0)
    def _():
        s_ref[...] = jnp.zeros_like(s_ref)

    row = lax.broadcasted_iota(jnp.int32, (c, HG_DK), 0)
    cols = lambda h: slice(h * HG_DK, (h + 1) * HG_DK)
    rws = lambda cc: slice(cc * c, (cc + 1) * c)
    gbuf = lambda h, cc: g_ref.at[h * nchunk + cc]
    lbs = [_lower_bound(lbr_ref, layer, cols(h)) for h in heads]

    q, k, v, gcum = {}, {}, {}, {}
    for u in units:
        h, cc = u
        q[u], log_f, k[u] = _hgrn_gates(hq_ref[0, rws(cc), cols(h)], hf_ref[0, rws(cc), cols(h)], lbs[h])
        v[u] = hi_ref[0, rws(cc), cols(h)]
        g = log_f
        sh = 1
        while sh < c:
            g = g + jnp.where(row >= sh, pltpu.roll(g, sh, 0), 0.0)
            sh *= 2
        gcum[u] = g
        gbuf(h, cc)[...] = g

    a = {u: jnp.zeros((c, c), F32) for u in units}
    for lvl, half in enumerate(_hgrn_levels()):
        for u in units:
            x = _hgrn_level_operand(q[u], k[u], gcum[u], gbuf(*u), half, row)
            a[u] = a[u] + _dot_nt(x, x) * m_ref[lvl]

    o_intra, qs, ks, decay = {}, {}, {}, {}
    for u in units:
        vb = v[u].astype(BF16)
        o_intra[u] = _dot(a[u].astype(BF16), vb) + jnp.sum(q[u] * k[u], axis=-1, keepdims=True) * v[u]
        g_last = gbuf(*u)[c - 1:c, :]
        qs[u] = (q[u] * jnp.exp(gcum[u])).astype(BF16)
        ks[u] = (k[u] * jnp.exp(g_last - gcum[u])).astype(BF16)
        decay[u] = jnp.transpose(jnp.broadcast_to(jnp.exp(g_last), (HG_DK, HG_DK)))

    state = [s_ref[h] for h in heads]
    for cc in range(nchunk):
        o_inter = [_dot(qs[h, cc], state[h].astype(BF16)) for h in heads]
        update = [_dot_tn(ks[h, cc], v[h, cc].astype(BF16)) for h in heads]
        for h in heads:
            o = o_inter[h] + o_intra[h, cc]
            o_ref[0, rws(cc), cols(h)] = _hgrn_finish(
                o, gn_ref[h:h + 1, :], hg_ref[0, rws(cc), cols(h)]).astype(o_ref.dtype)
            state[h] = state[h] * decay[h, cc] + update[h]
    for h in heads:
        s_ref[h] = state[h]

    @pl.when(step == nstep - 1)
    def _():
        so_ref[0] = s_ref[...]


def hgrn_prompt(proj, lb_raw, g_norm, layer, rows):
    b, l, _ = proj.shape
    c = HG_CHUNK
    masks = jnp.asarray(_hgrn_level_masks())
    col = lambda blk: pl.BlockSpec((1, rows, HG_WIDTH), lambda i, n: (i, n, blk))
    return pl.pallas_call(
        functools.partial(_hgrn_prompt_kernel, layer),
        grid=(b, l // rows),
        in_specs=[
            col(HQ_BLK_512), col(HF_BLK_512), col(HI_BLK_512), col(HG_BLK_512),
            pl.BlockSpec((DEPTH, HG_WIDTH), lambda i, n: (0, 0)),
            pl.BlockSpec((None, HG_HEADS, HG_DV), lambda i, n: (layer, 0, 0)),
            pl.BlockSpec(masks.shape, lambda i, n: (0, 0, 0)),
        ],
        out_specs=[
            pl.BlockSpec((1, rows, HG_WIDTH), lambda i, n: (i, n, 0)),
            pl.BlockSpec((1, HG_HEADS, HG_DK, HG_DV), lambda i, n: (i, 0, 0, 0)),
        ],
        out_shape=[jax.ShapeDtypeStruct((b, l, HG_WIDTH), BF16),
                   jax.ShapeDtypeStruct((b, HG_HEADS, HG_DK, HG_DV), F32)],
        scratch_shapes=[pltpu.VMEM((HG_HEADS, HG_DK, HG_DV), F32),
                        pltpu.VMEM((HG_HEADS * (rows // c), c, HG_DK), F32)],
        compiler_params=_cparams("parallel", "arbitrary"),
        name="hgrn_prompt",
    )(proj, proj, proj, proj, lb_raw, g_norm, masks)


def _hgrn_sample_kernel(layer, hq_ref, hf_ref, hi_ref, hg_ref, lbr_ref, gn_ref, s_ref, carried_ref,
                        o_ref, so_ref, q_ref, k_ref, g_ref):
    del carried_ref
    r8, tt = SAMPLE_ROWS, SAMPLE_T
    nseq = r8 // tt
    row = lax.broadcasted_iota(jnp.int32, (r8, HG_DK), 0)
    t_idx = row % tt
    seq = row // tt

    def pick(ref, s):
        out = ref[s:s + 1, :]
        for e in range(1, nseq):
            out = jnp.where(seq == e, ref[e * tt + s:e * tt + s + 1, :], out)
        return out

    for h in range(HG_HEADS):
        hs = slice(h * HG_DK, (h + 1) * HG_DK)
        lb = _lower_bound(lbr_ref, layer, hs)
        q, log_f, k = _hgrn_gates(hq_ref[:, hs], hf_ref[:, hs], lb)
        gcum = log_f
        sh = 1
        while sh < tt:
            gcum = gcum + jnp.where(t_idx >= sh, pltpu.roll(gcum, sh, 0), 0.0)
            sh *= 2
        q_ref[...] = q
        k_ref[...] = k
        g_ref[...] = gcum
        v = hi_ref[:, hs]
        g_last = pick(g_ref, tt - 1)
        qs = (q * jnp.exp(gcum)).astype(BF16)
        ks_all = k * jnp.exp(g_last - gcum)

        o = jnp.zeros((r8, HG_DV), F32)
        for e in range(nseq):
            state = s_ref[e, h]
            o = jnp.where(seq == e, _dot(qs, state.astype(BF16)), o)
            g_last_e = g_ref[e * tt + tt - 1:e * tt + tt, :]
            decay = jnp.transpose(jnp.broadcast_to(jnp.exp(g_last_e), (HG_DK, HG_DK)))
            ks_e = jnp.where(seq == e, ks_all, 0.0).astype(BF16)
            so_ref[e, h] = state * decay + _dot_tn(ks_e, v.astype(BF16))
        for s in range(tt):
            dec = jnp.exp(jnp.where(t_idx >= s, gcum - pick(g_ref, s), NEG_BIG))
            col = jnp.sum(q * (dec * pick(k_ref, s)), axis=-1, keepdims=True)
            o = o + col * pick(hi_ref.at[:, hs], s)
        o_ref[:, hs] = _hgrn_finish(o, gn_ref[h:h + 1, :], hg_ref[:, hs])


def hgrn_sample(proj, state, state_new, lb_raw, g_norm, layer):
    m = proj.shape[0]
    r8 = SAMPLE_ROWS
    nseq = r8 // SAMPLE_T
    col = lambda blk: pl.BlockSpec((r8, HG_WIDTH), lambda i: (i, blk))
    st_spec = pl.BlockSpec((None, nseq, HG_HEADS, HG_DK, HG_DV), lambda i: (layer, i, 0, 0, 0))
    return pl.pallas_call(
        functools.partial(_hgrn_sample_kernel, layer),
        grid=(m // r8,),
        in_specs=[
            col(HQ_BLK_512), col(HF_BLK_512), col(HI_BLK_512), col(HG_BLK_512),
            pl.BlockSpec((DEPTH, HG_WIDTH), lambda i: (0, 0)),
            pl.BlockSpec((None, HG_HEADS, HG_DV), lambda i: (layer, 0, 0)),
            st_spec,
            pl.BlockSpec(memory_space=pl.ANY),
        ],
        out_specs=[pl.BlockSpec((r8, HG_WIDTH), lambda i: (i, 0)), st_spec],
        out_shape=[jax.ShapeDtypeStruct((m, HG_WIDTH), F32), jax.ShapeDtypeStruct(state_new.shape, F32)],
        input_output_aliases={7: 1},
        scratch_shapes=[pltpu.VMEM((r8, HG_DK), F32), pltpu.VMEM((r8, HG_DK), F32), pltpu.VMEM((r8, HG_DK), F32)],
        compiler_params=_cparams("parallel"),
        name="hgrn_sample",
    )(proj, proj, proj, proj, lb_raw, g_norm, state, state_new)


def _conv_sample_kernel(cb_ref, cc_ref, ch_ref, w_ref, st_ref, y_ref, so_ref, u_ref):
    r8, tt = SAMPLE_ROWS, SAMPLE_T
    nseq = r8 // tt
    u = cc_ref[...] * ch_ref[...]
    u_ref[...] = u
    row = lax.broadcasted_iota(jnp.int32, u.shape, 0)
    t_idx = row % tt
    seq = row // tt
    last1 = st_ref[0, 1:2, :]
    last2 = st_ref[0, 0:1, :]
    for e in range(1, nseq):
        last1 = jnp.where(seq == e, st_ref[e, 1:2, :], last1)
        last2 = jnp.where(seq == e, st_ref[e, 0:1, :], last2)
    um1 = jnp.where(t_idx == 0, last1, pltpu.roll(u, 1, 0))
    um2 = jnp.where(t_idx == 0, last2, jnp.where(t_idx == 1, last1, pltpu.roll(u, 2, 0)))
    y_ref[...] = cb_ref[...] * _conv_taps(u, um1, um2, w_ref)
    for e in range(nseq):
        so_ref[e] = u_ref[(e + 1) * tt - (CONV_K - 1):(e + 1) * tt, :]


def conv_sample(proj, state, w, layer):
    m = proj.shape[0]
    r8 = SAMPLE_ROWS
    nseq = r8 // SAMPLE_T
    col = lambda blk: pl.BlockSpec((r8, CONV_WIDTH), lambda i: (i, blk))
    st_in = pl.BlockSpec((None, nseq, CONV_K - 1, CONV_WIDTH), lambda i: (layer, i, 0, 0))
    st_spec = pl.BlockSpec((nseq, CONV_K - 1, CONV_WIDTH), lambda i: (i, 0, 0))
    return pl.pallas_call(
        _conv_sample_kernel,
        grid=(m // r8,),
        in_specs=[col(CB_BLK_512), col(CC_BLK_512), col(CH_BLK_512),
                  pl.BlockSpec((None, CONV_K, CONV_WIDTH), lambda i: (layer, 0, 0)), st_in],
        out_specs=[pl.BlockSpec((r8, CONV_WIDTH), lambda i: (i, 0)), st_spec],
        out_shape=[jax.ShapeDtypeStruct((m, CONV_WIDTH), F32), jax.ShapeDtypeStruct(state.shape[1:], F32)],
        scratch_shapes=[pltpu.VMEM((r8, CONV_WIDTH), F32)],
        compiler_params=_cparams("parallel"),
        name="conv_sample",
    )(proj, proj, proj, w, state)


def kernel(x_prompt, x_sample, cache_attn_k, cache_attn_v, state_hgrn, state_conv, w_in, attn_sink,
           hgrn_lower_bounds, hgrn_norm, conv_w, w_out, norm_mix, norm_ffn, w_gate_up, w_down, norm_final):
    bp, lp, d = x_prompt.shape
    bs, ls, _ = x_sample.shape
    assert ls == SAMPLE_T and (bs * ls) % SAMPLE_ROWS == 0
    lb = cache_attn_k.shape[2]
    mp, ms = bp * lp, bs * ls

    pos_p = jnp.arange(lp, dtype=jnp.int32)
    tab_p = _rope_tables(pos_p)
    tab_pt = _rope_tables_t(pos_p)
    pos_s = PAST_LEN + jnp.arange(SAMPLE_ROWS, dtype=jnp.int32) % SAMPLE_T
    tab_s = _rope_tables(pos_s)

    xp = x_prompt.reshape(mp, d)
    xs = x_sample.reshape(ms, d)
    kc = cache_attn_k.reshape(DEPTH, bs, lb, KV_WIDTH)
    vc = cache_attn_v.reshape(DEPTH, bs, lb, KV_WIDTH)

    g_mix = norm_mix.reshape(DEPTH, 1, d)
    g_ffn = norm_ffn.reshape(DEPTH, 1, d)
    g_final = norm_final.reshape(1, d)

    k_s = jnp.zeros(kc.shape, F32)
    v_s = jnp.zeros(vc.shape, F32)
    s_s = jnp.zeros(state_hgrn.shape, F32)

    outs = {name: [] for name in ("kp", "vp", "sp", "cp", "cs")}
    for l in range(DEPTH):
        proj_s, w_in_b = in_proj_sample(xs, g_mix, w_in, l, 1024)
        a_s, k_s, v_s = attn_sample(proj_s, kc, vc, k_s, v_s, attn_sink, tab_s, l)
        o_s, s_s = hgrn_sample(proj_s, state_hgrn, s_s, hgrn_lower_bounds, hgrn_norm, l)
        c_s, cst_s = conv_sample(proj_s, state_conv, conv_w, l)
        xs, w_out_b = out_proj_sample(xs, a_s, o_s, c_s, w_out, l, 512)
        xs, w_gate_b, w_up_b, w_down_b = ffn_sample(xs, g_ffn, w_gate_up, w_down, g_final, l, 512)

        proj_p = in_proj(xp, g_mix, w_in_b, l, 1024, 1024)
        proj_p3 = proj_p.reshape(bp, lp, IN_WIDTH)
        a_p, k_p, v_p = attn_prompt(proj_p3, attn_sink, tab_p, tab_pt, l)
        o_p, s_p = hgrn_prompt(proj_p3, hgrn_lower_bounds, hgrn_norm, l, 4 * HG_CHUNK)
        xp, cst_p = out_proj(xp, a_p.reshape(mp, ATTN_WIDTH), o_p.reshape(mp, HG_WIDTH), proj_p,
                             conv_w, w_out_b, l, lp, 512)
        xp = ffn(xp, g_ffn, w_gate_b, w_up_b, w_down_b, g_final, l, 1024, 512)

        outs["kp"].append(k_p.reshape(bp, WINDOW, N_KV_HEADS, HEAD_DIM))
        outs["vp"].append(v_p.reshape(bp, WINDOW, N_KV_HEADS, HEAD_DIM))
        outs["sp"].append(s_p)
        outs["cp"].append(cst_p)
        outs["cs"].append(cst_s)

    st = lambda name: jnp.stack(outs[name])
    kv_shape = (DEPTH, bs, lb, N_KV_HEADS, HEAD_DIM)
    return (xp.reshape(bp, lp, d), xs.reshape(bs, ls, d), st("kp"), st("vp"), st("sp"), st("cp"),
            k_s.reshape(kv_shape), v_s.reshape(kv_shape), s_s, st("cs"))
```

```python
import functools

import jax
import jax.numpy as jnp
import numpy as np
from jax import lax
from jax.experimental import pallas as pl
from jax.experimental.pallas import tpu as pltpu

F32 = jnp.float32
BF16 = jnp.bfloat16

D_MODEL = 2048
DEPTH = 4
PAST_LEN = 16384
HEAD_DIM = 64
N_Q_HEADS = 16
N_KV_HEADS = 4
Q_PER_KV = 4
ATTN_WIDTH = 1024
KV_WIDTH = 256
WINDOW = 128
ROPE_THETA = 500000.0
ROT_DIM = 16
ATTN_SCALE = HEAD_DIM ** -0.5
NEG_BIG = -1e30
HG_HEADS = 4
HG_DK = 128
HG_DV = 128
HG_WIDTH = 512
HG_CHUNK = 64
LB_TINY = 1e-30
CONV_K = 3
CONV_WIDTH = 512
D_FF = 5632
IN_WIDTH = 5120
NORM_EPS = 1e-6
SAMPLE_T = 4
SAMPLE_ROWS = 8

Q_BLK_1024 = 0
K_BLK_256 = 4
V_BLK_256 = 5
HQ_BLK_512, HF_BLK_512, HI_BLK_512, HG_BLK_512 = 3, 4, 5, 6
CB_BLK_512, CC_BLK_512, CH_BLK_512 = 7, 8, 9

VMEM_LIMIT = 56 * 1024 * 1024


def _cparams(*sem):
    return pltpu.CompilerParams(dimension_semantics=sem, vmem_limit_bytes=VMEM_LIMIT)


def _rms(x, g):
    ms = jnp.mean(x * x, axis=-1, keepdims=True)
    return x * lax.rsqrt(ms + NORM_EPS) * g


def _silu(x):
    return x * (1.0 / (1.0 + jnp.exp(-x)))


def _dot(a, b):
    return jnp.dot(a, b, preferred_element_type=F32)


def _dot_nt(a, b):
    return lax.dot_general(a, b, (((1,), (1,)), ((), ())), preferred_element_type=F32)


def _dot_tn(a, b):
    return lax.dot_general(a, b, (((0,), (0,)), ((), ())), preferred_element_type=F32)


def _in_proj_sample_kernel(x_ref, g_ref, w_ref, o_ref, wb_ref, h_ref):
    @pl.when(pl.program_id(0) == 0)
    def _():
        h_ref[...] = _rms(x_ref[...], g_ref[...]).astype(BF16)

    wb = w_ref[...].astype(BF16)
    wb_ref[...] = wb
    o_ref[...] = _dot(h_ref[...], wb)


def in_proj_sample(x, g, w, layer, tn):
    m, k = x.shape
    n = w.shape[2]
    return pl.pallas_call(
        _in_proj_sample_kernel,
        grid=(n // tn,),
        in_specs=[
            pl.BlockSpec((m, k), lambda j: (0, 0)),
            pl.BlockSpec((None, 1, k), lambda j: (layer, 0, 0)),
            pl.BlockSpec((None, k, tn), lambda j: (layer, 0, j)),
        ],
        out_specs=[pl.BlockSpec((m, tn), lambda j: (0, j)), pl.BlockSpec((k, tn), lambda j: (0, j))],
        out_shape=[jax.ShapeDtypeStruct((m, n), F32), jax.ShapeDtypeStruct((k, n), BF16)],
        scratch_shapes=[pltpu.VMEM((m, k), BF16)],
        compiler_params=_cparams("arbitrary"),
        name="in_proj_sample",
    )(x, g, w)


NORM_ROW_SPLIT = 4


def _in_proj_kernel(x_ref, g_ref, w_ref, o_ref, h_ref):
    j = pl.program_id(1)
    rs = x_ref.shape[0] // NORM_ROW_SPLIT

    @pl.when(j == 0)
    def _():
        for r in range(NORM_ROW_SPLIT):
            rows = slice(r * rs, (r + 1) * rs)
            h = _rms(x_ref[rows, :], g_ref[...]).astype(BF16)
            h_ref[rows, :] = h
            o_ref[rows, :] = _dot(h, w_ref[...])

    @pl.when(j > 0)
    def _():
        o_ref[...] = _dot(h_ref[...], w_ref[...])


def in_proj(x, g, wb, layer, tm, tn):
    m, k = x.shape
    n = wb.shape[1]
    return pl.pallas_call(
        _in_proj_kernel,
        grid=(m // tm, n // tn),
        in_specs=[
            pl.BlockSpec((tm, k), lambda i, j: (i, 0)),
            pl.BlockSpec((None, 1, k), lambda i, j: (layer, 0, 0)),
            pl.BlockSpec((k, tn), lambda i, j: (0, j)),
        ],
        out_specs=pl.BlockSpec((tm, tn), lambda i, j: (i, j)),
        out_shape=jax.ShapeDtypeStruct((m, n), F32),
        scratch_shapes=[pltpu.VMEM((tm, k), BF16)],
        compiler_params=_cparams("parallel", "arbitrary"),
        name="in_proj",
    )(x, g, wb)


def _mix_dot(a, o, c, w):
    acc = _dot(a.astype(BF16), w[0:ATTN_WIDTH])
    acc = acc + _dot(o.astype(BF16), w[ATTN_WIDTH:ATTN_WIDTH + HG_WIDTH])
    return acc + _dot(c.astype(BF16), w[ATTN_WIDTH + HG_WIDTH:])


def _out_proj_sample_kernel(x_ref, a_ref, o_ref, c_ref, w_ref, y_ref, wb_ref):
    wb = w_ref[...].astype(BF16)
    wb_ref[...] = wb
    y_ref[...] = x_ref[...] + _mix_dot(a_ref[...], o_ref[...], c_ref[...], wb)


def out_proj_sample(x, a, o, c, w_out, layer, tn):
    m, d = x.shape
    kw = w_out.shape[1]
    full = lambda width: pl.BlockSpec((m, width), lambda j: (0, 0))
    return pl.pallas_call(
        _out_proj_sample_kernel,
        grid=(d // tn,),
        in_specs=[
            pl.BlockSpec((m, tn), lambda j: (0, j)),
            full(ATTN_WIDTH), full(HG_WIDTH), full(CONV_WIDTH),
            pl.BlockSpec((None, kw, tn), lambda j: (layer, 0, j)),
        ],
        out_specs=[pl.BlockSpec((m, tn), lambda j: (0, j)), pl.BlockSpec((kw, tn), lambda j: (0, j))],
        out_shape=[jax.ShapeDtypeStruct((m, d), F32), jax.ShapeDtypeStruct((kw, d), BF16)],
        compiler_params=_cparams("parallel"),
        name="out_proj_sample",
    )(x, a, o, c, w_out)


def _conv_taps(u, prev1, prev2, w_ref):
    return w_ref[0:1, :] * prev2 + w_ref[1:2, :] * prev1 + w_ref[2:3, :] * u


def _out_proj_kernel(tiles_per_seq, x_ref, a_ref, o_ref, cb_ref, cc_ref, ch_ref, cw_ref, w_ref,
                     y_ref, cs_ref, carry_ref):
    i = pl.program_id(0)
    tm = cc_ref.shape[0]

    @pl.when(i % tiles_per_seq == 0)
    def _():
        carry_ref[...] = jnp.zeros_like(carry_ref)

    u = cc_ref[...] * ch_ref[...]
    row = lax.broadcasted_iota(jnp.int32, u.shape, 0)
    last1 = carry_ref[7:8, :]
    last2 = carry_ref[6:7, :]
    um1 = jnp.where(row == 0, last1, pltpu.roll(u, 1, 0))
    um2 = jnp.where(row == 0, last2, jnp.where(row == 1, last1, pltpu.roll(u, 2, 0)))
    c = cb_ref[...] * _conv_taps(u, um1, um2, cw_ref)
    carry_ref[...] = u[tm - 8:tm]

    y_ref[...] = x_ref[...] + _mix_dot(a_ref[...], o_ref[...], c, w_ref)

    @pl.when(i % tiles_per_seq == tiles_per_seq - 1)
    def _():
        cs_ref[0] = carry_ref[8 - (CONV_K - 1):8, :]


def out_proj(x, a, o, proj, conv_w, wb, layer, seq_len, tm):
    m, d = x.shape
    tiles_per_seq = seq_len // tm
    col = lambda blk: pl.BlockSpec((tm, CONV_WIDTH), lambda i: (i, blk))
    return pl.pallas_call(
        functools.partial(_out_proj_kernel, tiles_per_seq),
        grid=(m // tm,),
        in_specs=[
            pl.BlockSpec((tm, d), lambda i: (i, 0)),
            pl.BlockSpec((tm, ATTN_WIDTH), lambda i: (i, 0)),
            pl.BlockSpec((tm, HG_WIDTH), lambda i: (i, 0)),
            col(CB_BLK_512), col(CC_BLK_512), col(CH_BLK_512),
            pl.BlockSpec((None, CONV_K, CONV_WIDTH), lambda i: (layer, 0, 0)),
            pl.BlockSpec(wb.shape, lambda i: (0, 0)),
        ],
        out_specs=[pl.BlockSpec((tm, d), lambda i: (i, 0)),
                   pl.BlockSpec((1, CONV_K - 1, CONV_WIDTH), lambda i: (i // tiles_per_seq, 0, 0))],
        out_shape=[jax.ShapeDtypeStruct((m, d), F32),
                   jax.ShapeDtypeStruct((m // seq_len, CONV_K - 1, CONV_WIDTH), F32)],
        scratch_shapes=[pltpu.VMEM((8, CONV_WIDTH), F32)],
        compiler_params=_cparams("arbitrary"),
        name="out_proj",
    )(x, a, o, proj, proj, proj, conv_w, wb)


def _ffn_step(h_ref, wg, wu, wd, y_ref):
    h = h_ref[...]
    gate = _dot(h, wg)
    up = _dot(h, wu)
    act = _silu(gate) * up
    y_ref[...] += _dot(act.astype(BF16), wd)


def _ffn_begin(x_ref, g_ref, y_ref, h_ref):
    x = x_ref[...]
    h_ref[...] = _rms(x, g_ref[...]).astype(BF16)
    y_ref[...] = x


def _ffn_sample_kernel(last_layer, x_ref, g_ref, wg_ref, wu_ref, wd_ref, gf_ref,
                       y_ref, wgb_ref, wub_ref, wdb_ref, h_ref):
    f = pl.program_id(0)

    @pl.when(f == 0)
    def _():
        _ffn_begin(x_ref, g_ref, y_ref, h_ref)

    wg = wg_ref[...].astype(BF16)
    wu = wu_ref[...].astype(BF16)
    wd = wd_ref[...].astype(BF16)
    wgb_ref[...] = wg
    wub_ref[...] = wu
    wdb_ref[...] = wd
    _ffn_step(h_ref, wg, wu, wd, y_ref)

    if last_layer:
        @pl.when(f == pl.num_programs(0) - 1)
        def _():
            y_ref[...] = _rms(y_ref[...], gf_ref[...])


def ffn_sample(x, g, w_gate_up, w_down, g_final, layer, tf):
    m, d = x.shape
    nf = D_FF // tf
    return pl.pallas_call(
        functools.partial(_ffn_sample_kernel, layer == DEPTH - 1),
        grid=(nf,),
        in_specs=[
            pl.BlockSpec((m, d), lambda f: (0, 0)),
            pl.BlockSpec((None, 1, d), lambda f: (layer, 0, 0)),
            pl.BlockSpec((None, d, tf), lambda f: (layer, 0, f)),
            pl.BlockSpec((None, d, tf), lambda f: (layer, 0, f + nf)),
            pl.BlockSpec((None, tf, d), lambda f: (layer, f, 0)),
            pl.BlockSpec((1, d), lambda f: (0, 0)),
        ],
        out_specs=[
            pl.BlockSpec((m, d), lambda f: (0, 0)),
            pl.BlockSpec((d, tf), lambda f: (0, f)),
            pl.BlockSpec((d, tf), lambda f: (0, f)),
            pl.BlockSpec((tf, d), lambda f: (f, 0)),
        ],
        out_shape=[jax.ShapeDtypeStruct((m, d), F32), jax.ShapeDtypeStruct((d, D_FF), BF16),
                   jax.ShapeDtypeStruct((d, D_FF), BF16), jax.ShapeDtypeStruct((D_FF, d), BF16)],
        scratch_shapes=[pltpu.VMEM((m, d), BF16)],
        compiler_params=_cparams("arbitrary"),
        name="ffn_sample",
    )(x, g, w_gate_up, w_gate_up, w_down, g_final)


def _ffn_kernel(last_layer, x_ref, g_ref, wg_ref, wu_ref, wd_ref, gf_ref, y_ref, h_ref):
    f = pl.program_id(1)
    rs = x_ref.shape[0] // NORM_ROW_SPLIT

    @pl.when(f == 0)
    def _():
        for r in range(NORM_ROW_SPLIT):
            rows = slice(r * rs, (r + 1) * rs)
            x = x_ref[rows, :]
            h = _rms(x, g_ref[...]).astype(BF16)
            h_ref[rows, :] = h
            act = _silu(_dot(h, wg_ref[...])) * _dot(h, wu_ref[...])
            y_ref[rows, :] = x + _dot(act.astype(BF16), wd_ref[...])

    @pl.when(f > 0)
    def _():
        _ffn_step(h_ref, wg_ref[...], wu_ref[...], wd_ref[...], y_ref)

    if last_layer:
        @pl.when(f == pl.num_programs(1) - 1)
        def _():
            y_ref[...] = _rms(y_ref[...], gf_ref[...])


def ffn(x, g, wgb, wub, wdb, g_final, layer, tm, tf):
    m, d = x.shape
    return pl.pallas_call(
        functools.partial(_ffn_kernel, layer == DEPTH - 1),
        grid=(m // tm, D_FF // tf),
        in_specs=[
            pl.BlockSpec((tm, d), lambda i, f: (i, 0)),
            pl.BlockSpec((None, 1, d), lambda i, f: (layer, 0, 0)),
            pl.BlockSpec((d, tf), lambda i, f: (0, f)),
            pl.BlockSpec((d, tf), lambda i, f: (0, f)),
            pl.BlockSpec((tf, d), lambda i, f: (f, 0)),
            pl.BlockSpec((1, d), lambda i, f: (0, 0)),
        ],
        out_specs=pl.BlockSpec((tm, d), lambda i, f: (i, 0)),
        out_shape=jax.ShapeDtypeStruct((m, d), F32),
        scratch_shapes=[pltpu.VMEM((tm, d), BF16)],
        compiler_params=_cparams("parallel", "arbitrary"),
        name="ffn",
    )(x, g, wgb, wub, wdb, g_final)


def _rope_tables(pos):
    half = ROT_DIM // 2
    inv = ROPE_THETA ** (-jnp.arange(half, dtype=F32) * 2.0 / ROT_DIM)
    ang = pos.astype(F32)[:, None] * inv[None, :]
    cos, sin = jnp.cos(ang), jnp.sin(ang)
    n = pos.shape[0]
    ones = jnp.ones((n, HEAD_DIM - ROT_DIM), F32)
    zeros = jnp.zeros((n, HEAD_DIM - ROT_DIM), F32)
    zh = jnp.zeros((n, half), F32)
    c = jnp.concatenate([cos, cos, ones], axis=-1)
    s_lo = jnp.concatenate([-sin, zh, zeros], axis=-1)
    s_hi = jnp.concatenate([zh, sin, zeros], axis=-1)
    rep = lambda t: jnp.concatenate([t, t], axis=-1)
    return rep(c), rep(s_lo), rep(s_hi)


def _rope(x, c, s_lo, s_hi):
    half = ROT_DIM // 2
    outs = []
    for j in range(x.shape[1] // 128):
        xc = x[:, j * 128:(j + 1) * 128]
        outs.append(xc * c + pltpu.roll(xc, 128 - half, 1) * s_lo + pltpu.roll(xc, half, 1) * s_hi)
    return outs[0] if len(outs) == 1 else jnp.concatenate(outs, axis=1)


def _rope_tables_t(pos):
    half = ROT_DIM // 2
    inv = ROPE_THETA ** (-jnp.arange(half, dtype=F32) * 2.0 / ROT_DIM)
    ang = inv[:, None] * pos.astype(F32)[None, :]
    return jnp.cos(ang), jnp.sin(ang)


def _rope_t(xt, cos_t, sin_t):
    half = ROT_DIM // 2
    pieces = []
    for base in range(0, xt.shape[0], HEAD_DIM):
        x1 = xt[base:base + half]
        x2 = xt[base + half:base + ROT_DIM]
        pieces += [x1 * cos_t - x2 * sin_t, x2 * cos_t + x1 * sin_t, xt[base + ROT_DIM:base + HEAD_DIM]]
    return jnp.concatenate(pieces, axis=0)


def _softmax_sink_pv(s, sk, v):
    m = jnp.maximum(jnp.max(s, axis=-1, keepdims=True), sk)
    p = jnp.exp(s - m)
    denom = jnp.sum(p, axis=-1, keepdims=True) + jnp.exp(sk - m)
    return _dot(p.astype(BF16), v) / denom


def _attn_prompt_kernel(layer, sink_ref, q_ref, k_ref, v_ref, c_ref, slo_ref, shi_ref, ct_ref, st_ref,
                        a_ref, ko_ref, vo_ref, kprev_ref, vtprev_ref):
    n = pl.program_id(1)
    nstep = pl.num_programs(1)
    blk = WINDOW
    nsub = q_ref.shape[1] // blk
    lanes = Q_PER_KV * blk
    sub = lambda j: slice(j * blk, (j + 1) * blk)

    k_all = _rope(k_ref[0], c_ref[...], slo_ref[...], shi_ref[...])
    v_all = v_ref[0]
    vt_all = v_all.T
    qt_all = (_rope_t(q_ref[0].T, ct_ref[...], st_ref[...]) * ATTN_SCALE).astype(BF16)

    key = lax.broadcasted_iota(jnp.int32, (blk, lanes), 0)
    qry = lax.broadcasted_iota(jnp.int32, (blk, lanes), 1) % blk
    newer = key > qry
    zero_rows = jnp.zeros((HEAD_DIM, lanes), BF16)

    def scores(h, j, first):
        pair = slice((h // 2) * 2 * HEAD_DIM, (h // 2 + 1) * 2 * HEAD_DIM)
        qt = qt_all[:, sub(j)]
        qh = jnp.concatenate(
            [qt[(Q_PER_KV * h + g) * HEAD_DIM:(Q_PER_KV * h + g + 1) * HEAD_DIM] for g in range(Q_PER_KV)], axis=1)
        qz = jnp.concatenate([qh, zero_rows] if h % 2 == 0 else [zero_rows, qh], axis=0)
        s_cur = _dot(k_all[sub(j), pair].astype(BF16), qz)
        if first:
            return None, s_cur
        k_prev = kprev_ref[:, pair] if j == 0 else k_all[sub(j - 1), pair]
        return _dot(k_prev.astype(BF16), qz), s_cur

    def softmax(h, s_prev, s_cur):
        sink = jnp.concatenate(
            [jnp.full((1, blk), sink_ref[layer, Q_PER_KV * h + g], F32) for g in range(Q_PER_KV)], axis=1)
        if s_prev is None:
            f = jnp.where(newer, NEG_BIG, s_cur)
            m = jnp.maximum(jnp.max(f, axis=0, keepdims=True), sink)
            p = jnp.exp(f - m)
            denom = jnp.sum(p, axis=0, keepdims=True) + jnp.exp(sink - m)
            return p.astype(BF16), None, denom
        f = jnp.where(newer, s_prev, s_cur)
        d = jnp.sum(jnp.where(key == qry, s_prev, 0.0), axis=0, keepdims=True)
        m = jnp.maximum(jnp.maximum(jnp.max(f, axis=0, keepdims=True), d), sink)
        p = jnp.exp(f - m)
        pd = jnp.exp(d - m)
        denom = jnp.sum(p, axis=0, keepdims=True) + pd + jnp.exp(sink - m)
        pp = jnp.concatenate([jnp.where(newer, p, 0.0), jnp.where(newer, 0.0, p)], axis=0).astype(BF16)
        return pp, pd, denom

    def weighted_values(h, j, pp, pd, denom):
        rows = slice(h * HEAD_DIM, (h + 1) * HEAD_DIM)
        vt_cur = vt_all[rows, sub(j)]
        inv = 1.0 / denom
        if pd is None:
            return _dot(vt_cur.astype(BF16), pp) * inv
        vt_prev = vtprev_ref[rows, :] if j == 0 else vt_all[rows, sub(j - 1)]
        o = _dot(jnp.concatenate([vt_prev, vt_cur], axis=1).astype(BF16), pp)
        return (o + pd * jnp.concatenate([vt_prev] * Q_PER_KV, axis=1)) * inv

    def run(first_step):
        units = [(h, j) for j in range(nsub) for h in range(N_KV_HEADS)]
        s = {u: scores(*u, first_step and u[1] == 0) for u in units}
        p = {u: softmax(u[0], *s[u]) for u in units}
        o = {u: weighted_values(*u, *p[u]) for u in units}
        for j in range(nsub):
            outs = [o[h, j][:, g * blk:(g + 1) * blk] for h in range(N_KV_HEADS) for g in range(Q_PER_KV)]
            a_ref[0, sub(j), :] = jnp.concatenate(outs, axis=0).T.astype(a_ref.dtype)

    @pl.when(n == 0)
    def _():
        run(True)

    @pl.when(n > 0)
    def _():
        run(False)

    kprev_ref[...] = k_all[sub(nsub - 1)]
    vtprev_ref[...] = vt_all[:, sub(nsub - 1)]

    @pl.when(n == nstep - 1)
    def _():
        ko_ref[0] = k_all[sub(nsub - 1)]
        vo_ref[0] = v_all[sub(nsub - 1)]


def attn_prompt(proj, sink, tables, tables_t, layer, rows):
    b, l, _ = proj.shape
    blk = WINDOW
    tab_spec = pl.BlockSpec((rows, 128), lambda i, n: (n, 0))
    tab_t_spec = pl.BlockSpec((ROT_DIM // 2, rows), lambda i, n: (0, n))
    kv_out = jax.ShapeDtypeStruct((b, blk, KV_WIDTH), F32)
    return pl.pallas_call(
        functools.partial(_attn_prompt_kernel, layer),
        grid=(b, l // rows),
        in_specs=[
            pl.BlockSpec(memory_space=pltpu.SMEM),
            pl.BlockSpec((1, rows, ATTN_WIDTH), lambda i, n: (i, n, Q_BLK_1024)),
            pl.BlockSpec((1, rows, KV_WIDTH), lambda i, n: (i, n, K_BLK_256)),
            pl.BlockSpec((1, rows, KV_WIDTH), lambda i, n: (i, n, V_BLK_256)),
            tab_spec, tab_spec, tab_spec, tab_t_spec, tab_t_spec,
        ],
        out_specs=[
            pl.BlockSpec((1, rows, ATTN_WIDTH), lambda i, n: (i, n, 0)),
            pl.BlockSpec((1, blk, KV_WIDTH), lambda i, n: (i, 0, 0)),
            pl.BlockSpec((1, blk, KV_WIDTH), lambda i, n: (i, 0, 0)),
        ],
        out_shape=[jax.ShapeDtypeStruct((b, l, ATTN_WIDTH), BF16), kv_out, kv_out],
        scratch_shapes=[pltpu.VMEM((blk, KV_WIDTH), F32), pltpu.VMEM((KV_WIDTH, blk), F32)],
        compiler_params=_cparams("parallel", "arbitrary"),
        name="attn_prompt",
    )(sink, proj, proj, proj, *tables, *tables_t)


def _attn_sample_kernel(layer, sink_ref, q_ref, k_ref, v_ref, kc_ref, vc_ref, c_ref, slo_ref, shi_ref,
                        k_carried_ref, v_carried_ref, a_ref, ko_ref, vo_ref):
    del k_carried_ref, v_carried_ref
    r8 = SAMPLE_ROWS
    nseq = r8 // SAMPLE_T
    lb = kc_ref.shape[1]
    c, s_lo, s_hi = c_ref[...], slo_ref[...], shi_ref[...]
    q = _rope(q_ref[...], c, s_lo, s_hi)
    k = _rope(k_ref[...], c, s_lo, s_hi)
    v = v_ref[...]
    pad = jnp.zeros((lb - r8, KV_WIDTH), F32)
    k_pad = jnp.concatenate([k, pad], axis=0)
    v_pad = jnp.concatenate([v, pad], axis=0)

    zeros = jnp.zeros((r8, HEAD_DIM), F32)
    qx = jnp.concatenate(
        [jnp.concatenate([q[:, hg * HEAD_DIM:(hg + 1) * HEAD_DIM] if slot == hg // Q_PER_KV else zeros
                          for slot in range(N_KV_HEADS)], axis=1) for hg in range(N_Q_HEADS)],
        axis=0).astype(BF16)
    sk = jnp.concatenate([jnp.full((r8, 1), sink_ref[layer, hg], F32) for hg in range(N_Q_HEADS)], axis=0)

    rows = N_Q_HEADS * r8
    ri = lax.broadcasted_iota(jnp.int32, (rows, 2 * lb), 0) % r8
    t = ri % SAMPLE_T
    kj = lax.broadcasted_iota(jnp.int32, (rows, 2 * lb), 1)
    cj = kj - lb
    row_seq = lax.broadcasted_iota(jnp.int32, (rows, KV_WIDTH), 0) % r8 // SAMPLE_T

    s = []
    for e in range(nseq):
        kk = jnp.concatenate([kc_ref[e], k_pad], axis=0).astype(BF16)
        s.append(_dot_nt(qx, kk) * ATTN_SCALE)
    o = None
    for e in range(nseq):
        mask = ((kj < lb) & (kj >= t)) | (
            (cj >= e * SAMPLE_T) & (cj < (e + 1) * SAMPLE_T) & (cj - e * SAMPLE_T <= t))
        vv = jnp.concatenate([vc_ref[e], v_pad], axis=0).astype(BF16)
        oe = _softmax_sink_pv(jnp.where(mask, s[e], NEG_BIG), sk, vv)
        o = oe if o is None else jnp.where(row_seq == e, oe, o)
    a_ref[...] = jnp.concatenate(
        [o[hg * r8:(hg + 1) * r8, (hg // Q_PER_KV) * HEAD_DIM:(hg // Q_PER_KV + 1) * HEAD_DIM]
         for hg in range(N_Q_HEADS)], axis=1)

    row = lax.broadcasted_iota(jnp.int32, (lb, KV_WIDTH), 0)
    for e in range(nseq):
        shift = (lb - SAMPLE_T - e * SAMPLE_T) % lb
        for new_pad, cache_ref, out_ref in ((k_pad, kc_ref, ko_ref), (v_pad, vc_ref, vo_ref)):
            new_rows = new_pad if shift == 0 else pltpu.roll(new_pad, shift, 0)
            old_rows = pltpu.roll(cache_ref[e], lb - SAMPLE_T, 0)
            out_ref[e] = jnp.where(row >= lb - SAMPLE_T, new_rows, old_rows)


def attn_sample(proj, k_cache, v_cache, k_new, v_new, sink, tables, layer):
    m = proj.shape[0]
    r8 = SAMPLE_ROWS
    nseq = r8 // SAMPLE_T
    _, b, lb, _ = k_cache.shape
    tab_spec = pl.BlockSpec((r8, 128), lambda i: (0, 0))
    cache_spec = pl.BlockSpec((None, nseq, lb, KV_WIDTH), lambda i: (layer, i, 0, 0))
    carried = pl.BlockSpec(memory_space=pl.ANY)
    kv_out = jax.ShapeDtypeStruct(k_new.shape, F32)
    return pl.pallas_call(
        functools.partial(_attn_sample_kernel, layer),
        grid=(m // r8,),
        in_specs=[
            pl.BlockSpec(memory_space=pltpu.SMEM),
            pl.BlockSpec((r8, ATTN_WIDTH), lambda i: (i, Q_BLK_1024)),
            pl.BlockSpec((r8, KV_WIDTH), lambda i: (i, K_BLK_256)),
            pl.BlockSpec((r8, KV_WIDTH), lambda i: (i, V_BLK_256)),
            cache_spec, cache_spec,
            tab_spec, tab_spec, tab_spec,
            carried, carried,
        ],
        out_specs=[pl.BlockSpec((r8, ATTN_WIDTH), lambda i: (i, 0)), cache_spec, cache_spec],
        out_shape=[jax.ShapeDtypeStruct((m, ATTN_WIDTH), F32), kv_out, kv_out],
        input_output_aliases={9: 1, 10: 2},
        compiler_params=_cparams("parallel"),
        name="attn_sample",
    )(sink, proj, proj, proj, k_cache, v_cache, *tables, k_new, v_new)


def _lower_bound(lbr_ref, layer, hs):
    rows = [lbr_ref[r:r + 1, hs] for r in range(DEPTH)]
    mx = functools.reduce(jnp.maximum, rows)
    es = [jnp.exp(r - mx) for r in rows]
    tot = functools.reduce(lambda a, b: a + b, es)
    lbp = [e / tot for e in es]
    acc = lbp[0]
    for r in range(1, layer + 1):
        acc = acc + lbp[r]
    return acc - lbp[0]


def _hgrn_gates(hq, hf, lb):
    q = _silu(hq) * (HG_DK ** -0.5)
    e = jnp.exp(-jnp.abs(hf))
    r = 1.0 / (1.0 + e)
    pos = hf >= 0.0
    sig_pos = jnp.where(pos, r, e * r)
    sig_neg = jnp.where(pos, e * r, r)
    log_f = jnp.log(jnp.maximum(lb, LB_TINY) + (1.0 - lb) * sig_pos)
    k = (1.0 - lb) * sig_neg
    return q, log_f, k


def _hgrn_finish(o, g_norm, hg):
    return _rms(o, g_norm) * _silu(hg)


def _hgrn_levels():
    s = HG_CHUNK // 2
    while s >= 1:
        yield s
        s //= 2


def _hgrn_level_masks():
    t = np.arange(HG_CHUNK)[:, None]
    s = np.arange(HG_CHUNK)[None, :]
    return np.stack([((t % (2 * h) >= h) & (t // (2 * h) == s // (2 * h)) & (s % (2 * h) < h))
                     for h in _hgrn_levels()]).astype(np.float32)


def _hgrn_pair_reference(g_ref, gcum, half, sub8):
    c = HG_CHUNK
    pair = 2 * half
    bcast = lambda r: jnp.broadcast_to(g_ref[r:r + 1, :], (8, HG_DK))
    if pair >= 8:
        return jnp.concatenate([bcast((8 * v // pair) * pair + half - 1) for v in range(c // 8)], axis=0)
    if pair == 4:
        lo = jnp.concatenate([bcast(8 * v + 1) for v in range(c // 8)], axis=0)
        hi = jnp.concatenate([bcast(8 * v + 5) for v in range(c // 8)], axis=0)
        return jnp.where(sub8 < 4, lo, hi)
    return jnp.where(sub8 % 2 == 1, pltpu.roll(gcum, 1, 0), gcum)


def _hgrn_level_operand(q, k, gcum, g_ref, half, row):
    ref = _hgrn_pair_reference(g_ref, gcum, half, row % 8)
    right = row % (2 * half) >= half
    return (jnp.where(right, q, k) * jnp.exp(-jnp.abs(gcum - ref))).astype(BF16)


def _chunk_cumsum(x, buf, row):
    sh = 1
    while sh < HG_CHUNK:
        x = x + jnp.where(row >= sh, pltpu.roll(x, sh, 0), 0.0)
        sh *= 2
    buf[...] = x
    return x


def _hgrn_tile(layer, hq_ref, hf_ref, hi_ref, hg_ref, lbr_ref, gn_ref, m_ref, g_ref, state, emit):
    c = HG_CHUNK
    nchunk = hq_ref.shape[1] // c
    heads = range(HG_HEADS)
    units = [(h, cc) for cc in range(nchunk) for h in heads]
    row = lax.broadcasted_iota(jnp.int32, (c, HG_DK), 0)
    cols = lambda h: slice(h * HG_DK, (h + 1) * HG_DK)
    rws = lambda cc: slice(cc * c, (cc + 1) * c)
    gbuf = lambda h, cc: g_ref.at[h * nchunk + cc]
    lbs = [_lower_bound(lbr_ref, layer, cols(h)) for h in heads]

    q, k, v, gcum = {}, {}, {}, {}
    for u in units:
        h, cc = u
        q[u], log_f, k[u] = _hgrn_gates(hq_ref[0, rws(cc), cols(h)], hf_ref[0, rws(cc), cols(h)], lbs[h])
        v[u] = hi_ref[0, rws(cc), cols(h)]
        gcum[u] = _chunk_cumsum(log_f, gbuf(h, cc), row)

    a = {u: jnp.zeros((c, c), F32) for u in units}
    for lvl, half in enumerate(_hgrn_levels()):
        for u in units:
            x = _hgrn_level_operand(q[u], k[u], gcum[u], gbuf(*u), half, row)
            a[u] = a[u] + _dot_nt(x, x) * m_ref[lvl]

    o_intra, qs, ks, decay = {}, {}, {}, {}
    for u in units:
        vb = v[u].astype(BF16)
        o_intra[u] = _dot(a[u].astype(BF16), vb) + jnp.sum(q[u] * k[u], axis=-1, keepdims=True) * v[u]
        g_last = gbuf(*u)[c - 1:c, :]
        qs[u] = (q[u] * jnp.exp(gcum[u])).astype(BF16)
        ks[u] = (k[u] * jnp.exp(g_last - gcum[u])).astype(BF16)
        decay[u] = jnp.transpose(jnp.broadcast_to(jnp.exp(g_last), (HG_DK, HG_DK)))

    state = list(state)
    for cc in range(nchunk):
        o_inter = [_dot(qs[h, cc], state[h].astype(BF16)) for h in heads]
        update = [_dot_tn(ks[h, cc], v[h, cc].astype(BF16)) for h in heads]
        for h in heads:
            o = o_inter[h] + o_intra[h, cc]
            emit(h, rws(cc), _hgrn_finish(o, gn_ref[h:h + 1, :], hg_ref[0, rws(cc), cols(h)]))
            state[h] = state[h] * decay[h, cc] + update[h]
    return state


def _hgrn_prompt_kernel(layer, hq_ref, hf_ref, hi_ref, hg_ref, lbr_ref, gn_ref, m_ref,
                        o_ref, so_ref, s_ref, g_ref):
    step = pl.program_id(1)
    heads = range(HG_HEADS)

    @pl.when(step == 0)
    def _():
        s_ref[...] = jnp.zeros_like(s_ref)

    def emit(h, rs, o):
        o_ref[0, rs, h * HG_DV:(h + 1) * HG_DV] = o.astype(o_ref.dtype)

    state = _hgrn_tile(layer, hq_ref, hf_ref, hi_ref, hg_ref, lbr_ref, gn_ref, m_ref, g_ref,
                       [s_ref[h] for h in heads], emit)
    for h in heads:
        s_ref[h] = state[h]

    @pl.when(step == pl.num_programs(1) - 1)
    def _():
        so_ref[0] = s_ref[...]


def hgrn_prompt(proj, lb_raw, g_norm, layer, rows):
    b, l, _ = proj.shape
    c = HG_CHUNK
    masks = jnp.asarray(_hgrn_level_masks())
    col = lambda blk: pl.BlockSpec((1, rows, HG_WIDTH), lambda i, n: (i, n, blk))
    return pl.pallas_call(
        functools.partial(_hgrn_prompt_kernel, layer),
        grid=(b, l // rows),
        in_specs=[
            col(HQ_BLK_512), col(HF_BLK_512), col(HI_BLK_512), col(HG_BLK_512),
            pl.BlockSpec((DEPTH, HG_WIDTH), lambda i, n: (0, 0)),
            pl.BlockSpec((None, HG_HEADS, HG_DV), lambda i, n: (layer, 0, 0)),
            pl.BlockSpec(masks.shape, lambda i, n: (0, 0, 0)),
        ],
        out_specs=[
            pl.BlockSpec((1, rows, HG_WIDTH), lambda i, n: (i, n, 0)),
            pl.BlockSpec((1, HG_HEADS, HG_DK, HG_DV), lambda i, n: (i, 0, 0, 0)),
        ],
        out_shape=[jax.ShapeDtypeStruct((b, l, HG_WIDTH), BF16),
                   jax.ShapeDtypeStruct((b, HG_HEADS, HG_DK, HG_DV), F32)],
        scratch_shapes=[pltpu.VMEM((HG_HEADS, HG_DK, HG_DV), F32),
                        pltpu.VMEM((HG_HEADS * (rows // c), c, HG_DK), F32)],
        compiler_params=_cparams("parallel", "arbitrary"),
        name="hgrn_prompt",
    )(proj, proj, proj, proj, lb_raw, g_norm, masks)


def _hgrn_sample_kernel(layer, hq_ref, hf_ref, hi_ref, hg_ref, lbr_ref, gn_ref, s_ref, carried_ref,
                        o_ref, so_ref, q_ref, k_ref, g_ref):
    del carried_ref
    r8, tt = SAMPLE_ROWS, SAMPLE_T
    nseq = r8 // tt
    row = lax.broadcasted_iota(jnp.int32, (r8, HG_DK), 0)
    t_idx = row % tt
    seq = row // tt

    def pick(ref, s):
        out = ref[s:s + 1, :]
        for e in range(1, nseq):
            out = jnp.where(seq == e, ref[e * tt + s:e * tt + s + 1, :], out)
        return out

    for h in range(HG_HEADS):
        hs = slice(h * HG_DK, (h + 1) * HG_DK)
        lb = _lower_bound(lbr_ref, layer, hs)
        q, log_f, k = _hgrn_gates(hq_ref[:, hs], hf_ref[:, hs], lb)
        gcum = log_f
        sh = 1
        while sh < tt:
            gcum = gcum + jnp.where(t_idx >= sh, pltpu.roll(gcum, sh, 0), 0.0)
            sh *= 2
        q_ref[...] = q
        k_ref[...] = k
        g_ref[...] = gcum
        v = hi_ref[:, hs]
        g_last = pick(g_ref, tt - 1)
        qs = (q * jnp.exp(gcum)).astype(BF16)
        ks_all = k * jnp.exp(g_last - gcum)

        o = jnp.zeros((r8, HG_DV), F32)
        for e in range(nseq):
            state = s_ref[e, h]
            o = jnp.where(seq == e, _dot(qs, state.astype(BF16)), o)
            g_last_e = g_ref[e * tt + tt - 1:e * tt + tt, :]
            decay = jnp.transpose(jnp.broadcast_to(jnp.exp(g_last_e), (HG_DK, HG_DK)))
            ks_e = jnp.where(seq == e, ks_all, 0.0).astype(BF16)
            so_ref[e, h] = state * decay + _dot_tn(ks_e, v.astype(BF16))
        for s in range(tt):
            dec = jnp.exp(jnp.where(t_idx >= s, gcum - pick(g_ref, s), NEG_BIG))
            col = jnp.sum(q * (dec * pick(k_ref, s)), axis=-1, keepdims=True)
            o = o + col * pick(hi_ref.at[:, hs], s)
        o_ref[:, hs] = _hgrn_finish(o, gn_ref[h:h + 1, :], hg_ref[:, hs])


def hgrn_sample(proj, state, state_new, lb_raw, g_norm, layer):
    m = proj.shape[0]
    r8 = SAMPLE_ROWS
    nseq = r8 // SAMPLE_T
    col = lambda blk: pl.BlockSpec((r8, HG_WIDTH), lambda i: (i, blk))
    st_spec = pl.BlockSpec((None, nseq, HG_HEADS, HG_DK, HG_DV), lambda i: (layer, i, 0, 0, 0))
    return pl.pallas_call(
        functools.partial(_hgrn_sample_kernel, layer),
        grid=(m // r8,),
        in_specs=[
            col(HQ_BLK_512), col(HF_BLK_512), col(HI_BLK_512), col(HG_BLK_512),
            pl.BlockSpec((DEPTH, HG_WIDTH), lambda i: (0, 0)),
            pl.BlockSpec((None, HG_HEADS, HG_DV), lambda i: (layer, 0, 0)),
            st_spec,
            pl.BlockSpec(memory_space=pl.ANY),
        ],
        out_specs=[pl.BlockSpec((r8, HG_WIDTH), lambda i: (i, 0)), st_spec],
        out_shape=[jax.ShapeDtypeStruct((m, HG_WIDTH), F32), jax.ShapeDtypeStruct(state_new.shape, F32)],
        input_output_aliases={7: 1},
        scratch_shapes=[pltpu.VMEM((r8, HG_DK), F32), pltpu.VMEM((r8, HG_DK), F32), pltpu.VMEM((r8, HG_DK), F32)],
        compiler_params=_cparams("parallel"),
        name="hgrn_sample",
    )(proj, proj, proj, proj, lb_raw, g_norm, state, state_new)


def _conv_sample_kernel(cb_ref, cc_ref, ch_ref, w_ref, st_ref, y_ref, so_ref, u_ref):
    r8, tt = SAMPLE_ROWS, SAMPLE_T
    nseq = r8 // tt
    u = cc_ref[...] * ch_ref[...]
    u_ref[...] = u
    row = lax.broadcasted_iota(jnp.int32, u.shape, 0)
    t_idx = row % tt
    seq = row // tt
    last1 = st_ref[0, 1:2, :]
    last2 = st_ref[0, 0:1, :]
    for e in range(1, nseq):
        last1 = jnp.where(seq == e, st_ref[e, 1:2, :], last1)
        last2 = jnp.where(seq == e, st_ref[e, 0:1, :], last2)
    um1 = jnp.where(t_idx == 0, last1, pltpu.roll(u, 1, 0))
    um2 = jnp.where(t_idx == 0, last2, jnp.where(t_idx == 1, last1, pltpu.roll(u, 2, 0)))
    y_ref[...] = cb_ref[...] * _conv_taps(u, um1, um2, w_ref)
    for e in range(nseq):
        so_ref[e] = u_ref[(e + 1) * tt - (CONV_K - 1):(e + 1) * tt, :]


def conv_sample(proj, state, w, layer):
    m = proj.shape[0]
    r8 = SAMPLE_ROWS
    nseq = r8 // SAMPLE_T
    col = lambda blk: pl.BlockSpec((r8, CONV_WIDTH), lambda i: (i, blk))
    st_in = pl.BlockSpec((None, nseq, CONV_K - 1, CONV_WIDTH), lambda i: (layer, i, 0, 0))
    st_spec = pl.BlockSpec((nseq, CONV_K - 1, CONV_WIDTH), lambda i: (i, 0, 0))
    return pl.pallas_call(
        _conv_sample_kernel,
        grid=(m // r8,),
        in_specs=[col(CB_BLK_512), col(CC_BLK_512), col(CH_BLK_512),
                  pl.BlockSpec((None, CONV_K, CONV_WIDTH), lambda i: (layer, 0, 0)), st_in],
        out_specs=[pl.BlockSpec((r8, CONV_WIDTH), lambda i: (i, 0)), st_spec],
        out_shape=[jax.ShapeDtypeStruct((m, CONV_WIDTH), F32), jax.ShapeDtypeStruct(state.shape[1:], F32)],
        scratch_shapes=[pltpu.VMEM((r8, CONV_WIDTH), F32)],
        compiler_params=_cparams("parallel"),
        name="conv_sample",
    )(proj, proj, proj, w, state)


def kernel(x_prompt, x_sample, cache_attn_k, cache_attn_v, state_hgrn, state_conv, w_in, attn_sink,
           hgrn_lower_bounds, hgrn_norm, conv_w, w_out, norm_mix, norm_ffn, w_gate_up, w_down, norm_final):
    bp, lp, d = x_prompt.shape
    bs, ls, _ = x_sample.shape
    assert ls == SAMPLE_T and (bs * ls) % SAMPLE_ROWS == 0
    lb = cache_attn_k.shape[2]
    mp, ms = bp * lp, bs * ls

    pos_p = jnp.arange(lp, dtype=jnp.int32)
    tab_p = _rope_tables(pos_p)
    tab_pt = _rope_tables_t(pos_p)
    pos_s = PAST_LEN + jnp.arange(SAMPLE_ROWS, dtype=jnp.int32) % SAMPLE_T
    tab_s = _rope_tables(pos_s)

    xp = x_prompt.reshape(mp, d)
    xs = x_sample.reshape(ms, d)
    kc = cache_attn_k.reshape(DEPTH, bs, lb, KV_WIDTH)
    vc = cache_attn_v.reshape(DEPTH, bs, lb, KV_WIDTH)

    g_mix = norm_mix.reshape(DEPTH, 1, d)
    g_ffn = norm_ffn.reshape(DEPTH, 1, d)
    g_final = norm_final.reshape(1, d)

    k_s = jnp.zeros(kc.shape, F32)
    v_s = jnp.zeros(vc.shape, F32)
    s_s = jnp.zeros(state_hgrn.shape, F32)

    outs = {name: [] for name in ("kp", "vp", "sp", "cp", "cs")}
    for l in range(DEPTH):
        proj_s, w_in_b = in_proj_sample(xs, g_mix, w_in, l, 1024)
        a_s, k_s, v_s = attn_sample(proj_s, kc, vc, k_s, v_s, attn_sink, tab_s, l)
        o_s, s_s = hgrn_sample(proj_s, state_hgrn, s_s, hgrn_lower_bounds, hgrn_norm, l)
        c_s, cst_s = conv_sample(proj_s, state_conv, conv_w, l)
        xs, w_out_b = out_proj_sample(xs, a_s, o_s, c_s, w_out, l, 512)
        xs, w_gate_b, w_up_b, w_down_b = ffn_sample(xs, g_ffn, w_gate_up, w_down, g_final, l, 512)

        proj_p = in_proj(xp, g_mix, w_in_b, l, 1024, 1024)
        proj_p3 = proj_p.reshape(bp, lp, IN_WIDTH)
        a_p, k_p, v_p = attn_prompt(proj_p3, attn_sink, tab_p, tab_pt, l, 4 * WINDOW)
        o_p, s_p = hgrn_prompt(proj_p3, hgrn_lower_bounds, hgrn_norm, l, 4 * HG_CHUNK)
        xp, cst_p = out_proj(xp, a_p.reshape(mp, ATTN_WIDTH), o_p.reshape(mp, HG_WIDTH), proj_p,
                             conv_w, w_out_b, l, lp, 512)
        xp = ffn(xp, g_ffn, w_gate_b, w_up_b, w_down_b, g_final, l, 1024, 512)

        outs["kp"].append(k_p.reshape(bp, WINDOW, N_KV_HEADS, HEAD_DIM))
        outs["vp"].append(v_p.reshape(bp, WINDOW, N_KV_HEADS, HEAD_DIM))
        outs["sp"].append(s_p)
        outs["cp"].append(cst_p)
        outs["cs"].append(cst_s)

    st = lambda name: jnp.stack(outs[name])
    kv_shape = (DEPTH, bs, lb, N_KV_HEADS, HEAD_DIM)
    return (xp.reshape(bp, lp, d), xs.reshape(bs, ls, d), st("kp"), st("vp"), st("sp"), st("cp"),
            k_s.reshape(kv_shape), v_s.reshape(kv_shape), s_s, st("cs"))
```

```python
import functools

import jax
import jax.numpy as jnp
import numpy as np
from jax import lax
from jax.experimental import pallas as pl
from jax.experimental.pallas import tpu as pltpu

F32 = jnp.float32
BF16 = jnp.bfloat16

D_MODEL = 2048
DEPTH = 4
PAST_LEN = 16384
HEAD_DIM = 64
N_Q_HEADS = 16
N_KV_HEADS = 4
Q_PER_KV = 4
ATTN_WIDTH = 1024
KV_WIDTH = 256
WINDOW = 128
ROPE_THETA = 500000.0
ROT_DIM = 16
ATTN_SCALE = HEAD_DIM ** -0.5
NEG_BIG = -1e30
HG_HEADS = 4
HG_DK = 128
HG_DV = 128
HG_WIDTH = 512
HG_CHUNK = 64
LB_TINY = 1e-30
CONV_K = 3
CONV_WIDTH = 512
D_FF = 5632
IN_WIDTH = 5120
NORM_EPS = 1e-6
SAMPLE_T = 4
SAMPLE_ROWS = 8

Q_BLK_1024 = 0
K_BLK_256 = 4
V_BLK_256 = 5
HQ_BLK_512, HF_BLK_512, HI_BLK_512, HG_BLK_512 = 3, 4, 5, 6
CB_BLK_512, CC_BLK_512, CH_BLK_512 = 7, 8, 9

VMEM_LIMIT = 56 * 1024 * 1024


def _cparams(*sem):
    return pltpu.CompilerParams(dimension_semantics=sem, vmem_limit_bytes=VMEM_LIMIT)


def _rms(x, g):
    ms = jnp.mean(x * x, axis=-1, keepdims=True)
    return x * lax.rsqrt(ms + NORM_EPS) * g


def _silu(x):
    return x * (1.0 / (1.0 + jnp.exp(-x)))


def _dot(a, b):
    return jnp.dot(a, b, preferred_element_type=F32)


def _dot_nt(a, b):
    return lax.dot_general(a, b, (((1,), (1,)), ((), ())), preferred_element_type=F32)


def _dot_tn(a, b):
    return lax.dot_general(a, b, (((0,), (0,)), ((), ())), preferred_element_type=F32)


def _in_proj_sample_kernel(x_ref, g_ref, w_ref, o_ref, wb_ref, h_ref):
    @pl.when(pl.program_id(0) == 0)
    def _():
        h_ref[...] = _rms(x_ref[...], g_ref[...]).astype(BF16)

    wb = w_ref[...].astype(BF16)
    wb_ref[...] = wb
    o_ref[...] = _dot(h_ref[...], wb)


def in_proj_sample(x, g, w, layer, tn):
    m, k = x.shape
    n = w.shape[2]
    return pl.pallas_call(
        _in_proj_sample_kernel,
        grid=(n // tn,),
        in_specs=[
            pl.BlockSpec((m, k), lambda j: (0, 0)),
            pl.BlockSpec((None, 1, k), lambda j: (layer, 0, 0)),
            pl.BlockSpec((None, k, tn), lambda j: (layer, 0, j)),
        ],
        out_specs=[pl.BlockSpec((m, tn), lambda j: (0, j)), pl.BlockSpec((k, tn), lambda j: (0, j))],
        out_shape=[jax.ShapeDtypeStruct((m, n), F32), jax.ShapeDtypeStruct((k, n), BF16)],
        scratch_shapes=[pltpu.VMEM((m, k), BF16)],
        compiler_params=_cparams("arbitrary"),
        name="in_proj_sample",
    )(x, g, w)


NORM_ROW_SPLIT = 4


def _in_proj_kernel(x_ref, g_ref, w_ref, o_ref, h_ref):
    j = pl.program_id(1)
    rs = x_ref.shape[0] // NORM_ROW_SPLIT

    @pl.when(j == 0)
    def _():
        for r in range(NORM_ROW_SPLIT):
            rows = slice(r * rs, (r + 1) * rs)
            h = _rms(x_ref[rows, :], g_ref[...]).astype(BF16)
            h_ref[rows, :] = h
            o_ref[rows, :] = _dot(h, w_ref[...])

    @pl.when(j > 0)
    def _():
        o_ref[...] = _dot(h_ref[...], w_ref[...])


def in_proj(x, g, wb, layer, tm, tn):
    m, k = x.shape
    n = wb.shape[1]
    return pl.pallas_call(
        _in_proj_kernel,
        grid=(m // tm, n // tn),
        in_specs=[
            pl.BlockSpec((tm, k), lambda i, j: (i, 0)),
            pl.BlockSpec((None, 1, k), lambda i, j: (layer, 0, 0)),
            pl.BlockSpec((k, tn), lambda i, j: (0, j)),
        ],
        out_specs=pl.BlockSpec((tm, tn), lambda i, j: (i, j)),
        out_shape=jax.ShapeDtypeStruct((m, n), F32),
        scratch_shapes=[pltpu.VMEM((tm, k), BF16)],
        compiler_params=_cparams("parallel", "arbitrary"),
        name="in_proj",
    )(x, g, wb)


def _mix_dot(a, o, c, w):
    acc = _dot(a.astype(BF16), w[0:ATTN_WIDTH])
    acc = acc + _dot(o.astype(BF16), w[ATTN_WIDTH:ATTN_WIDTH + HG_WIDTH])
    return acc + _dot(c.astype(BF16), w[ATTN_WIDTH + HG_WIDTH:])


def _out_proj_sample_kernel(x_ref, a_ref, o_ref, c_ref, w_ref, y_ref, wb_ref):
    wb = w_ref[...].astype(BF16)
    wb_ref[...] = wb
    y_ref[...] = x_ref[...] + _mix_dot(a_ref[...], o_ref[...], c_ref[...], wb)


def out_proj_sample(x, a, o, c, w_out, layer, tn):
    m, d = x.shape
    kw = w_out.shape[1]
    full = lambda width: pl.BlockSpec((m, width), lambda j: (0, 0))
    return pl.pallas_call(
        _out_proj_sample_kernel,
        grid=(d // tn,),
        in_specs=[
            pl.BlockSpec((m, tn), lambda j: (0, j)),
            full(ATTN_WIDTH), full(HG_WIDTH), full(CONV_WIDTH),
            pl.BlockSpec((None, kw, tn), lambda j: (layer, 0, j)),
        ],
        out_specs=[pl.BlockSpec((m, tn), lambda j: (0, j)), pl.BlockSpec((kw, tn), lambda j: (0, j))],
        out_shape=[jax.ShapeDtypeStruct((m, d), F32), jax.ShapeDtypeStruct((kw, d), BF16)],
        compiler_params=_cparams("parallel"),
        name="out_proj_sample",
    )(x, a, o, c, w_out)


def _conv_taps(u, prev1, prev2, w_ref):
    return w_ref[0:1, :] * prev2 + w_ref[1:2, :] * prev1 + w_ref[2:3, :] * u


def _out_proj_kernel(tiles_per_seq, x_ref, a_ref, o_ref, cb_ref, cc_ref, ch_ref, cw_ref, w_ref,
                     y_ref, cs_ref, carry_ref):
    i = pl.program_id(0)
    tm = cc_ref.shape[0]

    @pl.when(i % tiles_per_seq == 0)
    def _():
        carry_ref[...] = jnp.zeros_like(carry_ref)

    u = cc_ref[...] * ch_ref[...]
    row = lax.broadcasted_iota(jnp.int32, u.shape, 0)
    last1 = carry_ref[7:8, :]
    last2 = carry_ref[6:7, :]
    um1 = jnp.where(row == 0, last1, pltpu.roll(u, 1, 0))
    um2 = jnp.where(row == 0, last2, jnp.where(row == 1, last1, pltpu.roll(u, 2, 0)))
    c = cb_ref[...] * _conv_taps(u, um1, um2, cw_ref)
    carry_ref[...] = u[tm - 8:tm]

    y_ref[...] = x_ref[...] + _mix_dot(a_ref[...], o_ref[...], c, w_ref)

    @pl.when(i % tiles_per_seq == tiles_per_seq - 1)
    def _():
        cs_ref[0] = carry_ref[8 - (CONV_K - 1):8, :]


def out_proj(x, a, o, proj, conv_w, wb, layer, seq_len, tm):
    m, d = x.shape
    tiles_per_seq = seq_len // tm
    col = lambda blk: pl.BlockSpec((tm, CONV_WIDTH), lambda i: (i, blk))
    return pl.pallas_call(
        functools.partial(_out_proj_kernel, tiles_per_seq),
        grid=(m // tm,),
        in_specs=[
            pl.BlockSpec((tm, d), lambda i: (i, 0)),
            pl.BlockSpec((tm, ATTN_WIDTH), lambda i: (i, 0)),
            pl.BlockSpec((tm, HG_WIDTH), lambda i: (i, 0)),
            col(CB_BLK_512), col(CC_BLK_512), col(CH_BLK_512),
            pl.BlockSpec((None, CONV_K, CONV_WIDTH), lambda i: (layer, 0, 0)),
            pl.BlockSpec(wb.shape, lambda i: (0, 0)),
        ],
        out_specs=[pl.BlockSpec((tm, d), lambda i: (i, 0)),
                   pl.BlockSpec((1, CONV_K - 1, CONV_WIDTH), lambda i: (i // tiles_per_seq, 0, 0))],
        out_shape=[jax.ShapeDtypeStruct((m, d), F32),
                   jax.ShapeDtypeStruct((m // seq_len, CONV_K - 1, CONV_WIDTH), F32)],
        scratch_shapes=[pltpu.VMEM((8, CONV_WIDTH), F32)],
        compiler_params=_cparams("arbitrary"),
        name="out_proj",
    )(x, a, o, proj, proj, proj, conv_w, wb)


def _ffn_step(h_ref, wg, wu, wd, y_ref):
    h = h_ref[...]
    gate = _dot(h, wg)
    up = _dot(h, wu)
    act = _silu(gate) * up
    y_ref[...] += _dot(act.astype(BF16), wd)


def _ffn_begin(x_ref, g_ref, y_ref, h_ref):
    x = x_ref[...]
    h_ref[...] = _rms(x, g_ref[...]).astype(BF16)
    y_ref[...] = x


def _ffn_sample_kernel(last_layer, x_ref, g_ref, wg_ref, wu_ref, wd_ref, gf_ref,
                       y_ref, wgb_ref, wub_ref, wdb_ref, h_ref):
    f = pl.program_id(0)

    @pl.when(f == 0)
    def _():
        _ffn_begin(x_ref, g_ref, y_ref, h_ref)

    wg = wg_ref[...].astype(BF16)
    wu = wu_ref[...].astype(BF16)
    wd = wd_ref[...].astype(BF16)
    wgb_ref[...] = wg
    wub_ref[...] = wu
    wdb_ref[...] = wd
    _ffn_step(h_ref, wg, wu, wd, y_ref)

    if last_layer:
        @pl.when(f == pl.num_programs(0) - 1)
        def _():
            y_ref[...] = _rms(y_ref[...], gf_ref[...])


def ffn_sample(x, g, w_gate_up, w_down, g_final, layer, tf):
    m, d = x.shape
    nf = D_FF // tf
    return pl.pallas_call(
        functools.partial(_ffn_sample_kernel, layer == DEPTH - 1),
        grid=(nf,),
        in_specs=[
            pl.BlockSpec((m, d), lambda f: (0, 0)),
            pl.BlockSpec((None, 1, d), lambda f: (layer, 0, 0)),
            pl.BlockSpec((None, d, tf), lambda f: (layer, 0, f)),
            pl.BlockSpec((None, d, tf), lambda f: (layer, 0, f + nf)),
            pl.BlockSpec((None, tf, d), lambda f: (layer, f, 0)),
            pl.BlockSpec((1, d), lambda f: (0, 0)),
        ],
        out_specs=[
            pl.BlockSpec((m, d), lambda f: (0, 0)),
            pl.BlockSpec((d, tf), lambda f: (0, f)),
            pl.BlockSpec((d, tf), lambda f: (0, f)),
            pl.BlockSpec((tf, d), lambda f: (f, 0)),
        ],
        out_shape=[jax.ShapeDtypeStruct((m, d), F32), jax.ShapeDtypeStruct((d, D_FF), BF16),
                   jax.ShapeDtypeStruct((d, D_FF), BF16), jax.ShapeDtypeStruct((D_FF, d), BF16)],
        scratch_shapes=[pltpu.VMEM((m, d), BF16)],
        compiler_params=_cparams("arbitrary"),
        name="ffn_sample",
    )(x, g, w_gate_up, w_gate_up, w_down, g_final)


def _ffn_kernel(last_layer, x_ref, g_ref, wg_ref, wu_ref, wd_ref, gf_ref, y_ref, h_ref):
    f = pl.program_id(1)
    rs = x_ref.shape[0] // NORM_ROW_SPLIT

    @pl.when(f == 0)
    def _():
        for r in range(NORM_ROW_SPLIT):
            rows = slice(r * rs, (r + 1) * rs)
            x = x_ref[rows, :]
            h = _rms(x, g_ref[...]).astype(BF16)
            h_ref[rows, :] = h
            act = _silu(_dot(h, wg_ref[...])) * _dot(h, wu_ref[...])
            y_ref[rows, :] = x + _dot(act.astype(BF16), wd_ref[...])

    @pl.when(f > 0)
    def _():
        _ffn_step(h_ref, wg_ref[...], wu_ref[...], wd_ref[...], y_ref)

    if last_layer:
        @pl.when(f == pl.num_programs(1) - 1)
        def _():
            y_ref[...] = _rms(y_ref[...], gf_ref[...])


def ffn(x, g, wgb, wub, wdb, g_final, layer, tm, tf):
    m, d = x.shape
    return pl.pallas_call(
        functools.partial(_ffn_kernel, layer == DEPTH - 1),
        grid=(m // tm, D_FF // tf),
        in_specs=[
            pl.BlockSpec((tm, d), lambda i, f: (i, 0)),
            pl.BlockSpec((None, 1, d), lambda i, f: (layer, 0, 0)),
            pl.BlockSpec((d, tf), lambda i, f: (0, f)),
            pl.BlockSpec((d, tf), lambda i, f: (0, f)),
            pl.BlockSpec((tf, d), lambda i, f: (f, 0)),
            pl.BlockSpec((1, d), lambda i, f: (0, 0)),
        ],
        out_specs=pl.BlockSpec((tm, d), lambda i, f: (i, 0)),
        out_shape=jax.ShapeDtypeStruct((m, d), F32),
        scratch_shapes=[pltpu.VMEM((tm, d), BF16)],
        compiler_params=_cparams("parallel", "arbitrary"),
        name="ffn",
    )(x, g, wgb, wub, wdb, g_final)


def _rope_tables(pos):
    half = ROT_DIM // 2
    inv = ROPE_THETA ** (-jnp.arange(half, dtype=F32) * 2.0 / ROT_DIM)
    ang = pos.astype(F32)[:, None] * inv[None, :]
    cos, sin = jnp.cos(ang), jnp.sin(ang)
    n = pos.shape[0]
    ones = jnp.ones((n, HEAD_DIM - ROT_DIM), F32)
    zeros = jnp.zeros((n, HEAD_DIM - ROT_DIM), F32)
    zh = jnp.zeros((n, half), F32)
    c = jnp.concatenate([cos, cos, ones], axis=-1)
    s_lo = jnp.concatenate([-sin, zh, zeros], axis=-1)
    s_hi = jnp.concatenate([zh, sin, zeros], axis=-1)
    rep = lambda t: jnp.concatenate([t, t], axis=-1)
    return rep(c), rep(s_lo), rep(s_hi)


def _rope(x, c, s_lo, s_hi):
    half = ROT_DIM // 2
    outs = []
    for j in range(x.shape[1] // 128):
        xc = x[:, j * 128:(j + 1) * 128]
        outs.append(xc * c + pltpu.roll(xc, 128 - half, 1) * s_lo + pltpu.roll(xc, half, 1) * s_hi)
    return outs[0] if len(outs) == 1 else jnp.concatenate(outs, axis=1)


def _rope_tables_t(pos):
    half = ROT_DIM // 2
    inv = ROPE_THETA ** (-jnp.arange(half, dtype=F32) * 2.0 / ROT_DIM)
    ang = inv[:, None] * pos.astype(F32)[None, :]
    return jnp.cos(ang), jnp.sin(ang)


def _rope_t(xt, cos_t, sin_t):
    half = ROT_DIM // 2
    pieces = []
    for base in range(0, xt.shape[0], HEAD_DIM):
        x1 = xt[base:base + half]
        x2 = xt[base + half:base + ROT_DIM]
        pieces += [x1 * cos_t - x2 * sin_t, x2 * cos_t + x1 * sin_t, xt[base + ROT_DIM:base + HEAD_DIM]]
    return jnp.concatenate(pieces, axis=0)


def _softmax_sink_pv(s, sk, v):
    m = jnp.maximum(jnp.max(s, axis=-1, keepdims=True), sk)
    p = jnp.exp(s - m)
    denom = jnp.sum(p, axis=-1, keepdims=True) + jnp.exp(sk - m)
    return _dot(p.astype(BF16), v) / denom


def _attn_prompt_kernel(layer, sink_ref, q_ref, k_ref, v_ref, c_ref, slo_ref, shi_ref, ct_ref, st_ref,
                        a_ref, ko_ref, vo_ref, kprev_ref, vtprev_ref):
    n = pl.program_id(1)
    nstep = pl.num_programs(1)
    blk = WINDOW
    nsub = q_ref.shape[1] // blk
    lanes = Q_PER_KV * blk
    sub = lambda j: slice(j * blk, (j + 1) * blk)

    k_all = _rope(k_ref[0], c_ref[...], slo_ref[...], shi_ref[...])
    v_all = v_ref[0]
    vt_all = v_all.T
    qt_all = (_rope_t(q_ref[0].T, ct_ref[...], st_ref[...]) * ATTN_SCALE).astype(BF16)

    key = lax.broadcasted_iota(jnp.int32, (blk, lanes), 0)
    qry = lax.broadcasted_iota(jnp.int32, (blk, lanes), 1) % blk
    newer = key > qry
    zero_rows = jnp.zeros((HEAD_DIM, lanes), BF16)

    def scores(h, j, first):
        pair = slice((h // 2) * 2 * HEAD_DIM, (h // 2 + 1) * 2 * HEAD_DIM)
        qt = qt_all[:, sub(j)]
        qh = jnp.concatenate(
            [qt[(Q_PER_KV * h + g) * HEAD_DIM:(Q_PER_KV * h + g + 1) * HEAD_DIM] for g in range(Q_PER_KV)], axis=1)
        qz = jnp.concatenate([qh, zero_rows] if h % 2 == 0 else [zero_rows, qh], axis=0)
        s_cur = _dot(k_all[sub(j), pair].astype(BF16), qz)
        if first:
            return None, s_cur
        k_prev = kprev_ref[:, pair] if j == 0 else k_all[sub(j - 1), pair]
        return _dot(k_prev.astype(BF16), qz), s_cur

    def softmax(h, s_prev, s_cur):
        sink = jnp.concatenate(
            [jnp.full((1, blk), sink_ref[layer, Q_PER_KV * h + g], F32) for g in range(Q_PER_KV)], axis=1)
        if s_prev is None:
            f = jnp.where(newer, NEG_BIG, s_cur)
            m = jnp.maximum(jnp.max(f, axis=0, keepdims=True), sink)
            p = jnp.exp(f - m)
            denom = jnp.sum(p, axis=0, keepdims=True) + jnp.exp(sink - m)
            return p.astype(BF16), None, denom
        f = jnp.where(newer, s_prev, s_cur)
        d = jnp.sum(jnp.where(key == qry, s_prev, 0.0), axis=0, keepdims=True)
        m = jnp.maximum(jnp.maximum(jnp.max(f, axis=0, keepdims=True), d), sink)
        p = jnp.exp(f - m)
        pd = jnp.exp(d - m)
        denom = jnp.sum(p, axis=0, keepdims=True) + pd + jnp.exp(sink - m)
        pp = jnp.concatenate([jnp.where(newer, p, 0.0), jnp.where(newer, 0.0, p)], axis=0).astype(BF16)
        return pp, pd, denom

    def weighted_values(h, j, pp, pd, denom):
        rows = slice(h * HEAD_DIM, (h + 1) * HEAD_DIM)
        vt_cur = vt_all[rows, sub(j)]
        inv = 1.0 / denom
        if pd is None:
            return _dot(vt_cur.astype(BF16), pp) * inv
        vt_prev = vtprev_ref[rows, :] if j == 0 else vt_all[rows, sub(j - 1)]
        o = _dot(jnp.concatenate([vt_prev, vt_cur], axis=1).astype(BF16), pp)
        return (o + pd * jnp.concatenate([vt_prev] * Q_PER_KV, axis=1)) * inv

    def run(first_step):
        units = [(h, j) for j in range(nsub) for h in range(N_KV_HEADS)]
        s = {u: scores(*u, first_step and u[1] == 0) for u in units}
        p = {u: softmax(u[0], *s[u]) for u in units}
        o = {u: weighted_values(*u, *p[u]) for u in units}
        for j in range(nsub):
            outs = [o[h, j][:, g * blk:(g + 1) * blk] for h in range(N_KV_HEADS) for g in range(Q_PER_KV)]
            a_ref[0, sub(j), :] = jnp.concatenate(outs, axis=0).T.astype(a_ref.dtype)

    @pl.when(n == 0)
    def _():
        run(True)

    @pl.when(n > 0)
    def _():
        run(False)

    kprev_ref[...] = k_all[sub(nsub - 1)]
    vtprev_ref[...] = vt_all[:, sub(nsub - 1)]

    @pl.when(n == nstep - 1)
    def _():
        ko_ref[0] = k_all[sub(nsub - 1)]
        vo_ref[0] = v_all[sub(nsub - 1)]


def attn_prompt(proj, sink, tables, tables_t, layer, rows):
    b, l, _ = proj.shape
    blk = WINDOW
    tab_spec = pl.BlockSpec((rows, 128), lambda i, n: (n, 0))
    tab_t_spec = pl.BlockSpec((ROT_DIM // 2, rows), lambda i, n: (0, n))
    kv_out = jax.ShapeDtypeStruct((b, blk, KV_WIDTH), F32)
    return pl.pallas_call(
        functools.partial(_attn_prompt_kernel, layer),
        grid=(b, l // rows),
        in_specs=[
            pl.BlockSpec(memory_space=pltpu.SMEM),
            pl.BlockSpec((1, rows, ATTN_WIDTH), lambda i, n: (i, n, Q_BLK_1024)),
            pl.BlockSpec((1, rows, KV_WIDTH), lambda i, n: (i, n, K_BLK_256)),
            pl.BlockSpec((1, rows, KV_WIDTH), lambda i, n: (i, n, V_BLK_256)),
            tab_spec, tab_spec, tab_spec, tab_t_spec, tab_t_spec,
        ],
        out_specs=[
            pl.BlockSpec((1, rows, ATTN_WIDTH), lambda i, n: (i, n, 0)),
            pl.BlockSpec((1, blk, KV_WIDTH), lambda i, n: (i, 0, 0)),
            pl.BlockSpec((1, blk, KV_WIDTH), lambda i, n: (i, 0, 0)),
        ],
        out_shape=[jax.ShapeDtypeStruct((b, l, ATTN_WIDTH), BF16), kv_out, kv_out],
        scratch_shapes=[pltpu.VMEM((blk, KV_WIDTH), F32), pltpu.VMEM((KV_WIDTH, blk), F32)],
        compiler_params=_cparams("parallel", "arbitrary"),
        name="attn_prompt",
    )(sink, proj, proj, proj, *tables, *tables_t)


def _attn_sample_kernel(layer, sink_ref, q_ref, k_ref, v_ref, kc_ref, vc_ref, c_ref, slo_ref, shi_ref,
                        k_carried_ref, v_carried_ref, a_ref, ko_ref, vo_ref):
    del k_carried_ref, v_carried_ref
    r8 = SAMPLE_ROWS
    nseq = r8 // SAMPLE_T
    lb = kc_ref.shape[1]
    c, s_lo, s_hi = c_ref[...], slo_ref[...], shi_ref[...]
    q = _rope(q_ref[...], c, s_lo, s_hi)
    k = _rope(k_ref[...], c, s_lo, s_hi)
    v = v_ref[...]
    pad = jnp.zeros((lb - r8, KV_WIDTH), F32)
    k_pad = jnp.concatenate([k, pad], axis=0)
    v_pad = jnp.concatenate([v, pad], axis=0)

    zeros = jnp.zeros((r8, HEAD_DIM), F32)
    qx = jnp.concatenate(
        [jnp.concatenate([q[:, hg * HEAD_DIM:(hg + 1) * HEAD_DIM] if slot == hg // Q_PER_KV else zeros
                          for slot in range(N_KV_HEADS)], axis=1) for hg in range(N_Q_HEADS)],
        axis=0).astype(BF16)
    sk = jnp.concatenate([jnp.full((r8, 1), sink_ref[layer, hg], F32) for hg in range(N_Q_HEADS)], axis=0)

    rows = N_Q_HEADS * r8
    ri = lax.broadcasted_iota(jnp.int32, (rows, 2 * lb), 0) % r8
    t = ri % SAMPLE_T
    kj = lax.broadcasted_iota(jnp.int32, (rows, 2 * lb), 1)
    cj = kj - lb
    row_seq = lax.broadcasted_iota(jnp.int32, (rows, KV_WIDTH), 0) % r8 // SAMPLE_T

    s = []
    for e in range(nseq):
        kk = jnp.concatenate([kc_ref[e], k_pad], axis=0).astype(BF16)
        s.append(_dot_nt(qx, kk) * ATTN_SCALE)
    o = None
    for e in range(nseq):
        mask = ((kj < lb) & (kj >= t)) | (
            (cj >= e * SAMPLE_T) & (cj < (e + 1) * SAMPLE_T) & (cj - e * SAMPLE_T <= t))
        vv = jnp.concatenate([vc_ref[e], v_pad], axis=0).astype(BF16)
        oe = _softmax_sink_pv(jnp.where(mask, s[e], NEG_BIG), sk, vv)
        o = oe if o is None else jnp.where(row_seq == e, oe, o)
    a_ref[...] = jnp.concatenate(
        [o[hg * r8:(hg + 1) * r8, (hg // Q_PER_KV) * HEAD_DIM:(hg // Q_PER_KV + 1) * HEAD_DIM]
         for hg in range(N_Q_HEADS)], axis=1)

    row = lax.broadcasted_iota(jnp.int32, (lb, KV_WIDTH), 0)
    for e in range(nseq):
        shift = (lb - SAMPLE_T - e * SAMPLE_T) % lb
        for new_pad, cache_ref, out_ref in ((k_pad, kc_ref, ko_ref), (v_pad, vc_ref, vo_ref)):
            new_rows = new_pad if shift == 0 else pltpu.roll(new_pad, shift, 0)
            old_rows = pltpu.roll(cache_ref[e], lb - SAMPLE_T, 0)
            out_ref[e] = jnp.where(row >= lb - SAMPLE_T, new_rows, old_rows)


def _lower_bound(lbr_ref, layer, hs):
    rows = [lbr_ref[r:r + 1, hs] for r in range(DEPTH)]
    mx = functools.reduce(jnp.maximum, rows)
    es = [jnp.exp(r - mx) for r in rows]
    tot = functools.reduce(lambda a, b: a + b, es)
    lbp = [e / tot for e in es]
    acc = lbp[0]
    for r in range(1, layer + 1):
        acc = acc + lbp[r]
    return acc - lbp[0]


def _hgrn_gates(hq, hf, lb):
    q = _silu(hq) * (HG_DK ** -0.5)
    e = jnp.exp(-jnp.abs(hf))
    r = 1.0 / (1.0 + e)
    pos = hf >= 0.0
    sig_pos = jnp.where(pos, r, e * r)
    sig_neg = jnp.where(pos, e * r, r)
    log_f = jnp.log(jnp.maximum(lb, LB_TINY) + (1.0 - lb) * sig_pos)
    k = (1.0 - lb) * sig_neg
    return q, log_f, k


def _hgrn_finish(o, g_norm, hg):
    return _rms(o, g_norm) * _silu(hg)


def _hgrn_levels():
    s = HG_CHUNK // 2
    while s >= 1:
        yield s
        s //= 2


def _hgrn_level_masks():
    t = np.arange(HG_CHUNK)[:, None]
    s = np.arange(HG_CHUNK)[None, :]
    return np.stack([((t % (2 * h) >= h) & (t // (2 * h) == s // (2 * h)) & (s % (2 * h) < h))
                     for h in _hgrn_levels()]).astype(np.float32)


def _hgrn_pair_reference(g_ref, gcum, half, sub8):
    c = HG_CHUNK
    pair = 2 * half
    bcast = lambda r: jnp.broadcast_to(g_ref[r:r + 1, :], (8, HG_DK))
    if pair >= 8:
        return jnp.concatenate([bcast((8 * v // pair) * pair + half - 1) for v in range(c // 8)], axis=0)
    if pair == 4:
        lo = jnp.concatenate([bcast(8 * v + 1) for v in range(c // 8)], axis=0)
        hi = jnp.concatenate([bcast(8 * v + 5) for v in range(c // 8)], axis=0)
        return jnp.where(sub8 < 4, lo, hi)
    return jnp.where(sub8 % 2 == 1, pltpu.roll(gcum, 1, 0), gcum)


def _hgrn_level_operand(q, k, gcum, g_ref, half, row):
    ref = _hgrn_pair_reference(g_ref, gcum, half, row % 8)
    right = row % (2 * half) >= half
    return (jnp.where(right, q, k) * jnp.exp(-jnp.abs(gcum - ref))).astype(BF16)


def _chunk_cumsum(x, buf, row):
    sh = 1
    while sh < HG_CHUNK:
        x = x + jnp.where(row >= sh, pltpu.roll(x, sh, 0), 0.0)
        sh *= 2
    buf[...] = x
    return x


def _hgrn_tile(layer, hq_ref, hf_ref, hi_ref, hg_ref, lbr_ref, gn_ref, m_ref, g_ref, state, emit):
    c = HG_CHUNK
    nchunk = hq_ref.shape[1] // c
    heads = range(HG_HEADS)
    units = [(h, cc) for cc in range(nchunk) for h in heads]
    row = lax.broadcasted_iota(jnp.int32, (c, HG_DK), 0)
    cols = lambda h: slice(h * HG_DK, (h + 1) * HG_DK)
    rws = lambda cc: slice(cc * c, (cc + 1) * c)
    gbuf = lambda h, cc: g_ref.at[h * nchunk + cc]
    lbs = [_lower_bound(lbr_ref, layer, cols(h)) for h in heads]

    q, k, v, gcum = {}, {}, {}, {}
    for u in units:
        h, cc = u
        q[u], log_f, k[u] = _hgrn_gates(hq_ref[0, rws(cc), cols(h)], hf_ref[0, rws(cc), cols(h)], lbs[h])
        v[u] = hi_ref[0, rws(cc), cols(h)]
        gcum[u] = _chunk_cumsum(log_f, gbuf(h, cc), row)

    a = {u: jnp.zeros((c, c), F32) for u in units}
    for lvl, half in enumerate(_hgrn_levels()):
        for u in units:
            x = _hgrn_level_operand(q[u], k[u], gcum[u], gbuf(*u), half, row)
            a[u] = a[u] + _dot_nt(x, x) * m_ref[lvl]

    o_intra, qs, ks, decay = {}, {}, {}, {}
    for u in units:
        vb = v[u].astype(BF16)
        o_intra[u] = _dot(a[u].astype(BF16), vb) + jnp.sum(q[u] * k[u], axis=-1, keepdims=True) * v[u]
        g_last = gbuf(*u)[c - 1:c, :]
        qs[u] = (q[u] * jnp.exp(gcum[u])).astype(BF16)
        ks[u] = (k[u] * jnp.exp(g_last - gcum[u])).astype(BF16)
        decay[u] = jnp.transpose(jnp.broadcast_to(jnp.exp(g_last), (HG_DK, HG_DK)))

    state = list(state)
    for cc in range(nchunk):
        o_inter = [_dot(qs[h, cc], state[h].astype(BF16)) for h in heads]
        update = [_dot_tn(ks[h, cc], v[h, cc].astype(BF16)) for h in heads]
        for h in heads:
            o = o_inter[h] + o_intra[h, cc]
            emit(h, rws(cc), _hgrn_finish(o, gn_ref[h:h + 1, :], hg_ref[0, rws(cc), cols(h)]))
            state[h] = state[h] * decay[h, cc] + update[h]
    return state


def _hgrn_prompt_kernel(layer, hq_ref, hf_ref, hi_ref, hg_ref, lbr_ref, gn_ref, m_ref,
                        o_ref, so_ref, s_ref, g_ref):
    step = pl.program_id(1)
    heads = range(HG_HEADS)

    @pl.when(step == 0)
    def _():
        s_ref[...] = jnp.zeros_like(s_ref)

    def emit(h, rs, o):
        o_ref[0, rs, h * HG_DV:(h + 1) * HG_DV] = o.astype(o_ref.dtype)

    state = _hgrn_tile(layer, hq_ref, hf_ref, hi_ref, hg_ref, lbr_ref, gn_ref, m_ref, g_ref,
                       [s_ref[h] for h in heads], emit)
    for h in heads:
        s_ref[h] = state[h]

    @pl.when(step == pl.num_programs(1) - 1)
    def _():
        so_ref[0] = s_ref[...]


def hgrn_prompt(proj, lb_raw, g_norm, layer, rows):
    b, l, _ = proj.shape
    c = HG_CHUNK
    masks = jnp.asarray(_hgrn_level_masks())
    col = lambda blk: pl.BlockSpec((1, rows, HG_WIDTH), lambda i, n: (i, n, blk))
    return pl.pallas_call(
        functools.partial(_hgrn_prompt_kernel, layer),
        grid=(b, l // rows),
        in_specs=[
            col(HQ_BLK_512), col(HF_BLK_512), col(HI_BLK_512), col(HG_BLK_512),
            pl.BlockSpec((DEPTH, HG_WIDTH), lambda i, n: (0, 0)),
            pl.BlockSpec((None, HG_HEADS, HG_DV), lambda i, n: (layer, 0, 0)),
            pl.BlockSpec(masks.shape, lambda i, n: (0, 0, 0)),
        ],
        out_specs=[
            pl.BlockSpec((1, rows, HG_WIDTH), lambda i, n: (i, n, 0)),
            pl.BlockSpec((1, HG_HEADS, HG_DK, HG_DV), lambda i, n: (i, 0, 0, 0)),
        ],
        out_shape=[jax.ShapeDtypeStruct((b, l, HG_WIDTH), BF16),
                   jax.ShapeDtypeStruct((b, HG_HEADS, HG_DK, HG_DV), F32)],
        scratch_shapes=[pltpu.VMEM((HG_HEADS, HG_DK, HG_DV), F32),
                        pltpu.VMEM((HG_HEADS * (rows // c), c, HG_DK), F32)],
        compiler_params=_cparams("parallel", "arbitrary"),
        name="hgrn_prompt",
    )(proj, proj, proj, proj, lb_raw, g_norm, masks)


def _hgrn_sample_kernel(layer, hq_ref, hf_ref, hi_ref, hg_ref, lbr_ref, gn_ref, s_ref, carried_ref,
                        o_ref, so_ref, q_ref, k_ref, g_ref):
    del carried_ref
    r8, tt = SAMPLE_ROWS, SAMPLE_T
    nseq = r8 // tt
    row = lax.broadcasted_iota(jnp.int32, (r8, HG_DK), 0)
    t_idx = row % tt
    seq = row // tt

    def pick(ref, s):
        out = ref[s:s + 1, :]
        for e in range(1, nseq):
            out = jnp.where(seq == e, ref[e * tt + s:e * tt + s + 1, :], out)
        return out

    for h in range(HG_HEADS):
        hs = slice(h * HG_DK, (h + 1) * HG_DK)
        lb = _lower_bound(lbr_ref, layer, hs)
        q, log_f, k = _hgrn_gates(hq_ref[:, hs], hf_ref[:, hs], lb)
        gcum = log_f
        sh = 1
        while sh < tt:
            gcum = gcum + jnp.where(t_idx >= sh, pltpu.roll(gcum, sh, 0), 0.0)
            sh *= 2
        q_ref[...] = q
        k_ref[...] = k
        g_ref[...] = gcum
        v = hi_ref[:, hs]
        g_last = pick(g_ref, tt - 1)
        qs = (q * jnp.exp(gcum)).astype(BF16)
        ks_all = k * jnp.exp(g_last - gcum)

        o = jnp.zeros((r8, HG_DV), F32)
        for e in range(nseq):
            state = s_ref[e, h]
            o = jnp.where(seq == e, _dot(qs, state.astype(BF16)), o)
            g_last_e = g_ref[e * tt + tt - 1:e * tt + tt, :]
            decay = jnp.transpose(jnp.broadcast_to(jnp.exp(g_last_e), (HG_DK, HG_DK)))
            ks_e = jnp.where(seq == e, ks_all, 0.0).astype(BF16)
            so_ref[e, h] = state * decay + _dot_tn(ks_e, v.astype(BF16))
        for s in range(tt):
            dec = jnp.exp(jnp.where(t_idx >= s, gcum - pick(g_ref, s), NEG_BIG))
            col = jnp.sum(q * (dec * pick(k_ref, s)), axis=-1, keepdims=True)
            o = o + col * pick(hi_ref.at[:, hs], s)
        o_ref[:, hs] = _hgrn_finish(o, gn_ref[h:h + 1, :], hg_ref[:, hs])


def _conv_sample_kernel(cb_ref, cc_ref, ch_ref, w_ref, st_ref, y_ref, so_ref, u_ref):
    r8, tt = SAMPLE_ROWS, SAMPLE_T
    nseq = r8 // tt
    u = cc_ref[...] * ch_ref[...]
    u_ref[...] = u
    row = lax.broadcasted_iota(jnp.int32, u.shape, 0)
    t_idx = row % tt
    seq = row // tt
    last1 = st_ref[0, 1:2, :]
    last2 = st_ref[0, 0:1, :]
    for e in range(1, nseq):
        last1 = jnp.where(seq == e, st_ref[e, 1:2, :], last1)
        last2 = jnp.where(seq == e, st_ref[e, 0:1, :], last2)
    um1 = jnp.where(t_idx == 0, last1, pltpu.roll(u, 1, 0))
    um2 = jnp.where(t_idx == 0, last2, jnp.where(t_idx == 1, last1, pltpu.roll(u, 2, 0)))
    y_ref[...] = cb_ref[...] * _conv_taps(u, um1, um2, w_ref)
    for e in range(nseq):
        so_ref[e] = u_ref[(e + 1) * tt - (CONV_K - 1):(e + 1) * tt, :]


N_ATTN_SAMPLE_IN, N_HGRN_SAMPLE_IN, N_CONV_SAMPLE_IN = 11, 8, 5
N_ATTN_SAMPLE_OUT, N_HGRN_SAMPLE_OUT, N_CONV_SAMPLE_OUT = 3, 2, 2


def _mix_sample_kernel(layer, *refs):
    it = iter(refs)
    take = lambda n: [next(it) for _ in range(n)]
    attn_in, hgrn_in, conv_in = take(N_ATTN_SAMPLE_IN), take(N_HGRN_SAMPLE_IN), take(N_CONV_SAMPLE_IN)
    attn_out, hgrn_out, conv_out = take(N_ATTN_SAMPLE_OUT), take(N_HGRN_SAMPLE_OUT), take(N_CONV_SAMPLE_OUT)
    hgrn_scratch, conv_scratch = take(3), take(1)
    _attn_sample_kernel(layer, *attn_in, *attn_out)
    _hgrn_sample_kernel(layer, *hgrn_in, *hgrn_out, *hgrn_scratch)
    _conv_sample_kernel(*conv_in, *conv_out, *conv_scratch)


def mix_sample(proj, k_cache, v_cache, k_new, v_new, sink, tables, state, state_new, lb_raw, g_norm,
               conv_state, conv_w, layer):
    m = proj.shape[0]
    r8 = SAMPLE_ROWS
    nseq = r8 // SAMPLE_T
    lb = k_cache.shape[2]
    rows = lambda width, blk: pl.BlockSpec((r8, width), lambda i: (i, blk))
    tab_spec = pl.BlockSpec((r8, 128), lambda i: (0, 0))
    cache_spec = pl.BlockSpec((None, nseq, lb, KV_WIDTH), lambda i: (layer, i, 0, 0))
    st_spec = pl.BlockSpec((None, nseq, HG_HEADS, HG_DK, HG_DV), lambda i: (layer, i, 0, 0, 0))
    carried = pl.BlockSpec(memory_space=pl.ANY)
    attn_in = [pl.BlockSpec(memory_space=pltpu.SMEM),
               rows(ATTN_WIDTH, Q_BLK_1024), rows(KV_WIDTH, K_BLK_256), rows(KV_WIDTH, V_BLK_256),
               cache_spec, cache_spec, tab_spec, tab_spec, tab_spec, carried, carried]
    hgrn_in = [rows(HG_WIDTH, HQ_BLK_512), rows(HG_WIDTH, HF_BLK_512), rows(HG_WIDTH, HI_BLK_512),
               rows(HG_WIDTH, HG_BLK_512),
               pl.BlockSpec((DEPTH, HG_WIDTH), lambda i: (0, 0)),
               pl.BlockSpec((None, HG_HEADS, HG_DV), lambda i: (layer, 0, 0)),
               st_spec, carried]
    conv_in = [rows(CONV_WIDTH, CB_BLK_512), rows(CONV_WIDTH, CC_BLK_512), rows(CONV_WIDTH, CH_BLK_512),
               pl.BlockSpec((None, CONV_K, CONV_WIDTH), lambda i: (layer, 0, 0)),
               pl.BlockSpec((None, nseq, CONV_K - 1, CONV_WIDTH), lambda i: (layer, i, 0, 0))]
    assert (len(attn_in), len(hgrn_in), len(conv_in)) == (N_ATTN_SAMPLE_IN, N_HGRN_SAMPLE_IN, N_CONV_SAMPLE_IN)
    out_specs = [rows(ATTN_WIDTH, 0), cache_spec, cache_spec,
                 rows(HG_WIDTH, 0), st_spec,
                 rows(CONV_WIDTH, 0), pl.BlockSpec((nseq, CONV_K - 1, CONV_WIDTH), lambda i: (i, 0, 0))]
    out_shape = [jax.ShapeDtypeStruct((m, ATTN_WIDTH), F32),
                 jax.ShapeDtypeStruct(k_new.shape, F32), jax.ShapeDtypeStruct(v_new.shape, F32),
                 jax.ShapeDtypeStruct((m, HG_WIDTH), F32), jax.ShapeDtypeStruct(state_new.shape, F32),
                 jax.ShapeDtypeStruct((m, CONV_WIDTH), F32), jax.ShapeDtypeStruct(conv_state.shape[1:], F32)]
    return pl.pallas_call(
        functools.partial(_mix_sample_kernel, layer),
        grid=(m // r8,),
        in_specs=attn_in + hgrn_in + conv_in,
        out_specs=out_specs,
        out_shape=out_shape,
        input_output_aliases={9: 1, 10: 2, N_ATTN_SAMPLE_IN + N_HGRN_SAMPLE_IN - 1: 4},
        scratch_shapes=[pltpu.VMEM((r8, HG_DK), F32), pltpu.VMEM((r8, HG_DK), F32), pltpu.VMEM((r8, HG_DK), F32),
                        pltpu.VMEM((r8, CONV_WIDTH), F32)],
        compiler_params=_cparams("parallel"),
        name="mix_sample",
    )(sink, proj, proj, proj, k_cache, v_cache, *tables, k_new, v_new,
      proj, proj, proj, proj, lb_raw, g_norm, state, state_new,
      proj, proj, proj, conv_w, conv_state)


def kernel(x_prompt, x_sample, cache_attn_k, cache_attn_v, state_hgrn, state_conv, w_in, attn_sink,
           hgrn_lower_bounds, hgrn_norm, conv_w, w_out, norm_mix, norm_ffn, w_gate_up, w_down, norm_final):
    bp, lp, d = x_prompt.shape
    bs, ls, _ = x_sample.shape
    assert ls == SAMPLE_T and (bs * ls) % SAMPLE_ROWS == 0
    lb = cache_attn_k.shape[2]
    mp, ms = bp * lp, bs * ls

    pos_p = jnp.arange(lp, dtype=jnp.int32)
    tab_p = _rope_tables(pos_p)
    tab_pt = _rope_tables_t(pos_p)
    pos_s = PAST_LEN + jnp.arange(SAMPLE_ROWS, dtype=jnp.int32) % SAMPLE_T
    tab_s = _rope_tables(pos_s)

    xp = x_prompt.reshape(mp, d)
    xs = x_sample.reshape(ms, d)
    kc = cache_attn_k.reshape(DEPTH, bs, lb, KV_WIDTH)
    vc = cache_attn_v.reshape(DEPTH, bs, lb, KV_WIDTH)

    g_mix = norm_mix.reshape(DEPTH, 1, d)
    g_ffn = norm_ffn.reshape(DEPTH, 1, d)
    g_final = norm_final.reshape(1, d)

    k_s = jnp.zeros(kc.shape, F32)
    v_s = jnp.zeros(vc.shape, F32)
    s_s = jnp.zeros(state_hgrn.shape, F32)

    outs = {name: [] for name in ("kp", "vp", "sp", "cp", "cs")}
    for l in range(DEPTH):
        proj_s, w_in_b = in_proj_sample(xs, g_mix, w_in, l, 1024)
        a_s, k_s, v_s, o_s, s_s, c_s, cst_s = mix_sample(
            proj_s, kc, vc, k_s, v_s, attn_sink, tab_s, state_hgrn, s_s, hgrn_lower_bounds, hgrn_norm,
            state_conv, conv_w, l)
        xs, w_out_b = out_proj_sample(xs, a_s, o_s, c_s, w_out, l, 512)
        xs, w_gate_b, w_up_b, w_down_b = ffn_sample(xs, g_ffn, w_gate_up, w_down, g_final, l, 512)

        proj_p = in_proj(xp, g_mix, w_in_b, l, 1024, 1024)
        proj_p3 = proj_p.reshape(bp, lp, IN_WIDTH)
        a_p, k_p, v_p = attn_prompt(proj_p3, attn_sink, tab_p, tab_pt, l, 4 * WINDOW)
        o_p, s_p = hgrn_prompt(proj_p3, hgrn_lower_bounds, hgrn_norm, l, 8 * HG_CHUNK)
        xp, cst_p = out_proj(xp, a_p.reshape(mp, ATTN_WIDTH), o_p.reshape(mp, HG_WIDTH), proj_p,
                             conv_w, w_out_b, l, lp, 512)
        xp = ffn(xp, g_ffn, w_gate_b, w_up_b, w_down_b, g_final, l, 1024, 512)

        outs["kp"].append(k_p.reshape(bp, WINDOW, N_KV_HEADS, HEAD_DIM))
        outs["vp"].append(v_p.reshape(bp, WINDOW, N_KV_HEADS, HEAD_DIM))
        outs["sp"].append(s_p)
        outs["cp"].append(cst_p)
        outs["cs"].append(cst_s)

    st = lambda name: jnp.stack(outs[name])
    kv_shape = (DEPTH, bs, lb, N_KV_HEADS, HEAD_DIM)
    return (xp.reshape(bp, lp, d), xs.reshape(bs, ls, d), st("kp"), st("vp"), st("sp"), st("cp"),
            k_s.reshape(kv_shape), v_s.reshape(kv_shape), s_s, st("cs"))
```

```python
import functools

import jax
import jax.numpy as jnp
import numpy as np
from jax import lax
from jax.experimental import pallas as pl
from jax.experimental.pallas import tpu as pltpu

F32 = jnp.float32
BF16 = jnp.bfloat16

D_MODEL = 2048
DEPTH = 4
PAST_LEN = 16384
HEAD_DIM = 64
N_Q_HEADS = 16
N_KV_HEADS = 4
Q_PER_KV = 4
ATTN_WIDTH = 1024
KV_WIDTH = 256
WINDOW = 128
ROPE_THETA = 500000.0
ROT_DIM = 16
ATTN_SCALE = HEAD_DIM ** -0.5
NEG_BIG = -1e30
HG_HEADS = 4
HG_DK = 128
HG_DV = 128
HG_WIDTH = 512
HG_CHUNK = 64
LB_TINY = 1e-30
CONV_K = 3
CONV_WIDTH = 512
D_FF = 5632
IN_WIDTH = 5120
NORM_EPS = 1e-6
SAMPLE_T = 4
SAMPLE_ROWS = 8

Q_BLK_1024 = 0
K_BLK_256 = 4
V_BLK_256 = 5
HQ_BLK_512, HF_BLK_512, HI_BLK_512, HG_BLK_512 = 3, 4, 5, 6
CB_BLK_512, CC_BLK_512, CH_BLK_512 = 7, 8, 9

VMEM_LIMIT = 56 * 1024 * 1024


def _cparams(*sem, vmem=VMEM_LIMIT):
    return pltpu.CompilerParams(dimension_semantics=sem, vmem_limit_bytes=vmem)


def _rms(x, g):
    ms = jnp.mean(x * x, axis=-1, keepdims=True)
    return x * lax.rsqrt(ms + NORM_EPS) * g


def _silu(x):
    return x * (1.0 / (1.0 + jnp.exp(-x)))


def _dot(a, b):
    return jnp.dot(a, b, preferred_element_type=F32)


def _dot_nt(a, b):
    return lax.dot_general(a, b, (((1,), (1,)), ((), ())), preferred_element_type=F32)


def _dot_tn(a, b):
    return lax.dot_general(a, b, (((0,), (0,)), ((), ())), preferred_element_type=F32)


def _in_proj_sample_kernel(x_ref, g_ref, w_ref, o_ref, wb_ref, h_ref):
    @pl.when(pl.program_id(0) == 0)
    def _():
        h_ref[...] = _rms(x_ref[...], g_ref[...]).astype(BF16)

    wb = w_ref[...].astype(BF16)
    wb_ref[...] = wb
    o_ref[...] = _dot(h_ref[...], wb)


def in_proj_sample(x, g, w, layer, tn):
    m, k = x.shape
    n = w.shape[2]
    return pl.pallas_call(
        _in_proj_sample_kernel,
        grid=(n // tn,),
        in_specs=[
            pl.BlockSpec((m, k), lambda j: (0, 0)),
            pl.BlockSpec((None, 1, k), lambda j: (layer, 0, 0)),
            pl.BlockSpec((None, k, tn), lambda j: (layer, 0, j)),
        ],
        out_specs=[pl.BlockSpec((m, tn), lambda j: (0, j)), pl.BlockSpec((k, tn), lambda j: (0, j))],
        out_shape=[jax.ShapeDtypeStruct((m, n), F32), jax.ShapeDtypeStruct((k, n), BF16)],
        scratch_shapes=[pltpu.VMEM((m, k), BF16)],
        compiler_params=_cparams("arbitrary"),
        name="in_proj_sample",
    )(x, g, w)


NORM_ROW_SPLIT = 4


def _in_proj_kernel(x_ref, g_ref, w_ref, o_ref, h_ref):
    j = pl.program_id(1)
    rs = x_ref.shape[0] // NORM_ROW_SPLIT

    @pl.when(j == 0)
    def _():
        for r in range(NORM_ROW_SPLIT):
            rows = slice(r * rs, (r + 1) * rs)
            h = _rms(x_ref[rows, :], g_ref[...]).astype(BF16)
            h_ref[rows, :] = h
            o_ref[rows, :] = _dot(h, w_ref[...])

    @pl.when(j > 0)
    def _():
        o_ref[...] = _dot(h_ref[...], w_ref[...])


def in_proj(x, g, wb, layer, tm, tn):
    m, k = x.shape
    n = wb.shape[1]
    return pl.pallas_call(
        _in_proj_kernel,
        grid=(m // tm, n // tn),
        in_specs=[
            pl.BlockSpec((tm, k), lambda i, j: (i, 0)),
            pl.BlockSpec((None, 1, k), lambda i, j: (layer, 0, 0)),
            pl.BlockSpec((k, tn), lambda i, j: (0, j)),
        ],
        out_specs=pl.BlockSpec((tm, tn), lambda i, j: (i, j)),
        out_shape=jax.ShapeDtypeStruct((m, n), F32),
        scratch_shapes=[pltpu.VMEM((tm, k), BF16)],
        compiler_params=_cparams("parallel", "arbitrary"),
        name="in_proj",
    )(x, g, wb)


def _mix_dot(a, o, c, w):
    acc = _dot(a.astype(BF16), w[0:ATTN_WIDTH])
    acc = acc + _dot(o.astype(BF16), w[ATTN_WIDTH:ATTN_WIDTH + HG_WIDTH])
    return acc + _dot(c.astype(BF16), w[ATTN_WIDTH + HG_WIDTH:])


def _out_proj_sample_kernel(x_ref, a_ref, o_ref, c_ref, w_ref, y_ref, *wb_ref):
    w = w_ref[...]
    if wb_ref:
        w = w.astype(BF16)
        wb_ref[0][...] = w
    y_ref[...] = x_ref[...] + _mix_dot(a_ref[...], o_ref[...], c_ref[...], w)


def out_proj_sample(x, a, o, c, w_out, layer, tn):
    m, d = x.shape
    convert = w_out.ndim == 3
    kw = w_out.shape[-2]
    full = lambda width: pl.BlockSpec((m, width), lambda j: (0, 0))
    y_spec = pl.BlockSpec((m, tn), lambda j: (0, j))
    y_shape = jax.ShapeDtypeStruct((m, d), F32)
    if convert:
        w_spec = pl.BlockSpec((None, kw, tn), lambda j: (layer, 0, j))
        out_specs = [y_spec, pl.BlockSpec((kw, tn), lambda j: (0, j))]
        out_shape = [y_shape, jax.ShapeDtypeStruct((kw, d), BF16)]
    else:
        w_spec = pl.BlockSpec((kw, tn), lambda j: (0, j))
        out_specs, out_shape = y_spec, y_shape
    return pl.pallas_call(
        _out_proj_sample_kernel,
        grid=(d // tn,),
        in_specs=[y_spec, full(ATTN_WIDTH), full(HG_WIDTH), full(CONV_WIDTH), w_spec],
        out_specs=out_specs,
        out_shape=out_shape,
        compiler_params=_cparams("parallel"),
        name="out_proj_sample",
    )(x, a, o, c, w_out)


def _conv_taps(u, prev1, prev2, w_ref):
    return w_ref[0:1, :] * prev2 + w_ref[1:2, :] * prev1 + w_ref[2:3, :] * u


def _out_proj_kernel(tiles_per_seq, x_ref, a_ref, o_ref, cb_ref, cc_ref, ch_ref, cw_ref, w_ref,
                     y_ref, cs_ref, carry_ref):
    i = pl.program_id(0)
    tm = cc_ref.shape[0]

    @pl.when(i % tiles_per_seq == 0)
    def _():
        carry_ref[...] = jnp.zeros_like(carry_ref)

    u = cc_ref[...] * ch_ref[...]
    row = lax.broadcasted_iota(jnp.int32, u.shape, 0)
    last1 = carry_ref[7:8, :]
    last2 = carry_ref[6:7, :]
    um1 = jnp.where(row == 0, last1, pltpu.roll(u, 1, 0))
    um2 = jnp.where(row == 0, last2, jnp.where(row == 1, last1, pltpu.roll(u, 2, 0)))
    c = cb_ref[...] * _conv_taps(u, um1, um2, cw_ref)
    carry_ref[...] = u[tm - 8:tm]

    y_ref[...] = x_ref[...] + _mix_dot(a_ref[...], o_ref[...], c, w_ref)

    @pl.when(i % tiles_per_seq == tiles_per_seq - 1)
    def _():
        cs_ref[0] = carry_ref[8 - (CONV_K - 1):8, :]


def out_proj(x, a, o, proj, conv_w, wb, layer, seq_len, tm):
    m, d = x.shape
    tiles_per_seq = seq_len // tm
    col = lambda blk: pl.BlockSpec((tm, CONV_WIDTH), lambda i: (i, blk))
    return pl.pallas_call(
        functools.partial(_out_proj_kernel, tiles_per_seq),
        grid=(m // tm,),
        in_specs=[
            pl.BlockSpec((tm, d), lambda i: (i, 0)),
            pl.BlockSpec((tm, ATTN_WIDTH), lambda i: (i, 0)),
            pl.BlockSpec((tm, HG_WIDTH), lambda i: (i, 0)),
            col(CB_BLK_512), col(CC_BLK_512), col(CH_BLK_512),
            pl.BlockSpec((None, CONV_K, CONV_WIDTH), lambda i: (layer, 0, 0)),
            pl.BlockSpec(wb.shape, lambda i: (0, 0)),
        ],
        out_specs=[pl.BlockSpec((tm, d), lambda i: (i, 0)),
                   pl.BlockSpec((1, CONV_K - 1, CONV_WIDTH), lambda i: (i // tiles_per_seq, 0, 0))],
        out_shape=[jax.ShapeDtypeStruct((m, d), F32),
                   jax.ShapeDtypeStruct((m // seq_len, CONV_K - 1, CONV_WIDTH), F32)],
        scratch_shapes=[pltpu.VMEM((8, CONV_WIDTH), F32)],
        compiler_params=_cparams("arbitrary"),
        name="out_proj",
    )(x, a, o, proj, proj, proj, conv_w, wb)


def _ffn_step(h_ref, wg, wu, wd, y_ref):
    h = h_ref[...]
    gate = _dot(h, wg)
    up = _dot(h, wu)
    act = _silu(gate) * up
    y_ref[...] += _dot(act.astype(BF16), wd)


def _ffn_begin(x_ref, g_ref, y_ref, h_ref):
    x = x_ref[...]
    h_ref[...] = _rms(x, g_ref[...]).astype(BF16)
    y_ref[...] = x


def _ffn_sample_kernel(last_layer, x_ref, g_ref, wg_ref, wu_ref, wd_ref, gf_ref,
                       y_ref, wgb_ref, wub_ref, wdb_ref, h_ref):
    f = pl.program_id(0)

    @pl.when(f == 0)
    def _():
        _ffn_begin(x_ref, g_ref, y_ref, h_ref)

    wg = wg_ref[...].astype(BF16)
    wu = wu_ref[...].astype(BF16)
    wd = wd_ref[...].astype(BF16)
    wgb_ref[...] = wg
    wub_ref[...] = wu
    wdb_ref[...] = wd
    _ffn_step(h_ref, wg, wu, wd, y_ref)

    if last_layer:
        @pl.when(f == pl.num_programs(0) - 1)
        def _():
            y_ref[...] = _rms(y_ref[...], gf_ref[...])


def ffn_sample(x, g, w_gate_up, w_down, g_final, layer, tf):
    m, d = x.shape
    nf = D_FF // tf
    return pl.pallas_call(
        functools.partial(_ffn_sample_kernel, layer == DEPTH - 1),
        grid=(nf,),
        in_specs=[
            pl.BlockSpec((m, d), lambda f: (0, 0)),
            pl.BlockSpec((None, 1, d), lambda f: (layer, 0, 0)),
            pl.BlockSpec((None, d, tf), lambda f: (layer, 0, f)),
            pl.BlockSpec((None, d, tf), lambda f: (layer, 0, f + nf)),
            pl.BlockSpec((None, tf, d), lambda f: (layer, f, 0)),
            pl.BlockSpec((1, d), lambda f: (0, 0)),
        ],
        out_specs=[
            pl.BlockSpec((m, d), lambda f: (0, 0)),
            pl.BlockSpec((d, tf), lambda f: (0, f)),
            pl.BlockSpec((d, tf), lambda f: (0, f)),
            pl.BlockSpec((tf, d), lambda f: (f, 0)),
        ],
        out_shape=[jax.ShapeDtypeStruct((m, d), F32), jax.ShapeDtypeStruct((d, D_FF), BF16),
                   jax.ShapeDtypeStruct((d, D_FF), BF16), jax.ShapeDtypeStruct((D_FF, d), BF16)],
        scratch_shapes=[pltpu.VMEM((m, d), BF16)],
        compiler_params=_cparams("arbitrary"),
        name="ffn_sample",
    )(x, g, w_gate_up, w_gate_up, w_down, g_final)


def _ffn_kernel(last_layer, x_ref, g_ref, wg_ref, wu_ref, wd_ref, gf_ref, y_ref, h_ref, side=lambda: None):
    f = pl.program_id(1)
    rs = x_ref.shape[0] // NORM_ROW_SPLIT

    @pl.when(f == 0)
    def _():
        side()
        for r in range(NORM_ROW_SPLIT):
            rows = slice(r * rs, (r + 1) * rs)
            x = x_ref[rows, :]
            h = _rms(x, g_ref[...]).astype(BF16)
            h_ref[rows, :] = h
            act = _silu(_dot(h, wg_ref[...])) * _dot(h, wu_ref[...])
            y_ref[rows, :] = x + _dot(act.astype(BF16), wd_ref[...])

    @pl.when(f > 0)
    def _():
        side()
        _ffn_step(h_ref, wg_ref[...], wu_ref[...], wd_ref[...], y_ref)

    if last_layer:
        @pl.when(f == pl.num_programs(1) - 1)
        def _():
            y_ref[...] = _rms(y_ref[...], gf_ref[...])


def _side_cast(w, slab, axis, grid):
    nsteps = grid[0] * grid[1]
    shape = list(w.shape[1:])
    assert shape[axis] % nsteps == 0
    shape[axis] //= nsteps
    pos = lambda i, j: i * grid[1] + j
    idx = (lambda i, j: (pos(i, j), 0)) if axis == 0 else (lambda i, j: (0, pos(i, j)))
    src = pl.BlockSpec((None, *shape), lambda i, j: (slab, *idx(i, j)))
    return src, pl.BlockSpec(tuple(shape), idx), jax.ShapeDtypeStruct(w.shape[1:], BF16)


def _cast_sides(src_refs, dst_refs):
    for src, dst in zip(src_refs, dst_refs):
        dst[...] = src[...].astype(BF16)


FFN_CONVERT_VMEM = 60000 * 1024


def _ffn_convert_kernel(x_ref, g_ref, wg_ref, wu_ref, wd_ref, gf_ref, side_src, y_ref, side_dst, h_ref):
    _ffn_kernel(False, x_ref, g_ref, wg_ref, wu_ref, wd_ref, gf_ref, y_ref, h_ref,
                side=lambda: _cast_sides([side_src], [side_dst]))


def ffn(x, g, w_gate, gate_off, w_up, up_off, wdb, g_final, layer, tm, tf, convert=None):
    m, d = x.shape
    grid = (m // tm, D_FF // tf)
    in_specs = [
        pl.BlockSpec((tm, d), lambda i, f: (i, 0)),
        pl.BlockSpec((None, 1, d), lambda i, f: (layer, 0, 0)),
        pl.BlockSpec((d, tf), lambda i, f: (0, gate_off + f)),
        pl.BlockSpec((d, tf), lambda i, f: (0, up_off + f)),
        pl.BlockSpec((tf, d), lambda i, f: (f, 0)),
        pl.BlockSpec((1, d), lambda i, f: (0, 0)),
    ]
    y_spec = pl.BlockSpec((tm, d), lambda i, f: (i, 0))
    y_shape = jax.ShapeDtypeStruct((m, d), F32)
    if convert is None:
        return pl.pallas_call(
            functools.partial(_ffn_kernel, layer == DEPTH - 1),
            grid=grid, in_specs=in_specs, out_specs=y_spec, out_shape=y_shape,
            scratch_shapes=[pltpu.VMEM((tm, d), BF16)],
            compiler_params=_cparams("parallel", "arbitrary"),
            name="ffn",
        )(x, g, w_gate, w_up, wdb, g_final)

    src, dst, dst_shape = _side_cast(convert, layer + 1, 1, grid)
    return pl.pallas_call(
        _ffn_convert_kernel,
        grid=grid,
        in_specs=in_specs + [src],
        out_specs=[y_spec, dst],
        out_shape=[y_shape, dst_shape],
        scratch_shapes=[pltpu.VMEM((tm, d), BF16)],
        compiler_params=_cparams("parallel", "arbitrary", vmem=FFN_CONVERT_VMEM),
        name="ffn_convert",
    )(x, g, w_gate, w_up, wdb, g_final, convert)


def _rope_tables(pos):
    half = ROT_DIM // 2
    inv = ROPE_THETA ** (-jnp.arange(half, dtype=F32) * 2.0 / ROT_DIM)
    ang = pos.astype(F32)[:, None] * inv[None, :]
    cos, sin = jnp.cos(ang), jnp.sin(ang)
    n = pos.shape[0]
    ones = jnp.ones((n, HEAD_DIM - ROT_DIM), F32)
    zeros = jnp.zeros((n, HEAD_DIM - ROT_DIM), F32)
    zh = jnp.zeros((n, half), F32)
    c = jnp.concatenate([cos, cos, ones], axis=-1)
    s_lo = jnp.concatenate([-sin, zh, zeros], axis=-1)
    s_hi = jnp.concatenate([zh, sin, zeros], axis=-1)
    rep = lambda t: jnp.concatenate([t, t], axis=-1)
    return rep(c), rep(s_lo), rep(s_hi)


def _rope(x, c, s_lo, s_hi):
    half = ROT_DIM // 2
    outs = []
    for j in range(x.shape[1] // 128):
        xc = x[:, j * 128:(j + 1) * 128]
        outs.append(xc * c + pltpu.roll(xc, 128 - half, 1) * s_lo + pltpu.roll(xc, half, 1) * s_hi)
    return outs[0] if len(outs) == 1 else jnp.concatenate(outs, axis=1)


def _rope_tables_t(pos):
    half = ROT_DIM // 2
    inv = ROPE_THETA ** (-jnp.arange(half, dtype=F32) * 2.0 / ROT_DIM)
    ang = inv[:, None] * pos.astype(F32)[None, :]
    return jnp.cos(ang), jnp.sin(ang)


def _rope_t(xt, cos_t, sin_t):
    half = ROT_DIM // 2
    pieces = []
    for base in range(0, xt.shape[0], HEAD_DIM):
        x1 = xt[base:base + half]
        x2 = xt[base + half:base + ROT_DIM]
        pieces += [x1 * cos_t - x2 * sin_t, x2 * cos_t + x1 * sin_t, xt[base + ROT_DIM:base + HEAD_DIM]]
    return jnp.concatenate(pieces, axis=0)


def _softmax_sink_pv(s, sk, v):
    m = jnp.maximum(jnp.max(s, axis=-1, keepdims=True), sk)
    p = jnp.exp(s - m)
    denom = jnp.sum(p, axis=-1, keepdims=True) + jnp.exp(sk - m)
    return _dot(p.astype(BF16), v) / denom


N_ATTN_PROMPT_IN, N_ATTN_PROMPT_OUT = 9, 3


def _attn_prompt_kernel(layer, nside, *refs):
    it = iter(refs)
    take = lambda cnt: [next(it) for _ in range(cnt)]
    sink_ref, q_ref, k_ref, v_ref, c_ref, slo_ref, shi_ref, ct_ref, st_ref = take(N_ATTN_PROMPT_IN)
    side_src = take(nside)
    a_ref, ko_ref, vo_ref = take(N_ATTN_PROMPT_OUT)
    side_dst = take(nside)
    kprev_ref, vtprev_ref = take(2)

    n = pl.program_id(1)
    nstep = pl.num_programs(1)
    blk = WINDOW
    nsub = q_ref.shape[1] // blk
    lanes = Q_PER_KV * blk
    sub = lambda j: slice(j * blk, (j + 1) * blk)

    k_all = _rope(k_ref[0], c_ref[...], slo_ref[...], shi_ref[...])
    v_all = v_ref[0]
    vt_all = v_all.T
    qt_all = (_rope_t(q_ref[0].T, ct_ref[...], st_ref[...]) * ATTN_SCALE).astype(BF16)

    key = lax.broadcasted_iota(jnp.int32, (blk, lanes), 0)
    qry = lax.broadcasted_iota(jnp.int32, (blk, lanes), 1) % blk
    newer = key > qry
    zero_rows = jnp.zeros((HEAD_DIM, lanes), BF16)

    def scores(h, j, first):
        pair = slice((h // 2) * 2 * HEAD_DIM, (h // 2 + 1) * 2 * HEAD_DIM)
        qt = qt_all[:, sub(j)]
        qh = jnp.concatenate(
            [qt[(Q_PER_KV * h + g) * HEAD_DIM:(Q_PER_KV * h + g + 1) * HEAD_DIM] for g in range(Q_PER_KV)], axis=1)
        qz = jnp.concatenate([qh, zero_rows] if h % 2 == 0 else [zero_rows, qh], axis=0)
        s_cur = _dot(k_all[sub(j), pair].astype(BF16), qz)
        if first:
            return None, s_cur
        k_prev = kprev_ref[:, pair] if j == 0 else k_all[sub(j - 1), pair]
        return _dot(k_prev.astype(BF16), qz), s_cur

    def softmax(h, s_prev, s_cur):
        sink = jnp.concatenate(
            [jnp.full((1, blk), sink_ref[layer, Q_PER_KV * h + g], F32) for g in range(Q_PER_KV)], axis=1)
        if s_prev is None:
            f = jnp.where(newer, NEG_BIG, s_cur)
            m = jnp.maximum(jnp.max(f, axis=0, keepdims=True), sink)
            p = jnp.exp(f - m)
            denom = jnp.sum(p, axis=0, keepdims=True) + jnp.exp(sink - m)
            return p.astype(BF16), None, denom
        f = jnp.where(newer, s_prev, s_cur)
        d = jnp.sum(jnp.where(key == qry, s_prev, 0.0), axis=0, keepdims=True)
        m = jnp.maximum(jnp.maximum(jnp.max(f, axis=0, keepdims=True), d), sink)
        p = jnp.exp(f - m)
        pd = jnp.exp(d - m)
        denom = jnp.sum(p, axis=0, keepdims=True) + pd + jnp.exp(sink - m)
        pp = jnp.concatenate([jnp.where(newer, p, 0.0), jnp.where(newer, 0.0, p)], axis=0).astype(BF16)
        return pp, pd, denom

    def weighted_values(h, j, pp, pd, denom):
        rows = slice(h * HEAD_DIM, (h + 1) * HEAD_DIM)
        vt_cur = vt_all[rows, sub(j)]
        inv = 1.0 / denom
        if pd is None:
            return _dot(vt_cur.astype(BF16), pp) * inv
        vt_prev = vtprev_ref[rows, :] if j == 0 else vt_all[rows, sub(j - 1)]
        o = _dot(jnp.concatenate([vt_prev, vt_cur], axis=1).astype(BF16), pp)
        return (o + pd * jnp.concatenate([vt_prev] * Q_PER_KV, axis=1)) * inv

    def run(first_step):
        _cast_sides(side_src, side_dst)
        units = [(h, j) for j in range(nsub) for h in range(N_KV_HEADS)]
        s = {u: scores(*u, first_step and u[1] == 0) for u in units}
        p = {u: softmax(u[0], *s[u]) for u in units}
        o = {u: weighted_values(*u, *p[u]) for u in units}
        for j in range(nsub):
            outs = [o[h, j][:, g * blk:(g + 1) * blk] for h in range(N_KV_HEADS) for g in range(Q_PER_KV)]
            a_ref[0, sub(j), :] = jnp.concatenate(outs, axis=0).T.astype(a_ref.dtype)

    @pl.when(n == 0)
    def _():
        run(True)

    @pl.when(n > 0)
    def _():
        run(False)

    kprev_ref[...] = k_all[sub(nsub - 1)]
    vtprev_ref[...] = vt_all[:, sub(nsub - 1)]

    @pl.when(n == nstep - 1)
    def _():
        ko_ref[0] = k_all[sub(nsub - 1)]
        vo_ref[0] = v_all[sub(nsub - 1)]


def attn_prompt(proj, sink, tables, tables_t, layer, rows, convert=()):
    b, l, _ = proj.shape
    blk = WINDOW
    grid = (b, l // rows)
    tab_spec = pl.BlockSpec((rows, 128), lambda i, n: (n, 0))
    tab_t_spec = pl.BlockSpec((ROT_DIM // 2, rows), lambda i, n: (0, n))
    kv_out = jax.ShapeDtypeStruct((b, blk, KV_WIDTH), F32)
    sides = [_side_cast(w, layer + 1, 0, grid) for w in convert]
    in_specs = [
        pl.BlockSpec(memory_space=pltpu.SMEM),
        pl.BlockSpec((1, rows, ATTN_WIDTH), lambda i, n: (i, n, Q_BLK_1024)),
        pl.BlockSpec((1, rows, KV_WIDTH), lambda i, n: (i, n, K_BLK_256)),
        pl.BlockSpec((1, rows, KV_WIDTH), lambda i, n: (i, n, V_BLK_256)),
        tab_spec, tab_spec, tab_spec, tab_t_spec, tab_t_spec,
    ]
    out_specs = [
        pl.BlockSpec((1, rows, ATTN_WIDTH), lambda i, n: (i, n, 0)),
        pl.BlockSpec((1, blk, KV_WIDTH), lambda i, n: (i, 0, 0)),
        pl.BlockSpec((1, blk, KV_WIDTH), lambda i, n: (i, 0, 0)),
    ]
    assert (len(in_specs), len(out_specs)) == (N_ATTN_PROMPT_IN, N_ATTN_PROMPT_OUT)
    return pl.pallas_call(
        functools.partial(_attn_prompt_kernel, layer, len(sides)),
        grid=grid,
        in_specs=in_specs + [s[0] for s in sides],
        out_specs=out_specs + [s[1] for s in sides],
        out_shape=[jax.ShapeDtypeStruct((b, l, ATTN_WIDTH), BF16), kv_out, kv_out] + [s[2] for s in sides],
        scratch_shapes=[pltpu.VMEM((blk, KV_WIDTH), F32), pltpu.VMEM((KV_WIDTH, blk), F32)],
        compiler_params=_cparams("parallel", "arbitrary"),
        name="attn_prompt",
    )(sink, proj, proj, proj, *tables, *tables_t, *convert)


def _attn_sample_kernel(layer, sink_ref, q_ref, k_ref, v_ref, kc_ref, vc_ref, c_ref, slo_ref, shi_ref,
                        k_carried_ref, v_carried_ref, a_ref, ko_ref, vo_ref):
    del k_carried_ref, v_carried_ref
    r8 = SAMPLE_ROWS
    nseq = r8 // SAMPLE_T
    lb = kc_ref.shape[1]
    c, s_lo, s_hi = c_ref[...], slo_ref[...], shi_ref[...]
    q = _rope(q_ref[...], c, s_lo, s_hi)
    k = _rope(k_ref[...], c, s_lo, s_hi)
    v = v_ref[...]
    pad = jnp.zeros((lb - r8, KV_WIDTH), F32)
    k_pad = jnp.concatenate([k, pad], axis=0)
    v_pad = jnp.concatenate([v, pad], axis=0)

    zeros = jnp.zeros((r8, HEAD_DIM), F32)
    qx = jnp.concatenate(
        [jnp.concatenate([q[:, hg * HEAD_DIM:(hg + 1) * HEAD_DIM] if slot == hg // Q_PER_KV else zeros
                          for slot in range(N_KV_HEADS)], axis=1) for hg in range(N_Q_HEADS)],
        axis=0).astype(BF16)
    sk = jnp.concatenate([jnp.full((r8, 1), sink_ref[layer, hg], F32) for hg in range(N_Q_HEADS)], axis=0)

    rows = N_Q_HEADS * r8
    ri = lax.broadcasted_iota(jnp.int32, (rows, 2 * lb), 0) % r8
    t = ri % SAMPLE_T
    kj = lax.broadcasted_iota(jnp.int32, (rows, 2 * lb), 1)
    cj = kj - lb
    row_seq = lax.broadcasted_iota(jnp.int32, (rows, KV_WIDTH), 0) % r8 // SAMPLE_T

    s = []
    for e in range(nseq):
        kk = jnp.concatenate([kc_ref[e], k_pad], axis=0).astype(BF16)
        s.append(_dot_nt(qx, kk) * ATTN_SCALE)
    o = None
    for e in range(nseq):
        mask = ((kj < lb) & (kj >= t)) | (
            (cj >= e * SAMPLE_T) & (cj < (e + 1) * SAMPLE_T) & (cj - e * SAMPLE_T <= t))
        vv = jnp.concatenate([vc_ref[e], v_pad], axis=0).astype(BF16)
        oe = _softmax_sink_pv(jnp.where(mask, s[e], NEG_BIG), sk, vv)
        o = oe if o is None else jnp.where(row_seq == e, oe, o)
    a_ref[...] = jnp.concatenate(
        [o[hg * r8:(hg + 1) * r8, (hg // Q_PER_KV) * HEAD_DIM:(hg // Q_PER_KV + 1) * HEAD_DIM]
         for hg in range(N_Q_HEADS)], axis=1)

    row = lax.broadcasted_iota(jnp.int32, (lb, KV_WIDTH), 0)
    for e in range(nseq):
        shift = (lb - SAMPLE_T - e * SAMPLE_T) % lb
        for new_pad, cache_ref, out_ref in ((k_pad, kc_ref, ko_ref), (v_pad, vc_ref, vo_ref)):
            new_rows = new_pad if shift == 0 else pltpu.roll(new_pad, shift, 0)
            old_rows = pltpu.roll(cache_ref[e], lb - SAMPLE_T, 0)
            out_ref[e] = jnp.where(row >= lb - SAMPLE_T, new_rows, old_rows)


def _lower_bound(lbr_ref, layer, hs):
    rows = [lbr_ref[r:r + 1, hs] for r in range(DEPTH)]
    mx = functools.reduce(jnp.maximum, rows)
    es = [jnp.exp(r - mx) for r in rows]
    tot = functools.reduce(lambda a, b: a + b, es)
    lbp = [e / tot for e in es]
    acc = lbp[0]
    for r in range(1, layer + 1):
        acc = acc + lbp[r]
    return acc - lbp[0]


def _hgrn_gates(hq, hf, lb):
    q = _silu(hq) * (HG_DK ** -0.5)
    e = jnp.exp(-jnp.abs(hf))
    r = 1.0 / (1.0 + e)
    pos = hf >= 0.0
    sig_pos = jnp.where(pos, r, e * r)
    sig_neg = jnp.where(pos, e * r, r)
    log_f = jnp.log(jnp.maximum(lb, LB_TINY) + (1.0 - lb) * sig_pos)
    k = (1.0 - lb) * sig_neg
    return q, log_f, k


def _hgrn_finish(o, g_norm, hg):
    return _rms(o, g_norm) * _silu(hg)


def _hgrn_levels():
    s = HG_CHUNK // 2
    while s >= 1:
        yield s
        s //= 2


def _hgrn_level_masks():
    t = np.arange(HG_CHUNK)[:, None]
    s = np.arange(HG_CHUNK)[None, :]
    return np.stack([((t % (2 * h) >= h) & (t // (2 * h) == s // (2 * h)) & (s % (2 * h) < h))
                     for h in _hgrn_levels()]).astype(np.float32)


def _hgrn_pair_reference(g_ref, gcum, half, sub8):
    c = HG_CHUNK
    pair = 2 * half
    bcast = lambda r: jnp.broadcast_to(g_ref[r:r + 1, :], (8, HG_DK))
    if pair >= 8:
        return jnp.concatenate([bcast((8 * v // pair) * pair + half - 1) for v in range(c // 8)], axis=0)
    if pair == 4:
        lo = jnp.concatenate([bcast(8 * v + 1) for v in range(c // 8)], axis=0)
        hi = jnp.concatenate([bcast(8 * v + 5) for v in range(c // 8)], axis=0)
        return jnp.where(sub8 < 4, lo, hi)
    return jnp.where(sub8 % 2 == 1, pltpu.roll(gcum, 1, 0), gcum)


def _hgrn_level_operand(q, k, gcum, g_ref, half, row):
    ref = _hgrn_pair_reference(g_ref, gcum, half, row % 8)
    right = row % (2 * half) >= half
    return (jnp.where(right, q, k) * jnp.exp(-jnp.abs(gcum - ref))).astype(BF16)


def _chunk_cumsum(x, buf, row):
    sh = 1
    while sh < HG_CHUNK:
        x = x + jnp.where(row >= sh, pltpu.roll(x, sh, 0), 0.0)
        sh *= 2
    buf[...] = x
    return x


def _hgrn_tile(layer, hq_ref, hf_ref, hi_ref, hg_ref, lbr_ref, gn_ref, m_ref, g_ref, state, emit):
    c = HG_CHUNK
    nchunk = hq_ref.shape[1] // c
    heads = range(HG_HEADS)
    units = [(h, cc) for cc in range(nchunk) for h in heads]
    row = lax.broadcasted_iota(jnp.int32, (c, HG_DK), 0)
    cols = lambda h: slice(h * HG_DK, (h + 1) * HG_DK)
    rws = lambda cc: slice(cc * c, (cc + 1) * c)
    gbuf = lambda h, cc: g_ref.at[h * nchunk + cc]
    lbs = [_lower_bound(lbr_ref, layer, cols(h)) for h in heads]

    q, k, v, gcum = {}, {}, {}, {}
    for u in units:
        h, cc = u
        q[u], log_f, k[u] = _hgrn_gates(hq_ref[0, rws(cc), cols(h)], hf_ref[0, rws(cc), cols(h)], lbs[h])
        v[u] = hi_ref[0, rws(cc), cols(h)]
        gcum[u] = _chunk_cumsum(log_f, gbuf(h, cc), row)

    a = {u: jnp.zeros((c, c), F32) for u in units}
    for lvl, half in enumerate(_hgrn_levels()):
        for u in units:
            x = _hgrn_level_operand(q[u], k[u], gcum[u], gbuf(*u), half, row)
            a[u] = a[u] + _dot_nt(x, x) * m_ref[lvl]

    o_intra, qs, ks, decay = {}, {}, {}, {}
    for u in units:
        vb = v[u].astype(BF16)
        o_intra[u] = _dot(a[u].astype(BF16), vb) + jnp.sum(q[u] * k[u], axis=-1, keepdims=True) * v[u]
        g_last = gbuf(*u)[c - 1:c, :]
        qs[u] = (q[u] * jnp.exp(gcum[u])).astype(BF16)
        ks[u] = (k[u] * jnp.exp(g_last - gcum[u])).astype(BF16)
        decay[u] = jnp.transpose(jnp.broadcast_to(jnp.exp(g_last), (HG_DK, HG_DK)))

    state = list(state)
    for cc in range(nchunk):
        o_inter = [_dot(qs[h, cc], state[h].astype(BF16)) for h in heads]
        update = [_dot_tn(ks[h, cc], v[h, cc].astype(BF16)) for h in heads]
        for h in heads:
            o = o_inter[h] + o_intra[h, cc]
            emit(h, rws(cc), _hgrn_finish(o, gn_ref[h:h + 1, :], hg_ref[0, rws(cc), cols(h)]))
            state[h] = state[h] * decay[h, cc] + update[h]
    return state


N_HGRN_PROMPT_IN, N_HGRN_PROMPT_OUT = 7, 2


def _hgrn_prompt_kernel(layer, nside, *refs):
    it = iter(refs)
    take = lambda cnt: [next(it) for _ in range(cnt)]
    hq_ref, hf_ref, hi_ref, hg_ref, lbr_ref, gn_ref, m_ref = take(N_HGRN_PROMPT_IN)
    side_src = take(nside)
    o_ref, so_ref = take(N_HGRN_PROMPT_OUT)
    side_dst = take(nside)
    s_ref, g_ref = take(2)
    step = pl.program_id(1)
    heads = range(HG_HEADS)

    @pl.when(step == 0)
    def _():
        s_ref[...] = jnp.zeros_like(s_ref)

    def emit(h, rs, o):
        o_ref[0, rs, h * HG_DV:(h + 1) * HG_DV] = o.astype(o_ref.dtype)

    _cast_sides(side_src, side_dst)
    state = _hgrn_tile(layer, hq_ref, hf_ref, hi_ref, hg_ref, lbr_ref, gn_ref, m_ref, g_ref,
                       [s_ref[h] for h in heads], emit)
    for h in heads:
        s_ref[h] = state[h]

    @pl.when(step == pl.num_programs(1) - 1)
    def _():
        so_ref[0] = s_ref[...]


def hgrn_prompt(proj, lb_raw, g_norm, layer, rows, convert=()):
    b, l, _ = proj.shape
    c = HG_CHUNK
    grid = (b, l // rows)
    masks = jnp.asarray(_hgrn_level_masks())
    col = lambda blk: pl.BlockSpec((1, rows, HG_WIDTH), lambda i, n: (i, n, blk))
    sides = [_side_cast(w, layer + 1, 0, grid) for w in convert]
    in_specs = [
        col(HQ_BLK_512), col(HF_BLK_512), col(HI_BLK_512), col(HG_BLK_512),
        pl.BlockSpec((DEPTH, HG_WIDTH), lambda i, n: (0, 0)),
        pl.BlockSpec((None, HG_HEADS, HG_DV), lambda i, n: (layer, 0, 0)),
        pl.BlockSpec(masks.shape, lambda i, n: (0, 0, 0)),
    ]
    out_specs = [
        pl.BlockSpec((1, rows, HG_WIDTH), lambda i, n: (i, n, 0)),
        pl.BlockSpec((1, HG_HEADS, HG_DK, HG_DV), lambda i, n: (i, 0, 0, 0)),
    ]
    assert (len(in_specs), len(out_specs)) == (N_HGRN_PROMPT_IN, N_HGRN_PROMPT_OUT)
    return pl.pallas_call(
        functools.partial(_hgrn_prompt_kernel, layer, len(sides)),
        grid=grid,
        in_specs=in_specs + [s[0] for s in sides],
        out_specs=out_specs + [s[1] for s in sides],
        out_shape=[jax.ShapeDtypeStruct((b, l, HG_WIDTH), BF16),
                   jax.ShapeDtypeStruct((b, HG_HEADS, HG_DK, HG_DV), F32)] + [s[2] for s in sides],
        scratch_shapes=[pltpu.VMEM((HG_HEADS, HG_DK, HG_DV), F32),
                        pltpu.VMEM((HG_HEADS * (rows // c), c, HG_DK), F32)],
        compiler_params=_cparams("parallel", "arbitrary"),
        name="hgrn_prompt",
    )(proj, proj, proj, proj, lb_raw, g_norm, masks, *convert)


def _hgrn_sample_kernel(layer, hq_ref, hf_ref, hi_ref, hg_ref, lbr_ref, gn_ref, s_ref, carried_ref,
                        o_ref, so_ref, q_ref, k_ref, g_ref):
    del carried_ref
    r8, tt = SAMPLE_ROWS, SAMPLE_T
    nseq = r8 // tt
    row = lax.broadcasted_iota(jnp.int32, (r8, HG_DK), 0)
    t_idx = row % tt
    seq = row // tt

    def pick(ref, s):
        out = ref[s:s + 1, :]
        for e in range(1, nseq):
            out = jnp.where(seq == e, ref[e * tt + s:e * tt + s + 1, :], out)
        return out

    for h in range(HG_HEADS):
        hs = slice(h * HG_DK, (h + 1) * HG_DK)
        lb = _lower_bound(lbr_ref, layer, hs)
        q, log_f, k = _hgrn_gates(hq_ref[:, hs], hf_ref[:, hs], lb)
        gcum = log_f
        sh = 1
        while sh < tt:
            gcum = gcum + jnp.where(t_idx >= sh, pltpu.roll(gcum, sh, 0), 0.0)
            sh *= 2
        q_ref[...] = q
        k_ref[...] = k
        g_ref[...] = gcum
        v = hi_ref[:, hs]
        g_last = pick(g_ref, tt - 1)
        qs = (q * jnp.exp(gcum)).astype(BF16)
        ks_all = k * jnp.exp(g_last - gcum)

        o = jnp.zeros((r8, HG_DV), F32)
        for e in range(nseq):
            state = s_ref[e, h]
            o = jnp.where(seq == e, _dot(qs, state.astype(BF16)), o)
            g_last_e = g_ref[e * tt + tt - 1:e * tt + tt, :]
            decay = jnp.transpose(jnp.broadcast_to(jnp.exp(g_last_e), (HG_DK, HG_DK)))
            ks_e = jnp.where(seq == e, ks_all, 0.0).astype(BF16)
            so_ref[e, h] = state * decay + _dot_tn(ks_e, v.astype(BF16))
        for s in range(tt):
            dec = jnp.exp(jnp.where(t_idx >= s, gcum - pick(g_ref, s), NEG_BIG))
            col = jnp.sum(q * (dec * pick(k_ref, s)), axis=-1, keepdims=True)
            o = o + col * pick(hi_ref.at[:, hs], s)
        o_ref[:, hs] = _hgrn_finish(o, gn_ref[h:h + 1, :], hg_ref[:, hs])


def _conv_sample_kernel(cb_ref, cc_ref, ch_ref, w_ref, st_ref, y_ref, so_ref, u_ref):
    r8, tt = SAMPLE_ROWS, SAMPLE_T
    nseq = r8 // tt
    u = cc_ref[...] * ch_ref[...]
    u_ref[...] = u
    row = lax.broadcasted_iota(jnp.int32, u.shape, 0)
    t_idx = row % tt
    seq = row // tt
    last1 = st_ref[0, 1:2, :]
    last2 = st_ref[0, 0:1, :]
    for e in range(1, nseq):
        last1 = jnp.where(seq == e, st_ref[e, 1:2, :], last1)
        last2 = jnp.where(seq == e, st_ref[e, 0:1, :], last2)
    um1 = jnp.where(t_idx == 0, last1, pltpu.roll(u, 1, 0))
    um2 = jnp.where(t_idx == 0, last2, jnp.where(t_idx == 1, last1, pltpu.roll(u, 2, 0)))
    y_ref[...] = cb_ref[...] * _conv_taps(u, um1, um2, w_ref)
    for e in range(nseq):
        so_ref[e] = u_ref[(e + 1) * tt - (CONV_K - 1):(e + 1) * tt, :]


N_ATTN_SAMPLE_IN, N_HGRN_SAMPLE_IN, N_CONV_SAMPLE_IN = 11, 8, 5
N_ATTN_SAMPLE_OUT, N_HGRN_SAMPLE_OUT, N_CONV_SAMPLE_OUT = 3, 2, 2


def _mix_sample_kernel(layer, *refs):
    it = iter(refs)
    take = lambda n: [next(it) for _ in range(n)]
    attn_in, hgrn_in, conv_in = take(N_ATTN_SAMPLE_IN), take(N_HGRN_SAMPLE_IN), take(N_CONV_SAMPLE_IN)
    attn_out, hgrn_out, conv_out = take(N_ATTN_SAMPLE_OUT), take(N_HGRN_SAMPLE_OUT), take(N_CONV_SAMPLE_OUT)
    hgrn_scratch, conv_scratch = take(3), take(1)
    _attn_sample_kernel(layer, *attn_in, *attn_out)
    _hgrn_sample_kernel(layer, *hgrn_in, *hgrn_out, *hgrn_scratch)
    _conv_sample_kernel(*conv_in, *conv_out, *conv_scratch)


def mix_sample(proj, k_cache, v_cache, k_new, v_new, sink, tables, state, state_new, lb_raw, g_norm,
               conv_state, conv_w, layer):
    m = proj.shape[0]
    r8 = SAMPLE_ROWS
    nseq = r8 // SAMPLE_T
    lb = k_cache.shape[2]
    rows = lambda width, blk: pl.BlockSpec((r8, width), lambda i: (i, blk))
    tab_spec = pl.BlockSpec((r8, 128), lambda i: (0, 0))
    cache_spec = pl.BlockSpec((None, nseq, lb, KV_WIDTH), lambda i: (layer, i, 0, 0))
    st_spec = pl.BlockSpec((None, nseq, HG_HEADS, HG_DK, HG_DV), lambda i: (layer, i, 0, 0, 0))
    carried = pl.BlockSpec(memory_space=pl.ANY)
    attn_in = [pl.BlockSpec(memory_space=pltpu.SMEM),
               rows(ATTN_WIDTH, Q_BLK_1024), rows(KV_WIDTH, K_BLK_256), rows(KV_WIDTH, V_BLK_256),
               cache_spec, cache_spec, tab_spec, tab_spec, tab_spec, carried, carried]
    hgrn_in = [rows(HG_WIDTH, HQ_BLK_512), rows(HG_WIDTH, HF_BLK_512), rows(HG_WIDTH, HI_BLK_512),
               rows(HG_WIDTH, HG_BLK_512),
               pl.BlockSpec((DEPTH, HG_WIDTH), lambda i: (0, 0)),
               pl.BlockSpec((None, HG_HEADS, HG_DV), lambda i: (layer, 0, 0)),
               st_spec, carried]
    conv_in = [rows(CONV_WIDTH, CB_BLK_512), rows(CONV_WIDTH, CC_BLK_512), rows(CONV_WIDTH, CH_BLK_512),
               pl.BlockSpec((None, CONV_K, CONV_WIDTH), lambda i: (layer, 0, 0)),
               pl.BlockSpec((None, nseq, CONV_K - 1, CONV_WIDTH), lambda i: (layer, i, 0, 0))]
    assert (len(attn_in), len(hgrn_in), len(conv_in)) == (N_ATTN_SAMPLE_IN, N_HGRN_SAMPLE_IN, N_CONV_SAMPLE_IN)
    out_specs = [rows(ATTN_WIDTH, 0), cache_spec, cache_spec,
                 rows(HG_WIDTH, 0), st_spec,
                 rows(CONV_WIDTH, 0), pl.BlockSpec((nseq, CONV_K - 1, CONV_WIDTH), lambda i: (i, 0, 0))]
    out_shape = [jax.ShapeDtypeStruct((m, ATTN_WIDTH), F32),
                 jax.ShapeDtypeStruct(k_new.shape, F32), jax.ShapeDtypeStruct(v_new.shape, F32),
                 jax.ShapeDtypeStruct((m, HG_WIDTH), F32), jax.ShapeDtypeStruct(state_new.shape, F32),
                 jax.ShapeDtypeStruct((m, CONV_WIDTH), F32), jax.ShapeDtypeStruct(conv_state.shape[1:], F32)]
    return pl.pallas_call(
        functools.partial(_mix_sample_kernel, layer),
        grid=(m // r8,),
        in_specs=attn_in + hgrn_in + conv_in,
        out_specs=out_specs,
        out_shape=out_shape,
        input_output_aliases={9: 1, 10: 2, N_ATTN_SAMPLE_IN + N_HGRN_SAMPLE_IN - 1: 4},
        scratch_shapes=[pltpu.VMEM((r8, HG_DK), F32), pltpu.VMEM((r8, HG_DK), F32), pltpu.VMEM((r8, HG_DK), F32),
                        pltpu.VMEM((r8, CONV_WIDTH), F32)],
        compiler_params=_cparams("parallel"),
        name="mix_sample",
    )(sink, proj, proj, proj, k_cache, v_cache, *tables, k_new, v_new,
      proj, proj, proj, proj, lb_raw, g_norm, state, state_new,
      proj, proj, proj, conv_w, conv_state)


def kernel(x_prompt, x_sample, cache_attn_k, cache_attn_v, state_hgrn, state_conv, w_in, attn_sink,
           hgrn_lower_bounds, hgrn_norm, conv_w, w_out, norm_mix, norm_ffn, w_gate_up, w_down, norm_final):
    bp, lp, d = x_prompt.shape
    bs, ls, _ = x_sample.shape
    assert ls == SAMPLE_T and (bs * ls) % SAMPLE_ROWS == 0
    lb = cache_attn_k.shape[2]
    mp, ms = bp * lp, bs * ls

    pos_p = jnp.arange(lp, dtype=jnp.int32)
    tab_p = _rope_tables(pos_p)
    tab_pt = _rope_tables_t(pos_p)
    pos_s = PAST_LEN + jnp.arange(SAMPLE_ROWS, dtype=jnp.int32) % SAMPLE_T
    tab_s = _rope_tables(pos_s)

    xp = x_prompt.reshape(mp, d)
    xs = x_sample.reshape(ms, d)
    kc = cache_attn_k.reshape(DEPTH, bs, lb, KV_WIDTH)
    vc = cache_attn_v.reshape(DEPTH, bs, lb, KV_WIDTH)

    g_mix = norm_mix.reshape(DEPTH, 1, d)
    g_ffn = norm_ffn.reshape(DEPTH, 1, d)
    g_final = norm_final.reshape(1, d)

    k_s = jnp.zeros(kc.shape, F32)
    v_s = jnp.zeros(vc.shape, F32)
    s_s = jnp.zeros(state_hgrn.shape, F32)

    tf = 512
    outs = {name: [] for name in ("kp", "vp", "sp", "cp", "cs")}
    for l in range(DEPTH):
        if l == 0:
            proj_s, w_in_b = in_proj_sample(xs, g_mix, w_in, l, 1024)
        else:
            proj_s = in_proj(xs, g_mix, w_in_b, l, ms, 1024)
        a_s, k_s, v_s, o_s, s_s, c_s, cst_s = mix_sample(
            proj_s, kc, vc, k_s, v_s, attn_sink, tab_s, state_hgrn, s_s, hgrn_lower_bounds, hgrn_norm,
            state_conv, conv_w, l)
        if l == 0:
            xs, w_out_b = out_proj_sample(xs, a_s, o_s, c_s, w_out, l, 512)
            xs, w_gate_b, w_up_b, w_down_b = ffn_sample(xs, g_ffn, w_gate_up, w_down, g_final, l, tf)
            gate_up = (w_gate_b, 0, w_up_b, 0)
        else:
            xs = out_proj_sample(xs, a_s, o_s, c_s, w_out_b, l, 512)
            xs = ffn(xs, g_ffn, *gate_up, w_down_b, g_final, l, ms, tf)

        more = l < DEPTH - 1
        proj_p = in_proj(xp, g_mix, w_in_b, l, 1024, 1024)
        proj_p3 = proj_p.reshape(bp, lp, IN_WIDTH)
        a_p, k_p, v_p, *cast_a = attn_prompt(proj_p3, attn_sink, tab_p, tab_pt, l, 4 * WINDOW,
                                             convert=(w_in, w_out) if more else ())
        o_p, s_p, *cast_h = hgrn_prompt(proj_p3, hgrn_lower_bounds, hgrn_norm, l, 8 * HG_CHUNK,
                                        convert=(w_down,) if more else ())
        xp, cst_p = out_proj(xp, a_p.reshape(mp, ATTN_WIDTH), o_p.reshape(mp, HG_WIDTH), proj_p,
                             conv_w, w_out_b, l, lp, 512)
        if more:
            xp, w_gu_b = ffn(xp, g_ffn, *gate_up, w_down_b, g_final, l, 1024, tf, convert=w_gate_up)
            (w_in_b, w_out_b), (w_down_b,) = cast_a, cast_h
            gate_up = (w_gu_b, 0, w_gu_b, D_FF // tf)
        else:
            xp = ffn(xp, g_ffn, *gate_up, w_down_b, g_final, l, 1024, tf)

        outs["kp"].append(k_p.reshape(bp, WINDOW, N_KV_HEADS, HEAD_DIM))
        outs["vp"].append(v_p.reshape(bp, WINDOW, N_KV_HEADS, HEAD_DIM))
        outs["sp"].append(s_p)
        outs["cp"].append(cst_p)
        outs["cs"].append(cst_s)

    st = lambda name: jnp.stack(outs[name])
    kv_shape = (DEPTH, bs, lb, N_KV_HEADS, HEAD_DIM)
    return (xp.reshape(bp, lp, d), xs.reshape(bs, ls, d), st("kp"), st("vp"), st("sp"), st("cp"),
            k_s.reshape(kv_shape), v_s.reshape(kv_shape), s_s, st("cs"))
```

```python
import functools

import jax
import jax.numpy as jnp
import numpy as np
from jax import lax
from jax.experimental import pallas as pl
from jax.experimental.pallas import tpu as pltpu

F32 = jnp.float32
BF16 = jnp.bfloat16

D_MODEL = 2048
DEPTH = 4
PAST_LEN = 16384
HEAD_DIM = 64
N_Q_HEADS = 16
N_KV_HEADS = 4
Q_PER_KV = 4
ATTN_WIDTH = 1024
KV_WIDTH = 256
WINDOW = 128
ROPE_THETA = 500000.0
ROT_DIM = 16
ATTN_SCALE = HEAD_DIM ** -0.5
NEG_BIG = -1e30
HG_HEADS = 4
HG_DK = 128
HG_DV = 128
HG_WIDTH = 512
HG_CHUNK = 64
LB_TINY = 1e-30
CONV_K = 3
CONV_WIDTH = 512
D_FF = 5632
IN_WIDTH = 5120
NORM_EPS = 1e-6
SAMPLE_T = 4
SAMPLE_ROWS = 8

Q_BLK_1024 = 0
K_BLK_256 = 4
V_BLK_256 = 5
HQ_BLK_512, HF_BLK_512, HI_BLK_512, HG_BLK_512 = 3, 4, 5, 6
CB_BLK_512, CC_BLK_512, CH_BLK_512 = 7, 8, 9

VMEM_LIMIT = 56 * 1024 * 1024


def _cparams(*sem, vmem=VMEM_LIMIT):
    return pltpu.CompilerParams(dimension_semantics=sem, vmem_limit_bytes=vmem)


def _rms(x, g):
    ms = jnp.mean(x * x, axis=-1, keepdims=True)
    return x * lax.rsqrt(ms + NORM_EPS) * g


def _silu(x):
    return x * (1.0 / (1.0 + jnp.exp(-x)))


def _dot(a, b):
    return jnp.dot(a, b, preferred_element_type=F32)


def _dot_nt(a, b):
    return lax.dot_general(a, b, (((1,), (1,)), ((), ())), preferred_element_type=F32)


def _dot_tn(a, b):
    return lax.dot_general(a, b, (((0,), (0,)), ((), ())), preferred_element_type=F32)


def _in_proj_sample_kernel(x_ref, g_ref, w_ref, o_ref, wb_ref, h_ref):
    @pl.when(pl.program_id(0) == 0)
    def _():
        h_ref[...] = _rms(x_ref[...], g_ref[...]).astype(BF16)

    wb = w_ref[...].astype(BF16)
    wb_ref[...] = wb
    o_ref[...] = _dot(h_ref[...], wb)


def in_proj_sample(x, g, w, layer, tn):
    m, k = x.shape
    n = w.shape[2]
    return pl.pallas_call(
        _in_proj_sample_kernel,
        grid=(n // tn,),
        in_specs=[
            pl.BlockSpec((m, k), lambda j: (0, 0)),
            pl.BlockSpec((None, 1, k), lambda j: (layer, 0, 0)),
            pl.BlockSpec((None, k, tn), lambda j: (layer, 0, j)),
        ],
        out_specs=[pl.BlockSpec((m, tn), lambda j: (0, j)), pl.BlockSpec((k, tn), lambda j: (0, j))],
        out_shape=[jax.ShapeDtypeStruct((m, n), F32), jax.ShapeDtypeStruct((k, n), BF16)],
        scratch_shapes=[pltpu.VMEM((m, k), BF16)],
        compiler_params=_cparams("arbitrary"),
        name="in_proj_sample",
    )(x, g, w)


NORM_ROW_SPLIT = 4


def _in_proj_kernel(x_ref, g_ref, w_ref, o_ref, h_ref):
    j = pl.program_id(1)
    rs = x_ref.shape[0] // NORM_ROW_SPLIT

    @pl.when(j == 0)
    def _():
        for r in range(NORM_ROW_SPLIT):
            rows = slice(r * rs, (r + 1) * rs)
            h = _rms(x_ref[rows, :], g_ref[...]).astype(BF16)
            h_ref[rows, :] = h
            o_ref[rows, :] = _dot(h, w_ref[...])

    @pl.when(j > 0)
    def _():
        o_ref[...] = _dot(h_ref[...], w_ref[...])


def in_proj(x, g, wb, layer, tm, tn):
    m, k = x.shape
    n = wb.shape[1]
    return pl.pallas_call(
        _in_proj_kernel,
        grid=(m // tm, n // tn),
        in_specs=[
            pl.BlockSpec((tm, k), lambda i, j: (i, 0)),
            pl.BlockSpec((None, 1, k), lambda i, j: (layer, 0, 0)),
            pl.BlockSpec((k, tn), lambda i, j: (0, j)),
        ],
        out_specs=pl.BlockSpec((tm, tn), lambda i, j: (i, j)),
        out_shape=jax.ShapeDtypeStruct((m, n), F32),
        scratch_shapes=[pltpu.VMEM((tm, k), BF16)],
        compiler_params=_cparams("parallel", "arbitrary", vmem=FFN_CONVERT_VMEM),
        name="in_proj",
    )(x, g, wb)


def _mix_dot(a, o, c, w):
    acc = _dot(a.astype(BF16), w[0:ATTN_WIDTH])
    acc = acc + _dot(o.astype(BF16), w[ATTN_WIDTH:ATTN_WIDTH + HG_WIDTH])
    return acc + _dot(c.astype(BF16), w[ATTN_WIDTH + HG_WIDTH:])


def _out_proj_sample_kernel(x_ref, a_ref, o_ref, c_ref, w_ref, y_ref, *wb_ref):
    w = w_ref[...]
    if wb_ref:
        w = w.astype(BF16)
        wb_ref[0][...] = w
    y_ref[...] = x_ref[...] + _mix_dot(a_ref[...], o_ref[...], c_ref[...], w)


def out_proj_sample(x, a, o, c, w_out, layer, tn):
    m, d = x.shape
    convert = w_out.ndim == 3
    kw = w_out.shape[-2]
    full = lambda width: pl.BlockSpec((m, width), lambda j: (0, 0))
    y_spec = pl.BlockSpec((m, tn), lambda j: (0, j))
    y_shape = jax.ShapeDtypeStruct((m, d), F32)
    if convert:
        w_spec = pl.BlockSpec((None, kw, tn), lambda j: (layer, 0, j))
        out_specs = [y_spec, pl.BlockSpec((kw, tn), lambda j: (0, j))]
        out_shape = [y_shape, jax.ShapeDtypeStruct((kw, d), BF16)]
    else:
        w_spec = pl.BlockSpec((kw, tn), lambda j: (0, j))
        out_specs, out_shape = y_spec, y_shape
    return pl.pallas_call(
        _out_proj_sample_kernel,
        grid=(d // tn,),
        in_specs=[y_spec, full(ATTN_WIDTH), full(HG_WIDTH), full(CONV_WIDTH), w_spec],
        out_specs=out_specs,
        out_shape=out_shape,
        compiler_params=_cparams("parallel"),
        name="out_proj_sample",
    )(x, a, o, c, w_out)


def _conv_taps(u, prev1, prev2, w_ref):
    return w_ref[0:1, :] * prev2 + w_ref[1:2, :] * prev1 + w_ref[2:3, :] * u


def _out_proj_kernel(tiles_per_seq, x_ref, a_ref, o_ref, cb_ref, cc_ref, ch_ref, cw_ref, w_ref,
                     y_ref, cs_ref, carry_ref):
    i = pl.program_id(0)
    tm = cc_ref.shape[0]

    @pl.when(i % tiles_per_seq == 0)
    def _():
        carry_ref[...] = jnp.zeros_like(carry_ref)

    u = cc_ref[...] * ch_ref[...]
    row = lax.broadcasted_iota(jnp.int32, u.shape, 0)
    last1 = carry_ref[7:8, :]
    last2 = carry_ref[6:7, :]
    um1 = jnp.where(row == 0, last1, pltpu.roll(u, 1, 0))
    um2 = jnp.where(row == 0, last2, jnp.where(row == 1, last1, pltpu.roll(u, 2, 0)))
    c = cb_ref[...] * _conv_taps(u, um1, um2, cw_ref)
    carry_ref[...] = u[tm - 8:tm]

    y_ref[...] = x_ref[...] + _mix_dot(a_ref[...], o_ref[...], c, w_ref)

    @pl.when(i % tiles_per_seq == tiles_per_seq - 1)
    def _():
        cs_ref[0] = carry_ref[8 - (CONV_K - 1):8, :]


def out_proj(x, a, o, proj, conv_w, wb, layer, seq_len, tm):
    m, d = x.shape
    tiles_per_seq = seq_len // tm
    col = lambda blk: pl.BlockSpec((tm, CONV_WIDTH), lambda i: (i, blk))
    return pl.pallas_call(
        functools.partial(_out_proj_kernel, tiles_per_seq),
        grid=(m // tm,),
        in_specs=[
            pl.BlockSpec((tm, d), lambda i: (i, 0)),
            pl.BlockSpec((tm, ATTN_WIDTH), lambda i: (i, 0)),
            pl.BlockSpec((tm, HG_WIDTH), lambda i: (i, 0)),
            col(CB_BLK_512), col(CC_BLK_512), col(CH_BLK_512),
            pl.BlockSpec((None, CONV_K, CONV_WIDTH), lambda i: (layer, 0, 0)),
            pl.BlockSpec(wb.shape, lambda i: (0, 0)),
        ],
        out_specs=[pl.BlockSpec((tm, d), lambda i: (i, 0)),
                   pl.BlockSpec((1, CONV_K - 1, CONV_WIDTH), lambda i: (i // tiles_per_seq, 0, 0))],
        out_shape=[jax.ShapeDtypeStruct((m, d), F32),
                   jax.ShapeDtypeStruct((m // seq_len, CONV_K - 1, CONV_WIDTH), F32)],
        scratch_shapes=[pltpu.VMEM((8, CONV_WIDTH), F32)],
        compiler_params=_cparams("arbitrary"),
        name="out_proj",
    )(x, a, o, proj, proj, proj, conv_w, wb)


def _ffn_step(h_ref, wg, wu, wd, y_ref):
    h = h_ref[...]
    gate = _dot(h, wg)
    up = _dot(h, wu)
    act = _silu(gate) * up
    y_ref[...] += _dot(act.astype(BF16), wd)


def _ffn_begin(x_ref, g_ref, y_ref, h_ref):
    x = x_ref[...]
    h_ref[...] = _rms(x, g_ref[...]).astype(BF16)
    y_ref[...] = x


def _ffn_sample_kernel(last_layer, x_ref, g_ref, wg_ref, wu_ref, wd_ref, gf_ref,
                       y_ref, wgb_ref, wub_ref, wdb_ref, h_ref):
    f = pl.program_id(0)

    @pl.when(f == 0)
    def _():
        _ffn_begin(x_ref, g_ref, y_ref, h_ref)

    wg = wg_ref[...].astype(BF16)
    wu = wu_ref[...].astype(BF16)
    wd = wd_ref[...].astype(BF16)
    wgb_ref[...] = wg
    wub_ref[...] = wu
    wdb_ref[...] = wd
    _ffn_step(h_ref, wg, wu, wd, y_ref)

    if last_layer:
        @pl.when(f == pl.num_programs(0) - 1)
        def _():
            y_ref[...] = _rms(y_ref[...], gf_ref[...])


def ffn_sample(x, g, w_gate_up, w_down, g_final, layer, tf):
    m, d = x.shape
    nf = D_FF // tf
    return pl.pallas_call(
        functools.partial(_ffn_sample_kernel, layer == DEPTH - 1),
        grid=(nf,),
        in_specs=[
            pl.BlockSpec((m, d), lambda f: (0, 0)),
            pl.BlockSpec((None, 1, d), lambda f: (layer, 0, 0)),
            pl.BlockSpec((None, d, tf), lambda f: (layer, 0, f)),
            pl.BlockSpec((None, d, tf), lambda f: (layer, 0, f + nf)),
            pl.BlockSpec((None, tf, d), lambda f: (layer, f, 0)),
            pl.BlockSpec((1, d), lambda f: (0, 0)),
        ],
        out_specs=[
            pl.BlockSpec((m, d), lambda f: (0, 0)),
            pl.BlockSpec((d, tf), lambda f: (0, f)),
            pl.BlockSpec((d, tf), lambda f: (0, f)),
            pl.BlockSpec((tf, d), lambda f: (f, 0)),
        ],
        out_shape=[jax.ShapeDtypeStruct((m, d), F32), jax.ShapeDtypeStruct((d, D_FF), BF16),
                   jax.ShapeDtypeStruct((d, D_FF), BF16), jax.ShapeDtypeStruct((D_FF, d), BF16)],
        scratch_shapes=[pltpu.VMEM((m, d), BF16)],
        compiler_params=_cparams("arbitrary"),
        name="ffn_sample",
    )(x, g, w_gate_up, w_gate_up, w_down, g_final)


def _ffn_kernel(last_layer, x_ref, g_ref, wg_ref, wu_ref, wd_ref, gf_ref, y_ref, h_ref, side=lambda: None):
    f = pl.program_id(1)
    rs = x_ref.shape[0] // NORM_ROW_SPLIT

    @pl.when(f == 0)
    def _():
        side()
        for r in range(NORM_ROW_SPLIT):
            rows = slice(r * rs, (r + 1) * rs)
            x = x_ref[rows, :]
            h = _rms(x, g_ref[...]).astype(BF16)
            h_ref[rows, :] = h
            act = _silu(_dot(h, wg_ref[...])) * _dot(h, wu_ref[...])
            y_ref[rows, :] = x + _dot(act.astype(BF16), wd_ref[...])

    @pl.when(f > 0)
    def _():
        side()
        _ffn_step(h_ref, wg_ref[...], wu_ref[...], wd_ref[...], y_ref)

    if last_layer:
        @pl.when(f == pl.num_programs(1) - 1)
        def _():
            y_ref[...] = _rms(y_ref[...], gf_ref[...])


def _side_cast(w, slab, axis, grid):
    nsteps = grid[0] * grid[1]
    shape = list(w.shape[1:])
    assert shape[axis] % nsteps == 0
    shape[axis] //= nsteps
    pos = lambda i, j: i * grid[1] + j
    idx = (lambda i, j: (pos(i, j), 0)) if axis == 0 else (lambda i, j: (0, pos(i, j)))
    src = pl.BlockSpec((None, *shape), lambda i, j: (slab, *idx(i, j)))
    return src, pl.BlockSpec(tuple(shape), idx), jax.ShapeDtypeStruct(w.shape[1:], BF16)


def _cast_sides(src_refs, dst_refs):
    for src, dst in zip(src_refs, dst_refs):
        dst[...] = src[...].astype(BF16)


FFN_CONVERT_VMEM = 60000 * 1024


def _ffn_convert_kernel(x_ref, g_ref, wg_ref, wu_ref, wd_ref, gf_ref, side_src, y_ref, side_dst, h_ref):
    _ffn_kernel(False, x_ref, g_ref, wg_ref, wu_ref, wd_ref, gf_ref, y_ref, h_ref,
                side=lambda: _cast_sides([side_src], [side_dst]))


def ffn(x, g, w_gate, gate_off, w_up, up_off, wdb, g_final, layer, tm, tf, convert=None):
    m, d = x.shape
    grid = (m // tm, D_FF // tf)
    in_specs = [
        pl.BlockSpec((tm, d), lambda i, f: (i, 0)),
        pl.BlockSpec((None, 1, d), lambda i, f: (layer, 0, 0)),
        pl.BlockSpec((d, tf), lambda i, f: (0, gate_off + f)),
        pl.BlockSpec((d, tf), lambda i, f: (0, up_off + f)),
        pl.BlockSpec((tf, d), lambda i, f: (f, 0)),
        pl.BlockSpec((1, d), lambda i, f: (0, 0)),
    ]
    y_spec = pl.BlockSpec((tm, d), lambda i, f: (i, 0))
    y_shape = jax.ShapeDtypeStruct((m, d), F32)
    if convert is None:
        return pl.pallas_call(
            functools.partial(_ffn_kernel, layer == DEPTH - 1),
            grid=grid, in_specs=in_specs, out_specs=y_spec, out_shape=y_shape,
            scratch_shapes=[pltpu.VMEM((tm, d), BF16)],
            compiler_params=_cparams("parallel", "arbitrary"),
            name="ffn",
        )(x, g, w_gate, w_up, wdb, g_final)

    src, dst, dst_shape = _side_cast(convert, layer + 1, 1, grid)
    return pl.pallas_call(
        _ffn_convert_kernel,
        grid=grid,
        in_specs=in_specs + [src],
        out_specs=[y_spec, dst],
        out_shape=[y_shape, dst_shape],
        scratch_shapes=[pltpu.VMEM((tm, d), BF16)],
        compiler_params=_cparams("parallel", "arbitrary", vmem=FFN_CONVERT_VMEM),
        name="ffn_convert",
    )(x, g, w_gate, w_up, wdb, g_final, convert)


def _rope_tables(pos):
    half = ROT_DIM // 2
    inv = ROPE_THETA ** (-jnp.arange(half, dtype=F32) * 2.0 / ROT_DIM)
    ang = pos.astype(F32)[:, None] * inv[None, :]
    cos, sin = jnp.cos(ang), jnp.sin(ang)
    n = pos.shape[0]
    ones = jnp.ones((n, HEAD_DIM - ROT_DIM), F32)
    zeros = jnp.zeros((n, HEAD_DIM - ROT_DIM), F32)
    zh = jnp.zeros((n, half), F32)
    c = jnp.concatenate([cos, cos, ones], axis=-1)
    s_lo = jnp.concatenate([-sin, zh, zeros], axis=-1)
    s_hi = jnp.concatenate([zh, sin, zeros], axis=-1)
    rep = lambda t: jnp.concatenate([t, t], axis=-1)
    return rep(c), rep(s_lo), rep(s_hi)


def _rope(x, c, s_lo, s_hi):
    half = ROT_DIM // 2
    outs = []
    for j in range(x.shape[1] // 128):
        xc = x[:, j * 128:(j + 1) * 128]
        outs.append(xc * c + pltpu.roll(xc, 128 - half, 1) * s_lo + pltpu.roll(xc, half, 1) * s_hi)
    return outs[0] if len(outs) == 1 else jnp.concatenate(outs, axis=1)


def _rope_tables_t(pos):
    half = ROT_DIM // 2
    inv = ROPE_THETA ** (-jnp.arange(half, dtype=F32) * 2.0 / ROT_DIM)
    ang = inv[:, None] * pos.astype(F32)[None, :]
    return jnp.cos(ang), jnp.sin(ang)


def _rope_t(xt, cos_t, sin_t):
    half = ROT_DIM // 2
    pieces = []
    for base in range(0, xt.shape[0], HEAD_DIM):
        x1 = xt[base:base + half]
        x2 = xt[base + half:base + ROT_DIM]
        pieces += [x1 * cos_t - x2 * sin_t, x2 * cos_t + x1 * sin_t, xt[base + ROT_DIM:base + HEAD_DIM]]
    return jnp.concatenate(pieces, axis=0)


def _softmax_sink_pv(s, sk, v):
    m = jnp.maximum(jnp.max(s, axis=-1, keepdims=True), sk)
    p = jnp.exp(s - m)
    denom = jnp.sum(p, axis=-1, keepdims=True) + jnp.exp(sk - m)
    return _dot(p.astype(BF16), v) / denom


N_ATTN_PROMPT_IN, N_ATTN_PROMPT_OUT = 9, 3


def _attn_prompt_kernel(layer, nside, *refs):
    it = iter(refs)
    take = lambda cnt: [next(it) for _ in range(cnt)]
    sink_ref, q_ref, k_ref, v_ref, c_ref, slo_ref, shi_ref, ct_ref, st_ref = take(N_ATTN_PROMPT_IN)
    side_src = take(nside)
    a_ref, ko_ref, vo_ref = take(N_ATTN_PROMPT_OUT)
    side_dst = take(nside)
    kprev_ref, vtprev_ref = take(2)

    n = pl.program_id(1)
    nstep = pl.num_programs(1)
    blk = WINDOW
    nsub = q_ref.shape[1] // blk
    lanes = Q_PER_KV * blk
    sub = lambda j: slice(j * blk, (j + 1) * blk)

    k_all = _rope(k_ref[0], c_ref[...], slo_ref[...], shi_ref[...])
    v_all = v_ref[0]
    vt_all = v_all.T
    qt_all = (_rope_t(q_ref[0].T, ct_ref[...], st_ref[...]) * ATTN_SCALE).astype(BF16)

    key = lax.broadcasted_iota(jnp.int32, (blk, lanes), 0)
    qry = lax.broadcasted_iota(jnp.int32, (blk, lanes), 1) % blk
    newer = key > qry
    zero_rows = jnp.zeros((HEAD_DIM, lanes), BF16)

    def scores(h, j, first):
        pair = slice((h // 2) * 2 * HEAD_DIM, (h // 2 + 1) * 2 * HEAD_DIM)
        qt = qt_all[:, sub(j)]
        qh = jnp.concatenate(
            [qt[(Q_PER_KV * h + g) * HEAD_DIM:(Q_PER_KV * h + g + 1) * HEAD_DIM] for g in range(Q_PER_KV)], axis=1)
        qz = jnp.concatenate([qh, zero_rows] if h % 2 == 0 else [zero_rows, qh], axis=0)
        s_cur = _dot(k_all[sub(j), pair].astype(BF16), qz)
        if first:
            return None, s_cur
        k_prev = kprev_ref[:, pair] if j == 0 else k_all[sub(j - 1), pair]
        return _dot(k_prev.astype(BF16), qz), s_cur

    def softmax(h, s_prev, s_cur):
        sink = jnp.concatenate(
            [jnp.full((1, blk), sink_ref[layer, Q_PER_KV * h + g], F32) for g in range(Q_PER_KV)], axis=1)
        if s_prev is None:
            f = jnp.where(newer, NEG_BIG, s_cur)
            m = jnp.maximum(jnp.max(f, axis=0, keepdims=True), sink)
            p = jnp.exp(f - m)
            denom = jnp.sum(p, axis=0, keepdims=True) + jnp.exp(sink - m)
            return p.astype(BF16), None, denom
        f = jnp.where(newer, s_prev, s_cur)
        d = jnp.sum(jnp.where(key == qry, s_prev, 0.0), axis=0, keepdims=True)
        m = jnp.maximum(jnp.maximum(jnp.max(f, axis=0, keepdims=True), d), sink)
        p = jnp.exp(f - m)
        pd = jnp.exp(d - m)
        denom = jnp.sum(p, axis=0, keepdims=True) + pd + jnp.exp(sink - m)
        pp = jnp.concatenate([jnp.where(newer, p, 0.0), jnp.where(newer, 0.0, p)], axis=0).astype(BF16)
        return pp, pd, denom

    def weighted_values(h, j, pp, pd, denom):
        rows = slice(h * HEAD_DIM, (h + 1) * HEAD_DIM)
        vt_cur = vt_all[rows, sub(j)]
        inv = 1.0 / denom
        if pd is None:
            return _dot(vt_cur.astype(BF16), pp) * inv
        vt_prev = vtprev_ref[rows, :] if j == 0 else vt_all[rows, sub(j - 1)]
        o = _dot(jnp.concatenate([vt_prev, vt_cur], axis=1).astype(BF16), pp)
        return (o + pd * jnp.concatenate([vt_prev] * Q_PER_KV, axis=1)) * inv

    def run(first_step):
        _cast_sides(side_src, side_dst)
        units = [(h, j) for j in range(nsub) for h in range(N_KV_HEADS)]
        s = {u: scores(*u, first_step and u[1] == 0) for u in units}
        p = {u: softmax(u[0], *s[u]) for u in units}
        o = {u: weighted_values(*u, *p[u]) for u in units}
        for j in range(nsub):
            outs = [o[h, j][:, g * blk:(g + 1) * blk] for h in range(N_KV_HEADS) for g in range(Q_PER_KV)]
            a_ref[0, sub(j), :] = jnp.concatenate(outs, axis=0).T.astype(a_ref.dtype)

    @pl.when(n == 0)
    def _():
        run(True)

    @pl.when(n > 0)
    def _():
        run(False)

    kprev_ref[...] = k_all[sub(nsub - 1)]
    vtprev_ref[...] = vt_all[:, sub(nsub - 1)]

    @pl.when(n == nstep - 1)
    def _():
        ko_ref[0] = k_all[sub(nsub - 1)]
        vo_ref[0] = v_all[sub(nsub - 1)]


def attn_prompt(proj, sink, tables, tables_t, layer, rows, convert=()):
    b, l, _ = proj.shape
    blk = WINDOW
    grid = (b, l // rows)
    tab_spec = pl.BlockSpec((rows, 128), lambda i, n: (n, 0))
    tab_t_spec = pl.BlockSpec((ROT_DIM // 2, rows), lambda i, n: (0, n))
    kv_out = jax.ShapeDtypeStruct((b, blk, KV_WIDTH), F32)
    sides = [_side_cast(w, layer + 1, 0, grid) for w in convert]
    in_specs = [
        pl.BlockSpec(memory_space=pltpu.SMEM),
        pl.BlockSpec((1, rows, ATTN_WIDTH), lambda i, n: (i, n, Q_BLK_1024)),
        pl.BlockSpec((1, rows, KV_WIDTH), lambda i, n: (i, n, K_BLK_256)),
        pl.BlockSpec((1, rows, KV_WIDTH), lambda i, n: (i, n, V_BLK_256)),
        tab_spec, tab_spec, tab_spec, tab_t_spec, tab_t_spec,
    ]
    out_specs = [
        pl.BlockSpec((1, rows, ATTN_WIDTH), lambda i, n: (i, n, 0)),
        pl.BlockSpec((1, blk, KV_WIDTH), lambda i, n: (i, 0, 0)),
        pl.BlockSpec((1, blk, KV_WIDTH), lambda i, n: (i, 0, 0)),
    ]
    assert (len(in_specs), len(out_specs)) == (N_ATTN_PROMPT_IN, N_ATTN_PROMPT_OUT)
    return pl.pallas_call(
        functools.partial(_attn_prompt_kernel, layer, len(sides)),
        grid=grid,
        in_specs=in_specs + [s[0] for s in sides],
        out_specs=out_specs + [s[1] for s in sides],
        out_shape=[jax.ShapeDtypeStruct((b, l, ATTN_WIDTH), BF16), kv_out, kv_out] + [s[2] for s in sides],
        scratch_shapes=[pltpu.VMEM((blk, KV_WIDTH), F32), pltpu.VMEM((KV_WIDTH, blk), F32)],
        compiler_params=_cparams("parallel", "arbitrary"),
        name="attn_prompt",
    )(sink, proj, proj, proj, *tables, *tables_t, *convert)


def _attn_sample_kernel(layer, sink_ref, q_ref, k_ref, v_ref, kc_ref, vc_ref, c_ref, slo_ref, shi_ref,
                        k_carried_ref, v_carried_ref, a_ref, ko_ref, vo_ref):
    del k_carried_ref, v_carried_ref
    r8 = SAMPLE_ROWS
    nseq = r8 // SAMPLE_T
    lb = kc_ref.shape[1]
    c, s_lo, s_hi = c_ref[...], slo_ref[...], shi_ref[...]
    q = _rope(q_ref[...], c, s_lo, s_hi)
    k = _rope(k_ref[...], c, s_lo, s_hi)
    v = v_ref[...]
    pad = jnp.zeros((lb - r8, KV_WIDTH), F32)
    k_pad = jnp.concatenate([k, pad], axis=0)
    v_pad = jnp.concatenate([v, pad], axis=0)

    zeros = jnp.zeros((r8, HEAD_DIM), F32)
    qx = jnp.concatenate(
        [jnp.concatenate([q[:, hg * HEAD_DIM:(hg + 1) * HEAD_DIM] if slot == hg // Q_PER_KV else zeros
                          for slot in range(N_KV_HEADS)], axis=1) for hg in range(N_Q_HEADS)],
        axis=0).astype(BF16)
    sk = jnp.concatenate([jnp.full((r8, 1), sink_ref[layer, hg], F32) for hg in range(N_Q_HEADS)], axis=0)

    rows = N_Q_HEADS * r8
    ri = lax.broadcasted_iota(jnp.int32, (rows, 2 * lb), 0) % r8
    t = ri % SAMPLE_T
    kj = lax.broadcasted_iota(jnp.int32, (rows, 2 * lb), 1)
    cj = kj - lb
    row_seq = lax.broadcasted_iota(jnp.int32, (rows, KV_WIDTH), 0) % r8 // SAMPLE_T

    s = []
    for e in range(nseq):
        kk = jnp.concatenate([kc_ref[e], k_pad], axis=0).astype(BF16)
        s.append(_dot_nt(qx, kk) * ATTN_SCALE)
    o = None
    for e in range(nseq):
        mask = ((kj < lb) & (kj >= t)) | (
            (cj >= e * SAMPLE_T) & (cj < (e + 1) * SAMPLE_T) & (cj - e * SAMPLE_T <= t))
        vv = jnp.concatenate([vc_ref[e], v_pad], axis=0).astype(BF16)
        oe = _softmax_sink_pv(jnp.where(mask, s[e], NEG_BIG), sk, vv)
        o = oe if o is None else jnp.where(row_seq == e, oe, o)
    a_ref[...] = jnp.concatenate(
        [o[hg * r8:(hg + 1) * r8, (hg // Q_PER_KV) * HEAD_DIM:(hg // Q_PER_KV + 1) * HEAD_DIM]
         for hg in range(N_Q_HEADS)], axis=1)

    row = lax.broadcasted_iota(jnp.int32, (lb, KV_WIDTH), 0)
    for e in range(nseq):
        shift = (lb - SAMPLE_T - e * SAMPLE_T) % lb
        for new_pad, cache_ref, out_ref in ((k_pad, kc_ref, ko_ref), (v_pad, vc_ref, vo_ref)):
            new_rows = new_pad if shift == 0 else pltpu.roll(new_pad, shift, 0)
            old_rows = pltpu.roll(cache_ref[e], lb - SAMPLE_T, 0)
            out_ref[e] = jnp.where(row >= lb - SAMPLE_T, new_rows, old_rows)


def _lower_bound(lbr_ref, layer, hs):
    rows = [lbr_ref[r:r + 1, hs] for r in range(DEPTH)]
    mx = functools.reduce(jnp.maximum, rows)
    es = [jnp.exp(r - mx) for r in rows]
    tot = functools.reduce(lambda a, b: a + b, es)
    lbp = [e / tot for e in es]
    acc = lbp[0]
    for r in range(1, layer + 1):
        acc = acc + lbp[r]
    return acc - lbp[0]


def _hgrn_gates(hq, hf, lb):
    q = _silu(hq) * (HG_DK ** -0.5)
    e = jnp.exp(-jnp.abs(hf))
    r = 1.0 / (1.0 + e)
    pos = hf >= 0.0
    sig_pos = jnp.where(pos, r, e * r)
    sig_neg = jnp.where(pos, e * r, r)
    log_f = jnp.log(jnp.maximum(lb, LB_TINY) + (1.0 - lb) * sig_pos)
    k = (1.0 - lb) * sig_neg
    return q, log_f, k


def _hgrn_finish(o, g_norm, hg):
    return _rms(o, g_norm) * _silu(hg)


def _hgrn_levels():
    s = HG_CHUNK // 2
    while s >= 1:
        yield s
        s //= 2


def _hgrn_level_masks():
    t = np.arange(HG_CHUNK)[:, None]
    s = np.arange(HG_CHUNK)[None, :]
    return np.stack([((t % (2 * h) >= h) & (t // (2 * h) == s // (2 * h)) & (s % (2 * h) < h))
                     for h in _hgrn_levels()]).astype(np.float32)


def _hgrn_pair_reference(g_ref, gcum, half, sub8):
    c = HG_CHUNK
    pair = 2 * half
    bcast = lambda r: jnp.broadcast_to(g_ref[r:r + 1, :], (8, HG_DK))
    if pair >= 8:
        return jnp.concatenate([bcast((8 * v // pair) * pair + half - 1) for v in range(c // 8)], axis=0)
    if pair == 4:
        lo = jnp.concatenate([bcast(8 * v + 1) for v in range(c // 8)], axis=0)
        hi = jnp.concatenate([bcast(8 * v + 5) for v in range(c // 8)], axis=0)
        return jnp.where(sub8 < 4, lo, hi)
    return jnp.where(sub8 % 2 == 1, pltpu.roll(gcum, 1, 0), gcum)


def _hgrn_level_operand(q, k, gcum, g_ref, half, row):
    ref = _hgrn_pair_reference(g_ref, gcum, half, row % 8)
    right = row % (2 * half) >= half
    return (jnp.where(right, q, k) * jnp.exp(-jnp.abs(gcum - ref))).astype(BF16)


def _chunk_cumsum(x, buf, row):
    sh = 1
    while sh < HG_CHUNK:
        x = x + jnp.where(row >= sh, pltpu.roll(x, sh, 0), 0.0)
        sh *= 2
    buf[...] = x
    return x


def _hgrn_tile(layer, hq_ref, hf_ref, hi_ref, hg_ref, lbr_ref, gn_ref, m_ref, g_ref, state, emit):
    c = HG_CHUNK
    nchunk = hq_ref.shape[1] // c
    heads = range(HG_HEADS)
    units = [(h, cc) for cc in range(nchunk) for h in heads]
    row = lax.broadcasted_iota(jnp.int32, (c, HG_DK), 0)
    cols = lambda h: slice(h * HG_DK, (h + 1) * HG_DK)
    rws = lambda cc: slice(cc * c, (cc + 1) * c)
    gbuf = lambda h, cc: g_ref.at[h * nchunk + cc]
    lbs = [_lower_bound(lbr_ref, layer, cols(h)) for h in heads]

    q, k, v, gcum = {}, {}, {}, {}
    for u in units:
        h, cc = u
        q[u], log_f, k[u] = _hgrn_gates(hq_ref[0, rws(cc), cols(h)], hf_ref[0, rws(cc), cols(h)], lbs[h])
        v[u] = hi_ref[0, rws(cc), cols(h)]
        gcum[u] = _chunk_cumsum(log_f, gbuf(h, cc), row)

    a = {u: jnp.zeros((c, c), F32) for u in units}
    for lvl, half in enumerate(_hgrn_levels()):
        for u in units:
            x = _hgrn_level_operand(q[u], k[u], gcum[u], gbuf(*u), half, row)
            a[u] = a[u] + _dot_nt(x, x) * m_ref[lvl]

    o_intra, qs, ks, decay = {}, {}, {}, {}
    for u in units:
        vb = v[u].astype(BF16)
        o_intra[u] = _dot(a[u].astype(BF16), vb) + jnp.sum(q[u] * k[u], axis=-1, keepdims=True) * v[u]
        g_last = gbuf(*u)[c - 1:c, :]
        qs[u] = (q[u] * jnp.exp(gcum[u])).astype(BF16)
        ks[u] = (k[u] * jnp.exp(g_last - gcum[u])).astype(BF16)
        decay[u] = jnp.transpose(jnp.broadcast_to(jnp.exp(g_last), (HG_DK, HG_DK)))

    state = list(state)
    for cc in range(nchunk):
        o_inter = [_dot(qs[h, cc], state[h].astype(BF16)) for h in heads]
        update = [_dot_tn(ks[h, cc], v[h, cc].astype(BF16)) for h in heads]
        for h in heads:
            o = o_inter[h] + o_intra[h, cc]
            emit(h, rws(cc), _hgrn_finish(o, gn_ref[h:h + 1, :], hg_ref[0, rws(cc), cols(h)]))
            state[h] = state[h] * decay[h, cc] + update[h]
    return state


N_HGRN_PROMPT_IN, N_HGRN_PROMPT_OUT = 7, 2


def _hgrn_prompt_kernel(layer, nside, *refs):
    it = iter(refs)
    take = lambda cnt: [next(it) for _ in range(cnt)]
    hq_ref, hf_ref, hi_ref, hg_ref, lbr_ref, gn_ref, m_ref = take(N_HGRN_PROMPT_IN)
    side_src = take(nside)
    o_ref, so_ref = take(N_HGRN_PROMPT_OUT)
    side_dst = take(nside)
    s_ref, g_ref = take(2)
    step = pl.program_id(1)
    heads = range(HG_HEADS)

    @pl.when(step == 0)
    def _():
        s_ref[...] = jnp.zeros_like(s_ref)

    def emit(h, rs, o):
        o_ref[0, rs, h * HG_DV:(h + 1) * HG_DV] = o.astype(o_ref.dtype)

    _cast_sides(side_src, side_dst)
    state = _hgrn_tile(layer, hq_ref, hf_ref, hi_ref, hg_ref, lbr_ref, gn_ref, m_ref, g_ref,
                       [s_ref[h] for h in heads], emit)
    for h in heads:
        s_ref[h] = state[h]

    @pl.when(step == pl.num_programs(1) - 1)
    def _():
        so_ref[0] = s_ref[...]


def hgrn_prompt(proj, lb_raw, g_norm, layer, rows, convert=()):
    b, l, _ = proj.shape
    c = HG_CHUNK
    grid = (b, l // rows)
    masks = jnp.asarray(_hgrn_level_masks())
    col = lambda blk: pl.BlockSpec((1, rows, HG_WIDTH), lambda i, n: (i, n, blk))
    sides = [_side_cast(w, layer + 1, 0, grid) for w in convert]
    in_specs = [
        col(HQ_BLK_512), col(HF_BLK_512), col(HI_BLK_512), col(HG_BLK_512),
        pl.BlockSpec((DEPTH, HG_WIDTH), lambda i, n: (0, 0)),
        pl.BlockSpec((None, HG_HEADS, HG_DV), lambda i, n: (layer, 0, 0)),
        pl.BlockSpec(masks.shape, lambda i, n: (0, 0, 0)),
    ]
    out_specs = [
        pl.BlockSpec((1, rows, HG_WIDTH), lambda i, n: (i, n, 0)),
        pl.BlockSpec((1, HG_HEADS, HG_DK, HG_DV), lambda i, n: (i, 0, 0, 0)),
    ]
    assert (len(in_specs), len(out_specs)) == (N_HGRN_PROMPT_IN, N_HGRN_PROMPT_OUT)
    return pl.pallas_call(
        functools.partial(_hgrn_prompt_kernel, layer, len(sides)),
        grid=grid,
        in_specs=in_specs + [s[0] for s in sides],
        out_specs=out_specs + [s[1] for s in sides],
        out_shape=[jax.ShapeDtypeStruct((b, l, HG_WIDTH), BF16),
                   jax.ShapeDtypeStruct((b, HG_HEADS, HG_DK, HG_DV), F32)] + [s[2] for s in sides],
        scratch_shapes=[pltpu.VMEM((HG_HEADS, HG_DK, HG_DV), F32),
                        pltpu.VMEM((HG_HEADS * (rows // c), c, HG_DK), F32)],
        compiler_params=_cparams("parallel", "arbitrary"),
        name="hgrn_prompt",
    )(proj, proj, proj, proj, lb_raw, g_norm, masks, *convert)


def _hgrn_sample_kernel(layer, hq_ref, hf_ref, hi_ref, hg_ref, lbr_ref, gn_ref, s_ref, carried_ref,
                        o_ref, so_ref, q_ref, k_ref, g_ref):
    del carried_ref
    r8, tt = SAMPLE_ROWS, SAMPLE_T
    nseq = r8 // tt
    row = lax.broadcasted_iota(jnp.int32, (r8, HG_DK), 0)
    t_idx = row % tt
    seq = row // tt

    def pick(ref, s):
        out = ref[s:s + 1, :]
        for e in range(1, nseq):
            out = jnp.where(seq == e, ref[e * tt + s:e * tt + s + 1, :], out)
        return out

    for h in range(HG_HEADS):
        hs = slice(h * HG_DK, (h + 1) * HG_DK)
        lb = _lower_bound(lbr_ref, layer, hs)
        q, log_f, k = _hgrn_gates(hq_ref[:, hs], hf_ref[:, hs], lb)
        gcum = log_f
        sh = 1
        while sh < tt:
            gcum = gcum + jnp.where(t_idx >= sh, pltpu.roll(gcum, sh, 0), 0.0)
            sh *= 2
        q_ref[...] = q
        k_ref[...] = k
        g_ref[...] = gcum
        v = hi_ref[:, hs]
        g_last = pick(g_ref, tt - 1)
        qs = (q * jnp.exp(gcum)).astype(BF16)
        ks_all = k * jnp.exp(g_last - gcum)

        o = jnp.zeros((r8, HG_DV), F32)
        for e in range(nseq):
            state = s_ref[e, h]
            o = jnp.where(seq == e, _dot(qs, state.astype(BF16)), o)
            g_last_e = g_ref[e * tt + tt - 1:e * tt + tt, :]
            decay = jnp.transpose(jnp.broadcast_to(jnp.exp(g_last_e), (HG_DK, HG_DK)))
            ks_e = jnp.where(seq == e, ks_all, 0.0).astype(BF16)
            so_ref[e, h] = state * decay + _dot_tn(ks_e, v.astype(BF16))
        for s in range(tt):
            dec = jnp.exp(jnp.where(t_idx >= s, gcum - pick(g_ref, s), NEG_BIG))
            col = jnp.sum(q * (dec * pick(k_ref, s)), axis=-1, keepdims=True)
            o = o + col * pick(hi_ref.at[:, hs], s)
        o_ref[:, hs] = _hgrn_finish(o, gn_ref[h:h + 1, :], hg_ref[:, hs])


def _conv_sample_kernel(cb_ref, cc_ref, ch_ref, w_ref, st_ref, y_ref, so_ref, u_ref):
    r8, tt = SAMPLE_ROWS, SAMPLE_T
    nseq = r8 // tt
    u = cc_ref[...] * ch_ref[...]
    u_ref[...] = u
    row = lax.broadcasted_iota(jnp.int32, u.shape, 0)
    t_idx = row % tt
    seq = row // tt
    last1 = st_ref[0, 1:2, :]
    last2 = st_ref[0, 0:1, :]
    for e in range(1, nseq):
        last1 = jnp.where(seq == e, st_ref[e, 1:2, :], last1)
        last2 = jnp.where(seq == e, st_ref[e, 0:1, :], last2)
    um1 = jnp.where(t_idx == 0, last1, pltpu.roll(u, 1, 0))
    um2 = jnp.where(t_idx == 0, last2, jnp.where(t_idx == 1, last1, pltpu.roll(u, 2, 0)))
    y_ref[...] = cb_ref[...] * _conv_taps(u, um1, um2, w_ref)
    for e in range(nseq):
        so_ref[e] = u_ref[(e + 1) * tt - (CONV_K - 1):(e + 1) * tt, :]


N_ATTN_SAMPLE_IN, N_HGRN_SAMPLE_IN, N_CONV_SAMPLE_IN = 11, 8, 5
N_ATTN_SAMPLE_OUT, N_HGRN_SAMPLE_OUT, N_CONV_SAMPLE_OUT = 3, 2, 2


def _mix_sample_kernel(layer, *refs):
    it = iter(refs)
    take = lambda n: [next(it) for _ in range(n)]
    attn_in, hgrn_in, conv_in = take(N_ATTN_SAMPLE_IN), take(N_HGRN_SAMPLE_IN), take(N_CONV_SAMPLE_IN)
    attn_out, hgrn_out, conv_out = take(N_ATTN_SAMPLE_OUT), take(N_HGRN_SAMPLE_OUT), take(N_CONV_SAMPLE_OUT)
    hgrn_scratch, conv_scratch = take(3), take(1)
    _attn_sample_kernel(layer, *attn_in, *attn_out)
    _hgrn_sample_kernel(layer, *hgrn_in, *hgrn_out, *hgrn_scratch)
    _conv_sample_kernel(*conv_in, *conv_out, *conv_scratch)


def mix_sample(proj, k_cache, v_cache, k_new, v_new, sink, tables, state, state_new, lb_raw, g_norm,
               conv_state, conv_w, layer):
    m = proj.shape[0]
    r8 = SAMPLE_ROWS
    nseq = r8 // SAMPLE_T
    lb = k_cache.shape[2]
    rows = lambda width, blk: pl.BlockSpec((r8, width), lambda i: (i, blk))
    tab_spec = pl.BlockSpec((r8, 128), lambda i: (0, 0))
    cache_spec = pl.BlockSpec((None, nseq, lb, KV_WIDTH), lambda i: (layer, i, 0, 0))
    st_spec = pl.BlockSpec((None, nseq, HG_HEADS, HG_DK, HG_DV), lambda i: (layer, i, 0, 0, 0))
    carried = pl.BlockSpec(memory_space=pl.ANY)
    attn_in = [pl.BlockSpec(memory_space=pltpu.SMEM),
               rows(ATTN_WIDTH, Q_BLK_1024), rows(KV_WIDTH, K_BLK_256), rows(KV_WIDTH, V_BLK_256),
               cache_spec, cache_spec, tab_spec, tab_spec, tab_spec, carried, carried]
    hgrn_in = [rows(HG_WIDTH, HQ_BLK_512), rows(HG_WIDTH, HF_BLK_512), rows(HG_WIDTH, HI_BLK_512),
               rows(HG_WIDTH, HG_BLK_512),
               pl.BlockSpec((DEPTH, HG_WIDTH), lambda i: (0, 0)),
               pl.BlockSpec((None, HG_HEADS, HG_DV), lambda i: (layer, 0, 0)),
               st_spec, carried]
    conv_in = [rows(CONV_WIDTH, CB_BLK_512), rows(CONV_WIDTH, CC_BLK_512), rows(CONV_WIDTH, CH_BLK_512),
               pl.BlockSpec((None, CONV_K, CONV_WIDTH), lambda i: (layer, 0, 0)),
               pl.BlockSpec((None, nseq, CONV_K - 1, CONV_WIDTH), lambda i: (layer, i, 0, 0))]
    assert (len(attn_in), len(hgrn_in), len(conv_in)) == (N_ATTN_SAMPLE_IN, N_HGRN_SAMPLE_IN, N_CONV_SAMPLE_IN)
    out_specs = [rows(ATTN_WIDTH, 0), cache_spec, cache_spec,
                 rows(HG_WIDTH, 0), st_spec,
                 rows(CONV_WIDTH, 0), pl.BlockSpec((nseq, CONV_K - 1, CONV_WIDTH), lambda i: (i, 0, 0))]
    out_shape = [jax.ShapeDtypeStruct((m, ATTN_WIDTH), F32),
                 jax.ShapeDtypeStruct(k_new.shape, F32), jax.ShapeDtypeStruct(v_new.shape, F32),
                 jax.ShapeDtypeStruct((m, HG_WIDTH), F32), jax.ShapeDtypeStruct(state_new.shape, F32),
                 jax.ShapeDtypeStruct((m, CONV_WIDTH), F32), jax.ShapeDtypeStruct(conv_state.shape[1:], F32)]
    return pl.pallas_call(
        functools.partial(_mix_sample_kernel, layer),
        grid=(m // r8,),
        in_specs=attn_in + hgrn_in + conv_in,
        out_specs=out_specs,
        out_shape=out_shape,
        input_output_aliases={9: 1, 10: 2, N_ATTN_SAMPLE_IN + N_HGRN_SAMPLE_IN - 1: 4},
        scratch_shapes=[pltpu.VMEM((r8, HG_DK), F32), pltpu.VMEM((r8, HG_DK), F32), pltpu.VMEM((r8, HG_DK), F32),
                        pltpu.VMEM((r8, CONV_WIDTH), F32)],
        compiler_params=_cparams("parallel"),
        name="mix_sample",
    )(sink, proj, proj, proj, k_cache, v_cache, *tables, k_new, v_new,
      proj, proj, proj, proj, lb_raw, g_norm, state, state_new,
      proj, proj, proj, conv_w, conv_state)


def kernel(x_prompt, x_sample, cache_attn_k, cache_attn_v, state_hgrn, state_conv, w_in, attn_sink,
           hgrn_lower_bounds, hgrn_norm, conv_w, w_out, norm_mix, norm_ffn, w_gate_up, w_down, norm_final):
    bp, lp, d = x_prompt.shape
    bs, ls, _ = x_sample.shape
    assert ls == SAMPLE_T and (bs * ls) % SAMPLE_ROWS == 0
    lb = cache_attn_k.shape[2]
    mp, ms = bp * lp, bs * ls

    pos_p = jnp.arange(lp, dtype=jnp.int32)
    tab_p = _rope_tables(pos_p)
    tab_pt = _rope_tables_t(pos_p)
    pos_s = PAST_LEN + jnp.arange(SAMPLE_ROWS, dtype=jnp.int32) % SAMPLE_T
    tab_s = _rope_tables(pos_s)

    xp = x_prompt.reshape(mp, d)
    xs = x_sample.reshape(ms, d)
    kc = cache_attn_k.reshape(DEPTH, bs, lb, KV_WIDTH)
    vc = cache_attn_v.reshape(DEPTH, bs, lb, KV_WIDTH)

    g_mix = norm_mix.reshape(DEPTH, 1, d)
    g_ffn = norm_ffn.reshape(DEPTH, 1, d)
    g_final = norm_final.reshape(1, d)

    k_s = jnp.zeros(kc.shape, F32)
    v_s = jnp.zeros(vc.shape, F32)
    s_s = jnp.zeros(state_hgrn.shape, F32)

    tf = 512
    outs = {name: [] for name in ("kp", "vp", "sp", "cp", "cs")}
    for l in range(DEPTH):
        if l == 0:
            proj_s, w_in_b = in_proj_sample(xs, g_mix, w_in, l, 1024)
        else:
            proj_s = in_proj(xs, g_mix, w_in_b, l, ms, 1024)
        a_s, k_s, v_s, o_s, s_s, c_s, cst_s = mix_sample(
            proj_s, kc, vc, k_s, v_s, attn_sink, tab_s, state_hgrn, s_s, hgrn_lower_bounds, hgrn_norm,
            state_conv, conv_w, l)
        if l == 0:
            xs, w_out_b = out_proj_sample(xs, a_s, o_s, c_s, w_out, l, 512)
            xs, w_gate_b, w_up_b, w_down_b = ffn_sample(xs, g_ffn, w_gate_up, w_down, g_final, l, tf)
            gate_up = (w_gate_b, 0, w_up_b, 0)
        else:
            xs = out_proj_sample(xs, a_s, o_s, c_s, w_out_b, l, 512)
            xs = ffn(xs, g_ffn, *gate_up, w_down_b, g_final, l, ms, tf)

        more = l < DEPTH - 1
        in_tm, in_tn = ((1024, 1024), (1024, 512), (512, 2560), (1024, 1280))[l]
        proj_p = in_proj(xp, g_mix, w_in_b, l, in_tm, in_tn)
        proj_p3 = proj_p.reshape(bp, lp, IN_WIDTH)
        a_p, k_p, v_p, *cast_a = attn_prompt(proj_p3, attn_sink, tab_p, tab_pt, l, (4, 4, 4, 8)[l] * WINDOW,
                                             convert=(w_in, w_out) if more else ())
        o_p, s_p, *cast_h = hgrn_prompt(proj_p3, hgrn_lower_bounds, hgrn_norm, l, 8 * HG_CHUNK,
                                        convert=(w_down,) if more else ())
        xp, cst_p = out_proj(xp, a_p.reshape(mp, ATTN_WIDTH), o_p.reshape(mp, HG_WIDTH), proj_p,
                             conv_w, w_out_b, l, lp, (512, 256, 512, 256)[l])
        if more:
            xp, w_gu_b = ffn(xp, g_ffn, *gate_up, w_down_b, g_final, l, 1024, tf, convert=w_gate_up)
            (w_in_b, w_out_b), (w_down_b,) = cast_a, cast_h
            gate_up = (w_gu_b, 0, w_gu_b, D_FF // tf)
        else:
            xp = ffn(xp, g_ffn, *gate_up, w_down_b, g_final, l, 1024, tf)

        outs["kp"].append(k_p.reshape(bp, WINDOW, N_KV_HEADS, HEAD_DIM))
        outs["vp"].append(v_p.reshape(bp, WINDOW, N_KV_HEADS, HEAD_DIM))
        outs["sp"].append(s_p)
        outs["cp"].append(cst_p)
        outs["cs"].append(cst_s)

    st = lambda name: jnp.stack(outs[name])
    kv_shape = (DEPTH, bs, lb, N_KV_HEADS, HEAD_DIM)
    return (xp.reshape(bp, lp, d), xs.reshape(bs, ls, d), st("kp"), st("vp"), st("sp"), st("cp"),
            k_s.reshape(kv_shape), v_s.reshape(kv_shape), s_s, st("cs"))
```

```python
import functools

import jax
import jax.numpy as jnp
import numpy as np
from jax import lax
from jax.experimental import pallas as pl
from jax.experimental.pallas import tpu as pltpu

F32 = jnp.float32
BF16 = jnp.bfloat16

D_MODEL = 2048
DEPTH = 4
PAST_LEN = 16384
HEAD_DIM = 64
N_Q_HEADS = 16
N_KV_HEADS = 4
Q_PER_KV = 4
ATTN_WIDTH = 1024
KV_WIDTH = 256
WINDOW = 128
ROPE_THETA = 500000.0
ROT_DIM = 16
ATTN_SCALE = HEAD_DIM ** -0.5
NEG_BIG = -1e30
HG_HEADS = 4
HG_DK = 128
HG_DV = 128
HG_WIDTH = 512
HG_CHUNK = 64
LB_TINY = 1e-30
CONV_K = 3
CONV_WIDTH = 512
D_FF = 5632
IN_WIDTH = 5120
NORM_EPS = 1e-6
SAMPLE_T = 4
SAMPLE_ROWS = 8

Q_BLK_1024 = 0
K_BLK_256 = 4
V_BLK_256 = 5
HQ_BLK_512, HF_BLK_512, HI_BLK_512, HG_BLK_512 = 3, 4, 5, 6
CB_BLK_512, CC_BLK_512, CH_BLK_512 = 7, 8, 9

VMEM_LIMIT = 56 * 1024 * 1024


def _cparams(*sem, vmem=VMEM_LIMIT):
    return pltpu.CompilerParams(dimension_semantics=sem, vmem_limit_bytes=vmem)


def _rms(x, g):
    ms = jnp.mean(x * x, axis=-1, keepdims=True)
    return x * lax.rsqrt(ms + NORM_EPS) * g


def _silu(x):
    return x * (1.0 / (1.0 + jnp.exp(-x)))


def _dot(a, b):
    return jnp.dot(a, b, preferred_element_type=F32)


def _dot_nt(a, b):
    return lax.dot_general(a, b, (((1,), (1,)), ((), ())), preferred_element_type=F32)


def _dot_tn(a, b):
    return lax.dot_general(a, b, (((0,), (0,)), ((), ())), preferred_element_type=F32)


def _in_proj_sample_kernel(x_ref, g_ref, w_ref, o_ref, wb_ref, h_ref):
    @pl.when(pl.program_id(0) == 0)
    def _():
        h_ref[...] = _rms(x_ref[...], g_ref[...]).astype(BF16)

    wb = w_ref[...].astype(BF16)
    wb_ref[...] = wb
    o_ref[...] = _dot(h_ref[...], wb)


def in_proj_sample(x, g, w, layer, tn):
    m, k = x.shape
    n = w.shape[2]
    return pl.pallas_call(
        _in_proj_sample_kernel,
        grid=(n // tn,),
        in_specs=[
            pl.BlockSpec((m, k), lambda j: (0, 0)),
            pl.BlockSpec((None, 1, k), lambda j: (layer, 0, 0)),
            pl.BlockSpec((None, k, tn), lambda j: (layer, 0, j)),
        ],
        out_specs=[pl.BlockSpec((m, tn), lambda j: (0, j)), pl.BlockSpec((k, tn), lambda j: (0, j))],
        out_shape=[jax.ShapeDtypeStruct((m, n), F32), jax.ShapeDtypeStruct((k, n), BF16)],
        scratch_shapes=[pltpu.VMEM((m, k), BF16)],
        compiler_params=_cparams("arbitrary"),
        name="in_proj_sample",
    )(x, g, w)


NORM_ROW_SPLIT = 4


def _in_proj_kernel(x_ref, g_ref, w_ref, o_ref, h_ref):
    j = pl.program_id(1)
    rs = x_ref.shape[0] // NORM_ROW_SPLIT

    @pl.when(j == 0)
    def _():
        for r in range(NORM_ROW_SPLIT):
            rows = slice(r * rs, (r + 1) * rs)
            h = _rms(x_ref[rows, :], g_ref[...]).astype(BF16)
            h_ref[rows, :] = h
            o_ref[rows, :] = _dot(h, w_ref[...])

    @pl.when(j > 0)
    def _():
        o_ref[...] = _dot(h_ref[...], w_ref[...])


def in_proj(x, g, wb, layer, tm, tn):
    m, k = x.shape
    n = wb.shape[1]
    return pl.pallas_call(
        _in_proj_kernel,
        grid=(m // tm, n // tn),
        in_specs=[
            pl.BlockSpec((tm, k), lambda i, j: (i, 0)),
            pl.BlockSpec((None, 1, k), lambda i, j: (layer, 0, 0)),
            pl.BlockSpec((k, tn), lambda i, j: (0, j)),
        ],
        out_specs=pl.BlockSpec((tm, tn), lambda i, j: (i, j)),
        out_shape=jax.ShapeDtypeStruct((m, n), F32),
        scratch_shapes=[pltpu.VMEM((tm, k), BF16)],
        compiler_params=_cparams("parallel", "arbitrary", vmem=FFN_CONVERT_VMEM),
        name="in_proj",
    )(x, g, wb)


def _mix_dot(a, o, c, w):
    acc = _dot(a.astype(BF16), w[0:ATTN_WIDTH])
    acc = acc + _dot(o.astype(BF16), w[ATTN_WIDTH:ATTN_WIDTH + HG_WIDTH])
    return acc + _dot(c.astype(BF16), w[ATTN_WIDTH + HG_WIDTH:])


def _out_proj_sample_kernel(x_ref, a_ref, o_ref, c_ref, w_ref, y_ref, *wb_ref):
    w = w_ref[...]
    if wb_ref:
        w = w.astype(BF16)
        wb_ref[0][...] = w
    y_ref[...] = x_ref[...] + _mix_dot(a_ref[...], o_ref[...], c_ref[...], w)


def out_proj_sample(x, a, o, c, w_out, layer, tn):
    m, d = x.shape
    convert = w_out.ndim == 3
    kw = w_out.shape[-2]
    full = lambda width: pl.BlockSpec((m, width), lambda j: (0, 0))
    y_spec = pl.BlockSpec((m, tn), lambda j: (0, j))
    y_shape = jax.ShapeDtypeStruct((m, d), F32)
    if convert:
        w_spec = pl.BlockSpec((None, kw, tn), lambda j: (layer, 0, j))
        out_specs = [y_spec, pl.BlockSpec((kw, tn), lambda j: (0, j))]
        out_shape = [y_shape, jax.ShapeDtypeStruct((kw, d), BF16)]
    else:
        w_spec = pl.BlockSpec((kw, tn), lambda j: (0, j))
        out_specs, out_shape = y_spec, y_shape
    return pl.pallas_call(
        _out_proj_sample_kernel,
        grid=(d // tn,),
        in_specs=[y_spec, full(ATTN_WIDTH), full(HG_WIDTH), full(CONV_WIDTH), w_spec],
        out_specs=out_specs,
        out_shape=out_shape,
        compiler_params=_cparams("parallel"),
        name="out_proj_sample",
    )(x, a, o, c, w_out)


def _conv_taps(u, prev1, prev2, w_ref):
    return w_ref[0:1, :] * prev2 + w_ref[1:2, :] * prev1 + w_ref[2:3, :] * u


def _out_proj_kernel(tiles_per_seq, x_ref, a_ref, o_ref, cb_ref, cc_ref, ch_ref, cw_ref, w_ref,
                     y_ref, cs_ref, carry_ref):
    i = pl.program_id(0)
    tm = cc_ref.shape[0]

    @pl.when(i % tiles_per_seq == 0)
    def _():
        carry_ref[...] = jnp.zeros_like(carry_ref)

    u = cc_ref[...] * ch_ref[...]
    row = lax.broadcasted_iota(jnp.int32, u.shape, 0)
    last1 = carry_ref[7:8, :]
    last2 = carry_ref[6:7, :]
    um1 = jnp.where(row == 0, last1, pltpu.roll(u, 1, 0))
    um2 = jnp.where(row == 0, last2, jnp.where(row == 1, last1, pltpu.roll(u, 2, 0)))
    c = cb_ref[...] * _conv_taps(u, um1, um2, cw_ref)
    carry_ref[...] = u[tm - 8:tm]

    y_ref[...] = x_ref[...] + _mix_dot(a_ref[...], o_ref[...], c, w_ref)

    @pl.when(i % tiles_per_seq == tiles_per_seq - 1)
    def _():
        cs_ref[0] = carry_ref[8 - (CONV_K - 1):8, :]


def out_proj(x, a, o, proj, conv_w, wb, layer, seq_len, tm):
    m, d = x.shape
    tiles_per_seq = seq_len // tm
    col = lambda blk: pl.BlockSpec((tm, CONV_WIDTH), lambda i: (i, blk))
    return pl.pallas_call(
        functools.partial(_out_proj_kernel, tiles_per_seq),
        grid=(m // tm,),
        in_specs=[
            pl.BlockSpec((tm, d), lambda i: (i, 0)),
            pl.BlockSpec((tm, ATTN_WIDTH), lambda i: (i, 0)),
            pl.BlockSpec((tm, HG_WIDTH), lambda i: (i, 0)),
            col(CB_BLK_512), col(CC_BLK_512), col(CH_BLK_512),
            pl.BlockSpec((None, CONV_K, CONV_WIDTH), lambda i: (layer, 0, 0)),
            pl.BlockSpec(wb.shape, lambda i: (0, 0)),
        ],
        out_specs=[pl.BlockSpec((tm, d), lambda i: (i, 0)),
                   pl.BlockSpec((1, CONV_K - 1, CONV_WIDTH), lambda i: (i // tiles_per_seq, 0, 0))],
        out_shape=[jax.ShapeDtypeStruct((m, d), F32),
                   jax.ShapeDtypeStruct((m // seq_len, CONV_K - 1, CONV_WIDTH), F32)],
        scratch_shapes=[pltpu.VMEM((8, CONV_WIDTH), F32)],
        compiler_params=_cparams("arbitrary"),
        name="out_proj",
    )(x, a, o, proj, proj, proj, conv_w, wb)


def _ffn_step(h_ref, wg, wu, wd, y_ref):
    h = h_ref[...]
    gate = _dot(h, wg)
    up = _dot(h, wu)
    act = _silu(gate) * up
    y_ref[...] += _dot(act.astype(BF16), wd)


def _ffn_begin(x_ref, g_ref, y_ref, h_ref):
    x = x_ref[...]
    h_ref[...] = _rms(x, g_ref[...]).astype(BF16)
    y_ref[...] = x


def _ffn_sample_kernel(last_layer, x_ref, g_ref, wg_ref, wu_ref, wd_ref, gf_ref,
                       y_ref, wgb_ref, wub_ref, wdb_ref, h_ref):
    f = pl.program_id(0)

    @pl.when(f == 0)
    def _():
        _ffn_begin(x_ref, g_ref, y_ref, h_ref)

    wg = wg_ref[...].astype(BF16)
    wu = wu_ref[...].astype(BF16)
    wd = wd_ref[...].astype(BF16)
    wgb_ref[...] = wg
    wub_ref[...] = wu
    wdb_ref[...] = wd
    _ffn_step(h_ref, wg, wu, wd, y_ref)

    if last_layer:
        @pl.when(f == pl.num_programs(0) - 1)
        def _():
            y_ref[...] = _rms(y_ref[...], gf_ref[...])


def ffn_sample(x, g, w_gate_up, w_down, g_final, layer, tf):
    m, d = x.shape
    nf = D_FF // tf
    return pl.pallas_call(
        functools.partial(_ffn_sample_kernel, layer == DEPTH - 1),
        grid=(nf,),
        in_specs=[
            pl.BlockSpec((m, d), lambda f: (0, 0)),
            pl.BlockSpec((None, 1, d), lambda f: (layer, 0, 0)),
            pl.BlockSpec((None, d, tf), lambda f: (layer, 0, f)),
            pl.BlockSpec((None, d, tf), lambda f: (layer, 0, f + nf)),
            pl.BlockSpec((None, tf, d), lambda f: (layer, f, 0)),
            pl.BlockSpec((1, d), lambda f: (0, 0)),
        ],
        out_specs=[
            pl.BlockSpec((m, d), lambda f: (0, 0)),
            pl.BlockSpec((d, tf), lambda f: (0, f)),
            pl.BlockSpec((d, tf), lambda f: (0, f)),
            pl.BlockSpec((tf, d), lambda f: (f, 0)),
        ],
        out_shape=[jax.ShapeDtypeStruct((m, d), F32), jax.ShapeDtypeStruct((d, D_FF), BF16),
                   jax.ShapeDtypeStruct((d, D_FF), BF16), jax.ShapeDtypeStruct((D_FF, d), BF16)],
        scratch_shapes=[pltpu.VMEM((m, d), BF16)],
        compiler_params=_cparams("arbitrary"),
        name="ffn_sample",
    )(x, g, w_gate_up, w_gate_up, w_down, g_final)


def _ffn_kernel(last_layer, x_ref, g_ref, wg_ref, wu_ref, wd_ref, gf_ref, y_ref, h_ref, side=lambda: None):
    f = pl.program_id(1)
    rs = x_ref.shape[0] // NORM_ROW_SPLIT

    @pl.when(f == 0)
    def _():
        side()
        for r in range(NORM_ROW_SPLIT):
            rows = slice(r * rs, (r + 1) * rs)
            x = x_ref[rows, :]
            h = _rms(x, g_ref[...]).astype(BF16)
            h_ref[rows, :] = h
            act = _silu(_dot(h, wg_ref[...])) * _dot(h, wu_ref[...])
            y_ref[rows, :] = x + _dot(act.astype(BF16), wd_ref[...])

    @pl.when(f > 0)
    def _():
        side()
        _ffn_step(h_ref, wg_ref[...], wu_ref[...], wd_ref[...], y_ref)

    if last_layer:
        @pl.when(f == pl.num_programs(1) - 1)
        def _():
            y_ref[...] = _rms(y_ref[...], gf_ref[...])


def _side_cast(w, slab, axis, grid):
    nsteps = grid[0] * grid[1]
    shape = list(w.shape[1:])
    assert shape[axis] % nsteps == 0
    shape[axis] //= nsteps
    pos = lambda i, j: i * grid[1] + j
    idx = (lambda i, j: (pos(i, j), 0)) if axis == 0 else (lambda i, j: (0, pos(i, j)))
    src = pl.BlockSpec((None, *shape), lambda i, j: (slab, *idx(i, j)))
    return src, pl.BlockSpec(tuple(shape), idx), jax.ShapeDtypeStruct(w.shape[1:], BF16)


def _cast_sides(src_refs, dst_refs):
    for src, dst in zip(src_refs, dst_refs):
        dst[...] = src[...].astype(BF16)


FFN_CONVERT_VMEM = 60000 * 1024


def _ffn_convert_kernel(x_ref, g_ref, wg_ref, wu_ref, wd_ref, gf_ref, side_src, y_ref, side_dst, h_ref):
    _ffn_kernel(False, x_ref, g_ref, wg_ref, wu_ref, wd_ref, gf_ref, y_ref, h_ref,
                side=lambda: _cast_sides([side_src], [side_dst]))


def ffn(x, g, w_gate, gate_off, w_up, up_off, wdb, g_final, layer, tm, tf, convert=None):
    m, d = x.shape
    grid = (m // tm, D_FF // tf)
    in_specs = [
        pl.BlockSpec((tm, d), lambda i, f: (i, 0)),
        pl.BlockSpec((None, 1, d), lambda i, f: (layer, 0, 0)),
        pl.BlockSpec((d, tf), lambda i, f: (0, gate_off + f)),
        pl.BlockSpec((d, tf), lambda i, f: (0, up_off + f)),
        pl.BlockSpec((tf, d), lambda i, f: (f, 0)),
        pl.BlockSpec((1, d), lambda i, f: (0, 0)),
    ]
    y_spec = pl.BlockSpec((tm, d), lambda i, f: (i, 0))
    y_shape = jax.ShapeDtypeStruct((m, d), F32)
    if convert is None:
        return pl.pallas_call(
            functools.partial(_ffn_kernel, layer == DEPTH - 1),
            grid=grid, in_specs=in_specs, out_specs=y_spec, out_shape=y_shape,
            scratch_shapes=[pltpu.VMEM((tm, d), BF16)],
            compiler_params=_cparams("parallel", "arbitrary"),
            name="ffn",
        )(x, g, w_gate, w_up, wdb, g_final)

    src, dst, dst_shape = _side_cast(convert, layer + 1, 1, grid)
    return pl.pallas_call(
        _ffn_convert_kernel,
        grid=grid,
        in_specs=in_specs + [src],
        out_specs=[y_spec, dst],
        out_shape=[y_shape, dst_shape],
        scratch_shapes=[pltpu.VMEM((tm, d), BF16)],
        compiler_params=_cparams("parallel", "arbitrary", vmem=FFN_CONVERT_VMEM),
        name="ffn_convert",
    )(x, g, w_gate, w_up, wdb, g_final, convert)


def _rope_tables(pos):
    half = ROT_DIM // 2
    inv = ROPE_THETA ** (-jnp.arange(half, dtype=F32) * 2.0 / ROT_DIM)
    ang = pos.astype(F32)[:, None] * inv[None, :]
    cos, sin = jnp.cos(ang), jnp.sin(ang)
    n = pos.shape[0]
    ones = jnp.ones((n, HEAD_DIM - ROT_DIM), F32)
    zeros = jnp.zeros((n, HEAD_DIM - ROT_DIM), F32)
    zh = jnp.zeros((n, half), F32)
    c = jnp.concatenate([cos, cos, ones], axis=-1)
    s_lo = jnp.concatenate([-sin, zh, zeros], axis=-1)
    s_hi = jnp.concatenate([zh, sin, zeros], axis=-1)
    rep = lambda t: jnp.concatenate([t, t], axis=-1)
    return rep(c), rep(s_lo), rep(s_hi)


def _rope(x, c, s_lo, s_hi):
    half = ROT_DIM // 2
    outs = []
    for j in range(x.shape[1] // 128):
        xc = x[:, j * 128:(j + 1) * 128]
        outs.append(xc * c + pltpu.roll(xc, 128 - half, 1) * s_lo + pltpu.roll(xc, half, 1) * s_hi)
    return outs[0] if len(outs) == 1 else jnp.concatenate(outs, axis=1)


def _rope_tables_t(pos):
    half = ROT_DIM // 2
    inv = ROPE_THETA ** (-jnp.arange(half, dtype=F32) * 2.0 / ROT_DIM)
    ang = inv[:, None] * pos.astype(F32)[None, :]
    return jnp.cos(ang), jnp.sin(ang)


def _rope_t(xt, cos_t, sin_t):
    half = ROT_DIM // 2
    pieces = []
    for base in range(0, xt.shape[0], HEAD_DIM):
        x1 = xt[base:base + half]
        x2 = xt[base + half:base + ROT_DIM]
        pieces += [x1 * cos_t - x2 * sin_t, x2 * cos_t + x1 * sin_t, xt[base + ROT_DIM:base + HEAD_DIM]]
    return jnp.concatenate(pieces, axis=0)


def _softmax_sink_pv(s, sk, v):
    m = jnp.maximum(jnp.max(s, axis=-1, keepdims=True), sk)
    p = jnp.exp(s - m)
    denom = jnp.sum(p, axis=-1, keepdims=True) + jnp.exp(sk - m)
    return _dot(p.astype(BF16), v) / denom


N_ATTN_PROMPT_IN, N_ATTN_PROMPT_OUT = 9, 3


def _attn_prompt_kernel(layer, nside, *refs):
    it = iter(refs)
    take = lambda cnt: [next(it) for _ in range(cnt)]
    sink_ref, q_ref, k_ref, v_ref, c_ref, slo_ref, shi_ref, ct_ref, st_ref = take(N_ATTN_PROMPT_IN)
    side_src = take(nside)
    a_ref, ko_ref, vo_ref = take(N_ATTN_PROMPT_OUT)
    side_dst = take(nside)
    kprev_ref, vtprev_ref = take(2)

    n = pl.program_id(1)
    nstep = pl.num_programs(1)
    blk = WINDOW
    nsub = q_ref.shape[1] // blk
    lanes = Q_PER_KV * blk
    sub = lambda j: slice(j * blk, (j + 1) * blk)

    k_all = _rope(k_ref[0], c_ref[...], slo_ref[...], shi_ref[...])
    v_all = v_ref[0]
    vt_all = v_all.T
    qt_all = (_rope_t(q_ref[0].T, ct_ref[...], st_ref[...]) * ATTN_SCALE).astype(BF16)

    key = lax.broadcasted_iota(jnp.int32, (blk, lanes), 0)
    qry = lax.broadcasted_iota(jnp.int32, (blk, lanes), 1) % blk
    newer = key > qry
    zero_rows = jnp.zeros((HEAD_DIM, lanes), BF16)

    def scores(h, j, first):
        pair = slice((h // 2) * 2 * HEAD_DIM, (h // 2 + 1) * 2 * HEAD_DIM)
        qt = qt_all[:, sub(j)]
        qh = jnp.concatenate(
            [qt[(Q_PER_KV * h + g) * HEAD_DIM:(Q_PER_KV * h + g + 1) * HEAD_DIM] for g in range(Q_PER_KV)], axis=1)
        qz = jnp.concatenate([qh, zero_rows] if h % 2 == 0 else [zero_rows, qh], axis=0)
        s_cur = _dot(k_all[sub(j), pair].astype(BF16), qz)
        if first:
            return None, s_cur
        k_prev = kprev_ref[:, pair] if j == 0 else k_all[sub(j - 1), pair]
        return _dot(k_prev.astype(BF16), qz), s_cur

    def softmax(h, s_prev, s_cur):
        sink = jnp.concatenate(
            [jnp.full((1, blk), sink_ref[layer, Q_PER_KV * h + g], F32) for g in range(Q_PER_KV)], axis=1)
        if s_prev is None:
            f = jnp.where(newer, NEG_BIG, s_cur)
            m = jnp.maximum(jnp.max(f, axis=0, keepdims=True), sink)
            p = jnp.exp(f - m)
            denom = jnp.sum(p, axis=0, keepdims=True) + jnp.exp(sink - m)
            return p.astype(BF16), None, denom
        f = jnp.where(newer, s_prev, s_cur)
        d = jnp.sum(jnp.where(key == qry, s_prev, 0.0), axis=0, keepdims=True)
        m = jnp.maximum(jnp.maximum(jnp.max(f, axis=0, keepdims=True), d), sink)
        p = jnp.exp(f - m)
        pd = jnp.exp(d - m)
        denom = jnp.sum(p, axis=0, keepdims=True) + pd + jnp.exp(sink - m)
        pp = jnp.concatenate([jnp.where(newer, p, 0.0), jnp.where(newer, 0.0, p)], axis=0).astype(BF16)
        return pp, pd, denom

    def weighted_values(h, j, pp, pd, denom):
        rows = slice(h * HEAD_DIM, (h + 1) * HEAD_DIM)
        vt_cur = vt_all[rows, sub(j)]
        inv = 1.0 / denom
        if pd is None:
            return _dot(vt_cur.astype(BF16), pp) * inv
        vt_prev = vtprev_ref[rows, :] if j == 0 else vt_all[rows, sub(j - 1)]
        o = _dot(jnp.concatenate([vt_prev, vt_cur], axis=1).astype(BF16), pp)
        return (o + pd * jnp.concatenate([vt_prev] * Q_PER_KV, axis=1)) * inv

    def run(first_step):
        _cast_sides(side_src, side_dst)
        units = [(h, j) for j in range(nsub) for h in range(N_KV_HEADS)]
        s = {u: scores(*u, first_step and u[1] == 0) for u in units}
        p = {u: softmax(u[0], *s[u]) for u in units}
        o = {u: weighted_values(*u, *p[u]) for u in units}
        for j in range(nsub):
            outs = [o[h, j][:, g * blk:(g + 1) * blk] for h in range(N_KV_HEADS) for g in range(Q_PER_KV)]
            a_ref[0, sub(j), :] = jnp.concatenate(outs, axis=0).T.astype(a_ref.dtype)

    @pl.when(n == 0)
    def _():
        run(True)

    @pl.when(n > 0)
    def _():
        run(False)

    kprev_ref[...] = k_all[sub(nsub - 1)]
    vtprev_ref[...] = vt_all[:, sub(nsub - 1)]

    @pl.when(n == nstep - 1)
    def _():
        ko_ref[0] = k_all[sub(nsub - 1)]
        vo_ref[0] = v_all[sub(nsub - 1)]


def attn_prompt(proj, sink, tables, tables_t, layer, rows, convert=()):
    b, l, _ = proj.shape
    blk = WINDOW
    grid = (b, l // rows)
    tab_spec = pl.BlockSpec((rows, 128), lambda i, n: (n, 0))
    tab_t_spec = pl.BlockSpec((ROT_DIM // 2, rows), lambda i, n: (0, n))
    kv_out = jax.ShapeDtypeStruct((b, blk, KV_WIDTH), F32)
    sides = [_side_cast(w, layer + 1, 0, grid) for w in convert]
    in_specs = [
        pl.BlockSpec(memory_space=pltpu.SMEM),
        pl.BlockSpec((1, rows, ATTN_WIDTH), lambda i, n: (i, n, Q_BLK_1024)),
        pl.BlockSpec((1, rows, KV_WIDTH), lambda i, n: (i, n, K_BLK_256)),
        pl.BlockSpec((1, rows, KV_WIDTH), lambda i, n: (i, n, V_BLK_256)),
        tab_spec, tab_spec, tab_spec, tab_t_spec, tab_t_spec,
    ]
    out_specs = [
        pl.BlockSpec((1, rows, ATTN_WIDTH), lambda i, n: (i, n, 0)),
        pl.BlockSpec((1, blk, KV_WIDTH), lambda i, n: (i, 0, 0)),
        pl.BlockSpec((1, blk, KV_WIDTH), lambda i, n: (i, 0, 0)),
    ]
    assert (len(in_specs), len(out_specs)) == (N_ATTN_PROMPT_IN, N_ATTN_PROMPT_OUT)
    return pl.pallas_call(
        functools.partial(_attn_prompt_kernel, layer, len(sides)),
        grid=grid,
        in_specs=in_specs + [s[0] for s in sides],
        out_specs=out_specs + [s[1] for s in sides],
        out_shape=[jax.ShapeDtypeStruct((b, l, ATTN_WIDTH), BF16), kv_out, kv_out] + [s[2] for s in sides],
        scratch_shapes=[pltpu.VMEM((blk, KV_WIDTH), F32), pltpu.VMEM((KV_WIDTH, blk), F32)],
        compiler_params=_cparams("parallel", "arbitrary"),
        name="attn_prompt",
    )(sink, proj, proj, proj, *tables, *tables_t, *convert)


def _attn_sample_kernel(layer, sink_ref, q_ref, k_ref, v_ref, kc_ref, vc_ref, c_ref, slo_ref, shi_ref,
                        k_carried_ref, v_carried_ref, a_ref, ko_ref, vo_ref):
    del k_carried_ref, v_carried_ref
    r8 = SAMPLE_ROWS
    nseq = r8 // SAMPLE_T
    lb = kc_ref.shape[1]
    c, s_lo, s_hi = c_ref[...], slo_ref[...], shi_ref[...]
    q = _rope(q_ref[...], c, s_lo, s_hi)
    k = _rope(k_ref[...], c, s_lo, s_hi)
    v = v_ref[...]
    pad = jnp.zeros((lb - r8, KV_WIDTH), F32)
    k_pad = jnp.concatenate([k, pad], axis=0)
    v_pad = jnp.concatenate([v, pad], axis=0)

    zeros = jnp.zeros((r8, HEAD_DIM), F32)
    qx = jnp.concatenate(
        [jnp.concatenate([q[:, hg * HEAD_DIM:(hg + 1) * HEAD_DIM] if slot == hg // Q_PER_KV else zeros
                          for slot in range(N_KV_HEADS)], axis=1) for hg in range(N_Q_HEADS)],
        axis=0).astype(BF16)
    sk = jnp.concatenate([jnp.full((r8, 1), sink_ref[layer, hg], F32) for hg in range(N_Q_HEADS)], axis=0)

    rows = N_Q_HEADS * r8
    ri = lax.broadcasted_iota(jnp.int32, (rows, 2 * lb), 0) % r8
    t = ri % SAMPLE_T
    kj = lax.broadcasted_iota(jnp.int32, (rows, 2 * lb), 1)
    cj = kj - lb
    row_seq = lax.broadcasted_iota(jnp.int32, (rows, KV_WIDTH), 0) % r8 // SAMPLE_T

    s = []
    for e in range(nseq):
        kk = jnp.concatenate([kc_ref[e], k_pad], axis=0).astype(BF16)
        s.append(_dot_nt(qx, kk) * ATTN_SCALE)
    o = None
    for e in range(nseq):
        mask = ((kj < lb) & (kj >= t)) | (
            (cj >= e * SAMPLE_T) & (cj < (e + 1) * SAMPLE_T) & (cj - e * SAMPLE_T <= t))
        vv = jnp.concatenate([vc_ref[e], v_pad], axis=0).astype(BF16)
        oe = _softmax_sink_pv(jnp.where(mask, s[e], NEG_BIG), sk, vv)
        o = oe if o is None else jnp.where(row_seq == e, oe, o)
    a_ref[...] = jnp.concatenate(
        [o[hg * r8:(hg + 1) * r8, (hg // Q_PER_KV) * HEAD_DIM:(hg // Q_PER_KV + 1) * HEAD_DIM]
         for hg in range(N_Q_HEADS)], axis=1)

    row = lax.broadcasted_iota(jnp.int32, (lb, KV_WIDTH), 0)
    for e in range(nseq):
        shift = (lb - SAMPLE_T - e * SAMPLE_T) % lb
        for new_pad, cache_ref, out_ref in ((k_pad, kc_ref, ko_ref), (v_pad, vc_ref, vo_ref)):
            new_rows = new_pad if shift == 0 else pltpu.roll(new_pad, shift, 0)
            old_rows = pltpu.roll(cache_ref[e], lb - SAMPLE_T, 0)
            out_ref[e] = jnp.where(row >= lb - SAMPLE_T, new_rows, old_rows)


def _lower_bound(lbr_ref, layer, hs):
    rows = [lbr_ref[r:r + 1, hs] for r in range(DEPTH)]
    mx = functools.reduce(jnp.maximum, rows)
    es = [jnp.exp(r - mx) for r in rows]
    tot = functools.reduce(lambda a, b: a + b, es)
    lbp = [e / tot for e in es]
    acc = lbp[0]
    for r in range(1, layer + 1):
        acc = acc + lbp[r]
    return acc - lbp[0]


def _hgrn_gates(hq, hf, lb):
    q = _silu(hq) * (HG_DK ** -0.5)
    e = jnp.exp(-jnp.abs(hf))
    r = 1.0 / (1.0 + e)
    pos = hf >= 0.0
    sig_pos = jnp.where(pos, r, e * r)
    sig_neg = jnp.where(pos, e * r, r)
    log_f = jnp.log(jnp.maximum(lb, LB_TINY) + (1.0 - lb) * sig_pos)
    k = (1.0 - lb) * sig_neg
    return q, log_f, k


def _hgrn_finish(o, g_norm, hg):
    return _rms(o, g_norm) * _silu(hg)


def _hgrn_levels():
    s = HG_CHUNK // 2
    while s >= 1:
        yield s
        s //= 2


def _hgrn_level_masks():
    t = np.arange(HG_CHUNK)[:, None]
    s = np.arange(HG_CHUNK)[None, :]
    return np.stack([((t % (2 * h) >= h) & (t // (2 * h) == s // (2 * h)) & (s % (2 * h) < h))
                     for h in _hgrn_levels()]).astype(np.float32)


def _hgrn_pair_reference(g_ref, gcum, half, sub8):
    c = HG_CHUNK
    pair = 2 * half
    bcast = lambda r: jnp.broadcast_to(g_ref[r:r + 1, :], (8, HG_DK))
    if pair >= 8:
        return jnp.concatenate([bcast((8 * v // pair) * pair + half - 1) for v in range(c // 8)], axis=0)
    if pair == 4:
        lo = jnp.concatenate([bcast(8 * v + 1) for v in range(c // 8)], axis=0)
        hi = jnp.concatenate([bcast(8 * v + 5) for v in range(c // 8)], axis=0)
        return jnp.where(sub8 < 4, lo, hi)
    return jnp.where(sub8 % 2 == 1, pltpu.roll(gcum, 1, 0), gcum)


def _hgrn_level_operand(q, k, gcum, g_ref, half, row):
    ref = _hgrn_pair_reference(g_ref, gcum, half, row % 8)
    right = row % (2 * half) >= half
    return (jnp.where(right, q, k) * jnp.exp(-jnp.abs(gcum - ref))).astype(BF16)


def _chunk_cumsum(x, buf, row):
    sh = 1
    while sh < HG_CHUNK:
        x = x + jnp.where(row >= sh, pltpu.roll(x, sh, 0), 0.0)
        sh *= 2
    buf[...] = x
    return x


def _hgrn_tile(layer, hq_ref, hf_ref, hi_ref, hg_ref, lbr_ref, gn_ref, m_ref, g_ref, state, emit):
    c = HG_CHUNK
    nchunk = hq_ref.shape[1] // c
    heads = range(HG_HEADS)
    units = [(h, cc) for cc in range(nchunk) for h in heads]
    row = lax.broadcasted_iota(jnp.int32, (c, HG_DK), 0)
    cols = lambda h: slice(h * HG_DK, (h + 1) * HG_DK)
    rws = lambda cc: slice(cc * c, (cc + 1) * c)
    gbuf = lambda h, cc: g_ref.at[h * nchunk + cc]
    lbs = [_lower_bound(lbr_ref, layer, cols(h)) for h in heads]

    q, k, v, gcum = {}, {}, {}, {}
    for u in units:
        h, cc = u
        q[u], log_f, k[u] = _hgrn_gates(hq_ref[0, rws(cc), cols(h)], hf_ref[0, rws(cc), cols(h)], lbs[h])
        v[u] = hi_ref[0, rws(cc), cols(h)]
        gcum[u] = _chunk_cumsum(log_f, gbuf(h, cc), row)

    a = {u: jnp.zeros((c, c), F32) for u in units}
    for lvl, half in enumerate(_hgrn_levels()):
        for u in units:
            x = _hgrn_level_operand(q[u], k[u], gcum[u], gbuf(*u), half, row)
            a[u] = a[u] + _dot_nt(x, x) * m_ref[lvl]

    o_intra, qs, ks, decay = {}, {}, {}, {}
    for u in units:
        vb = v[u].astype(BF16)
        o_intra[u] = _dot(a[u].astype(BF16), vb) + jnp.sum(q[u] * k[u], axis=-1, keepdims=True) * v[u]
        g_last = gbuf(*u)[c - 1:c, :]
        qs[u] = (q[u] * jnp.exp(gcum[u])).astype(BF16)
        ks[u] = (k[u] * jnp.exp(g_last - gcum[u])).astype(BF16)
        decay[u] = jnp.transpose(jnp.broadcast_to(jnp.exp(g_last), (HG_DK, HG_DK)))

    state = list(state)
    for cc in range(nchunk):
        o_inter = [_dot(qs[h, cc], state[h].astype(BF16)) for h in heads]
        update = [_dot_tn(ks[h, cc], v[h, cc].astype(BF16)) for h in heads]
        for h in heads:
            o = o_inter[h] + o_intra[h, cc]
            emit(h, rws(cc), _hgrn_finish(o, gn_ref[h:h + 1, :], hg_ref[0, rws(cc), cols(h)]))
            state[h] = state[h] * decay[h, cc] + update[h]
    return state


N_HGRN_PROMPT_IN, N_HGRN_PROMPT_OUT = 7, 2


def _hgrn_prompt_kernel(layer, nside, *refs):
    it = iter(refs)
    take = lambda cnt: [next(it) for _ in range(cnt)]
    hq_ref, hf_ref, hi_ref, hg_ref, lbr_ref, gn_ref, m_ref = take(N_HGRN_PROMPT_IN)
    side_src = take(nside)
    o_ref, so_ref = take(N_HGRN_PROMPT_OUT)
    side_dst = take(nside)
    s_ref, g_ref = take(2)
    step = pl.program_id(1)
    heads = range(HG_HEADS)

    @pl.when(step == 0)
    def _():
        s_ref[...] = jnp.zeros_like(s_ref)

    def emit(h, rs, o):
        o_ref[0, rs, h * HG_DV:(h + 1) * HG_DV] = o.astype(o_ref.dtype)

    _cast_sides(side_src, side_dst)
    state = _hgrn_tile(layer, hq_ref, hf_ref, hi_ref, hg_ref, lbr_ref, gn_ref, m_ref, g_ref,
                       [s_ref[h] for h in heads], emit)
    for h in heads:
        s_ref[h] = state[h]

    @pl.when(step == pl.num_programs(1) - 1)
    def _():
        so_ref[0] = s_ref[...]


def hgrn_prompt(proj, lb_raw, g_norm, layer, rows, convert=()):
    b, l, _ = proj.shape
    c = HG_CHUNK
    grid = (b, l // rows)
    masks = jnp.asarray(_hgrn_level_masks())
    col = lambda blk: pl.BlockSpec((1, rows, HG_WIDTH), lambda i, n: (i, n, blk))
    sides = [_side_cast(w, layer + 1, 0, grid) for w in convert]
    in_specs = [
        col(HQ_BLK_512), col(HF_BLK_512), col(HI_BLK_512), col(HG_BLK_512),
        pl.BlockSpec((DEPTH, HG_WIDTH), lambda i, n: (0, 0)),
        pl.BlockSpec((None, HG_HEADS, HG_DV), lambda i, n: (layer, 0, 0)),
        pl.BlockSpec(masks.shape, lambda i, n: (0, 0, 0)),
    ]
    out_specs = [
        pl.BlockSpec((1, rows, HG_WIDTH), lambda i, n: (i, n, 0)),
        pl.BlockSpec((1, HG_HEADS, HG_DK, HG_DV), lambda i, n: (i, 0, 0, 0)),
    ]
    assert (len(in_specs), len(out_specs)) == (N_HGRN_PROMPT_IN, N_HGRN_PROMPT_OUT)
    return pl.pallas_call(
        functools.partial(_hgrn_prompt_kernel, layer, len(sides)),
        grid=grid,
        in_specs=in_specs + [s[0] for s in sides],
        out_specs=out_specs + [s[1] for s in sides],
        out_shape=[jax.ShapeDtypeStruct((b, l, HG_WIDTH), BF16),
                   jax.ShapeDtypeStruct((b, HG_HEADS, HG_DK, HG_DV), F32)] + [s[2] for s in sides],
        scratch_shapes=[pltpu.VMEM((HG_HEADS, HG_DK, HG_DV), F32),
                        pltpu.VMEM((HG_HEADS * (rows // c), c, HG_DK), F32)],
        compiler_params=_cparams("parallel", "arbitrary"),
        name="hgrn_prompt",
    )(proj, proj, proj, proj, lb_raw, g_norm, masks, *convert)


def _hgrn_sample_kernel(layer, hq_ref, hf_ref, hi_ref, hg_ref, lbr_ref, gn_ref, s_ref, carried_ref,
                        o_ref, so_ref, q_ref, k_ref, g_ref):
    del carried_ref
    r8, tt = SAMPLE_ROWS, SAMPLE_T
    nseq = r8 // tt
    row = lax.broadcasted_iota(jnp.int32, (r8, HG_DK), 0)
    t_idx = row % tt
    seq = row // tt

    def pick(ref, s):
        out = ref[s:s + 1, :]
        for e in range(1, nseq):
            out = jnp.where(seq == e, ref[e * tt + s:e * tt + s + 1, :], out)
        return out

    for h in range(HG_HEADS):
        hs = slice(h * HG_DK, (h + 1) * HG_DK)
        lb = _lower_bound(lbr_ref, layer, hs)
        q, log_f, k = _hgrn_gates(hq_ref[:, hs], hf_ref[:, hs], lb)
        gcum = log_f
        sh = 1
        while sh < tt:
            gcum = gcum + jnp.where(t_idx >= sh, pltpu.roll(gcum, sh, 0), 0.0)
            sh *= 2
        q_ref[...] = q
        k_ref[...] = k
        g_ref[...] = gcum
        v = hi_ref[:, hs]
        g_last = pick(g_ref, tt - 1)
        qs = (q * jnp.exp(gcum)).astype(BF16)
        ks_all = k * jnp.exp(g_last - gcum)

        o = jnp.zeros((r8, HG_DV), F32)
        for e in range(nseq):
            state = s_ref[e, h]
            o = jnp.where(seq == e, _dot(qs, state.astype(BF16)), o)
            g_last_e = g_ref[e * tt + tt - 1:e * tt + tt, :]
            decay = jnp.transpose(jnp.broadcast_to(jnp.exp(g_last_e), (HG_DK, HG_DK)))
            ks_e = jnp.where(seq == e, ks_all, 0.0).astype(BF16)
            so_ref[e, h] = state * decay + _dot_tn(ks_e, v.astype(BF16))
        for s in range(tt):
            dec = jnp.exp(jnp.where(t_idx >= s, gcum - pick(g_ref, s), NEG_BIG))
            col = jnp.sum(q * (dec * pick(k_ref, s)), axis=-1, keepdims=True)
            o = o + col * pick(hi_ref.at[:, hs], s)
        o_ref[:, hs] = _hgrn_finish(o, gn_ref[h:h + 1, :], hg_ref[:, hs])


def _conv_sample_kernel(cb_ref, cc_ref, ch_ref, w_ref, st_ref, y_ref, so_ref, u_ref):
    r8, tt = SAMPLE_ROWS, SAMPLE_T
    nseq = r8 // tt
    u = cc_ref[...] * ch_ref[...]
    u_ref[...] = u
    row = lax.broadcasted_iota(jnp.int32, u.shape, 0)
    t_idx = row % tt
    seq = row // tt
    last1 = st_ref[0, 1:2, :]
    last2 = st_ref[0, 0:1, :]
    for e in range(1, nseq):
        last1 = jnp.where(seq == e, st_ref[e, 1:2, :], last1)
        last2 = jnp.where(seq == e, st_ref[e, 0:1, :], last2)
    um1 = jnp.where(t_idx == 0, last1, pltpu.roll(u, 1, 0))
    um2 = jnp.where(t_idx == 0, last2, jnp.where(t_idx == 1, last1, pltpu.roll(u, 2, 0)))
    y_ref[...] = cb_ref[...] * _conv_taps(u, um1, um2, w_ref)
    for e in range(nseq):
        so_ref[e] = u_ref[(e + 1) * tt - (CONV_K - 1):(e + 1) * tt, :]


N_ATTN_SAMPLE_IN, N_HGRN_SAMPLE_IN, N_CONV_SAMPLE_IN = 11, 8, 5
N_ATTN_SAMPLE_OUT, N_HGRN_SAMPLE_OUT, N_CONV_SAMPLE_OUT = 3, 2, 2


def _mix_sample_kernel(layer, *refs):
    it = iter(refs)
    take = lambda n: [next(it) for _ in range(n)]
    attn_in, hgrn_in, conv_in = take(N_ATTN_SAMPLE_IN), take(N_HGRN_SAMPLE_IN), take(N_CONV_SAMPLE_IN)
    attn_out, hgrn_out, conv_out = take(N_ATTN_SAMPLE_OUT), take(N_HGRN_SAMPLE_OUT), take(N_CONV_SAMPLE_OUT)
    hgrn_scratch, conv_scratch = take(3), take(1)
    _attn_sample_kernel(layer, *attn_in, *attn_out)
    _hgrn_sample_kernel(layer, *hgrn_in, *hgrn_out, *hgrn_scratch)
    _conv_sample_kernel(*conv_in, *conv_out, *conv_scratch)


def mix_sample(proj, k_cache, v_cache, k_new, v_new, sink, tables, state, state_new, lb_raw, g_norm,
               conv_state, conv_w, layer):
    m = proj.shape[0]
    r8 = SAMPLE_ROWS
    nseq = r8 // SAMPLE_T
    lb = k_cache.shape[2]
    rows = lambda width, blk: pl.BlockSpec((r8, width), lambda i: (i, blk))
    tab_spec = pl.BlockSpec((r8, 128), lambda i: (0, 0))
    cache_spec = pl.BlockSpec((None, nseq, lb, KV_WIDTH), lambda i: (layer, i, 0, 0))
    st_spec = pl.BlockSpec((None, nseq, HG_HEADS, HG_DK, HG_DV), lambda i: (layer, i, 0, 0, 0))
    carried = pl.BlockSpec(memory_space=pl.ANY)
    attn_in = [pl.BlockSpec(memory_space=pltpu.SMEM),
               rows(ATTN_WIDTH, Q_BLK_1024), rows(KV_WIDTH, K_BLK_256), rows(KV_WIDTH, V_BLK_256),
               cache_spec, cache_spec, tab_spec, tab_spec, tab_spec, carried, carried]
    hgrn_in = [rows(HG_WIDTH, HQ_BLK_512), rows(HG_WIDTH, HF_BLK_512), rows(HG_WIDTH, HI_BLK_512),
               rows(HG_WIDTH, HG_BLK_512),
               pl.BlockSpec((DEPTH, HG_WIDTH), lambda i: (0, 0)),
               pl.BlockSpec((None, HG_HEADS, HG_DV), lambda i: (layer, 0, 0)),
               st_spec, carried]
    conv_in = [rows(CONV_WIDTH, CB_BLK_512), rows(CONV_WIDTH, CC_BLK_512), rows(CONV_WIDTH, CH_BLK_512),
               pl.BlockSpec((None, CONV_K, CONV_WIDTH), lambda i: (layer, 0, 0)),
               pl.BlockSpec((None, nseq, CONV_K - 1, CONV_WIDTH), lambda i: (layer, i, 0, 0))]
    assert (len(attn_in), len(hgrn_in), len(conv_in)) == (N_ATTN_SAMPLE_IN, N_HGRN_SAMPLE_IN, N_CONV_SAMPLE_IN)
    out_specs = [rows(ATTN_WIDTH, 0), cache_spec, cache_spec,
                 rows(HG_WIDTH, 0), st_spec,
                 rows(CONV_WIDTH, 0), pl.BlockSpec((nseq, CONV_K - 1, CONV_WIDTH), lambda i: (i, 0, 0))]
    out_shape = [jax.ShapeDtypeStruct((m, ATTN_WIDTH), F32),
                 jax.ShapeDtypeStruct(k_new.shape, F32), jax.ShapeDtypeStruct(v_new.shape, F32),
                 jax.ShapeDtypeStruct((m, HG_WIDTH), F32), jax.ShapeDtypeStruct(state_new.shape, F32),
                 jax.ShapeDtypeStruct((m, CONV_WIDTH), F32), jax.ShapeDtypeStruct(conv_state.shape[1:], F32)]
    return pl.pallas_call(
        functools.partial(_mix_sample_kernel, layer),
        grid=(m // r8,),
        in_specs=attn_in + hgrn_in + conv_in,
        out_specs=out_specs,
        out_shape=out_shape,
        input_output_aliases={9: 1, 10: 2, N_ATTN_SAMPLE_IN + N_HGRN_SAMPLE_IN - 1: 4},
        scratch_shapes=[pltpu.VMEM((r8, HG_DK), F32), pltpu.VMEM((r8, HG_DK), F32), pltpu.VMEM((r8, HG_DK), F32),
                        pltpu.VMEM((r8, CONV_WIDTH), F32)],
        compiler_params=_cparams("parallel"),
        name="mix_sample",
    )(sink, proj, proj, proj, k_cache, v_cache, *tables, k_new, v_new,
      proj, proj, proj, proj, lb_raw, g_norm, state, state_new,
      proj, proj, proj, conv_w, conv_state)


def kernel(x_prompt, x_sample, cache_attn_k, cache_attn_v, state_hgrn, state_conv, w_in, attn_sink,
           hgrn_lower_bounds, hgrn_norm, conv_w, w_out, norm_mix, norm_ffn, w_gate_up, w_down, norm_final):
    bp, lp, d = x_prompt.shape
    bs, ls, _ = x_sample.shape
    assert ls == SAMPLE_T and (bs * ls) % SAMPLE_ROWS == 0
    lb = cache_attn_k.shape[2]
    mp, ms = bp * lp, bs * ls

    pos_p = jnp.arange(lp, dtype=jnp.int32)
    tab_p = _rope_tables(pos_p)
    tab_pt = _rope_tables_t(pos_p)
    pos_s = PAST_LEN + jnp.arange(SAMPLE_ROWS, dtype=jnp.int32) % SAMPLE_T
    tab_s = _rope_tables(pos_s)

    xp = x_prompt.reshape(mp, d)
    xs = x_sample.reshape(ms, d)
    kc = cache_attn_k.reshape(DEPTH, bs, lb, KV_WIDTH)
    vc = cache_attn_v.reshape(DEPTH, bs, lb, KV_WIDTH)

    g_mix = norm_mix.reshape(DEPTH, 1, d)
    g_ffn = norm_ffn.reshape(DEPTH, 1, d)
    g_final = norm_final.reshape(1, d)

    k_s = jnp.zeros(kc.shape, F32)
    v_s = jnp.zeros(vc.shape, F32)
    s_s = jnp.zeros(state_hgrn.shape, F32)

    in_tiles, out_tm, ffn_tiles = (1024, 1280), 512, (1024, 512)
    attn_rows, attn_rows_last, hgrn_rows = 4 * WINDOW, 8 * WINDOW, 8 * HG_CHUNK
    s_in_tn_f32, s_in_tn, s_out_tn_f32, s_out_tn, s_ffn_tf_f32, s_ffn_tf = 1024, 2560, 512, 1024, 512, 1408

    def gate_up_blocks(w_gate, w_up, tf):
        return (w_gate, 0, w_up, 0 if w_up is not w_gate else D_FF // tf)

    outs = {name: [] for name in ("kp", "vp", "sp", "cp", "cs")}
    for l in range(DEPTH):
        if l == 0:
            proj_s, w_in_b = in_proj_sample(xs, g_mix, w_in, l, s_in_tn_f32)
        else:
            proj_s = in_proj(xs, g_mix, w_in_b, l, ms, s_in_tn)
        a_s, k_s, v_s, o_s, s_s, c_s, cst_s = mix_sample(
            proj_s, kc, vc, k_s, v_s, attn_sink, tab_s, state_hgrn, s_s, hgrn_lower_bounds, hgrn_norm,
            state_conv, conv_w, l)
        if l == 0:
            xs, w_out_b = out_proj_sample(xs, a_s, o_s, c_s, w_out, l, s_out_tn_f32)
            xs, w_gate_b, w_up_b, w_down_b = ffn_sample(xs, g_ffn, w_gate_up, w_down, g_final, l, s_ffn_tf_f32)
        else:
            xs = out_proj_sample(xs, a_s, o_s, c_s, w_out_b, l, s_out_tn)
            xs = ffn(xs, g_ffn, *gate_up_blocks(w_gate_b, w_up_b, s_ffn_tf), w_down_b, g_final, l, ms, s_ffn_tf)

        more = l < DEPTH - 1
        proj_p = in_proj(xp, g_mix, w_in_b, l, *in_tiles)
        proj_p3 = proj_p.reshape(bp, lp, IN_WIDTH)
        a_p, k_p, v_p, *cast_a = attn_prompt(proj_p3, attn_sink, tab_p, tab_pt, l,
                                             attn_rows if more else attn_rows_last,
                                             convert=(w_in, w_out) if more else ())
        o_p, s_p, *cast_h = hgrn_prompt(proj_p3, hgrn_lower_bounds, hgrn_norm, l, hgrn_rows,
                                        convert=(w_down,) if more else ())
        xp, cst_p = out_proj(xp, a_p.reshape(mp, ATTN_WIDTH), o_p.reshape(mp, HG_WIDTH), proj_p,
                             conv_w, w_out_b, l, lp, out_tm)
        gate_up = gate_up_blocks(w_gate_b, w_up_b, ffn_tiles[1])
        if more:
            xp, w_gu_b = ffn(xp, g_ffn, *gate_up, w_down_b, g_final, l, *ffn_tiles, convert=w_gate_up)
            (w_in_b, w_out_b), (w_down_b,) = cast_a, cast_h
            w_gate_b = w_up_b = w_gu_b
        else:
            xp = ffn(xp, g_ffn, *gate_up, w_down_b, g_final, l, *ffn_tiles)

        outs["kp"].append(k_p.reshape(bp, WINDOW, N_KV_HEADS, HEAD_DIM))
        outs["vp"].append(v_p.reshape(bp, WINDOW, N_KV_HEADS, HEAD_DIM))
        outs["sp"].append(s_p)
        outs["cp"].append(cst_p)
        outs["cs"].append(cst_s)

    st = lambda name: jnp.stack(outs[name])
    kv_shape = (DEPTH, bs, lb, N_KV_HEADS, HEAD_DIM)
    return (xp.reshape(bp, lp, d), xs.reshape(bs, ls, d), st("kp"), st("vp"), st("sp"), st("cp"),
            k_s.reshape(kv_shape), v_s.reshape(kv_shape), s_s, st("cs"))
```

```python
import functools

import jax
import jax.numpy as jnp
import numpy as np
from jax import lax
from jax.experimental import pallas as pl
from jax.experimental.pallas import tpu as pltpu

F32 = jnp.float32
BF16 = jnp.bfloat16

D_MODEL = 2048
DEPTH = 4
PAST_LEN = 16384
HEAD_DIM = 64
N_Q_HEADS = 16
N_KV_HEADS = 4
Q_PER_KV = 4
ATTN_WIDTH = 1024
KV_WIDTH = 256
WINDOW = 128
ROPE_THETA = 500000.0
ROT_DIM = 16
ATTN_SCALE = HEAD_DIM ** -0.5
NEG_BIG = -1e30
HG_HEADS = 4
HG_DK = 128
HG_DV = 128
HG_WIDTH = 512
HG_CHUNK = 64
LB_TINY = 1e-30
CONV_K = 3
CONV_WIDTH = 512
D_FF = 5632
IN_WIDTH = 5120
NORM_EPS = 1e-6
SAMPLE_T = 4
SAMPLE_ROWS = 8

Q_BLK_1024 = 0
K_BLK_256 = 4
V_BLK_256 = 5
HQ_BLK_512, HF_BLK_512, HI_BLK_512, HG_BLK_512 = 3, 4, 5, 6
CB_BLK_512, CC_BLK_512, CH_BLK_512 = 7, 8, 9

VMEM_LIMIT = 56 * 1024 * 1024


def _cparams(*sem, vmem=VMEM_LIMIT):
    return pltpu.CompilerParams(dimension_semantics=sem, vmem_limit_bytes=vmem)


def _rms(x, g):
    ms = jnp.mean(x * x, axis=-1, keepdims=True)
    return x * lax.rsqrt(ms + NORM_EPS) * g


def _silu(x):
    return x * (1.0 / (1.0 + jnp.exp(-x)))


def _dot(a, b):
    return jnp.dot(a, b, preferred_element_type=F32)


def _dot_nt(a, b):
    return lax.dot_general(a, b, (((1,), (1,)), ((), ())), preferred_element_type=F32)


def _dot_tn(a, b):
    return lax.dot_general(a, b, (((0,), (0,)), ((), ())), preferred_element_type=F32)


def _in_proj_sample_kernel(x_ref, g_ref, w_ref, o_ref, wb_ref, h_ref):
    @pl.when(pl.program_id(0) == 0)
    def _():
        h_ref[...] = _rms(x_ref[...], g_ref[...]).astype(BF16)

    wb = w_ref[...].astype(BF16)
    wb_ref[...] = wb
    o_ref[...] = _dot(h_ref[...], wb)


def in_proj_sample(x, g, w, layer, tn):
    m, k = x.shape
    n = w.shape[2]
    return pl.pallas_call(
        _in_proj_sample_kernel,
        grid=(n // tn,),
        in_specs=[
            pl.BlockSpec((m, k), lambda j: (0, 0)),
            pl.BlockSpec((None, 1, k), lambda j: (layer, 0, 0)),
            pl.BlockSpec((None, k, tn), lambda j: (layer, 0, j)),
        ],
        out_specs=[pl.BlockSpec((m, tn), lambda j: (0, j)), pl.BlockSpec((k, tn), lambda j: (0, j))],
        out_shape=[jax.ShapeDtypeStruct((m, n), F32), jax.ShapeDtypeStruct((k, n), BF16)],
        scratch_shapes=[pltpu.VMEM((m, k), BF16)],
        compiler_params=_cparams("arbitrary"),
        name="in_proj_sample",
    )(x, g, w)


NORM_ROW_SPLIT = 4


def _in_proj_kernel(x_ref, g_ref, w_ref, o_ref, h_ref):
    j = pl.program_id(1)
    rs = x_ref.shape[0] // NORM_ROW_SPLIT

    @pl.when(j == 0)
    def _():
        for r in range(NORM_ROW_SPLIT):
            rows = slice(r * rs, (r + 1) * rs)
            h = _rms(x_ref[rows, :], g_ref[...]).astype(BF16)
            h_ref[rows, :] = h
            o_ref[rows, :] = _dot(h, w_ref[...])

    @pl.when(j > 0)
    def _():
        o_ref[...] = _dot(h_ref[...], w_ref[...])


def in_proj(x, g, wb, layer, tm, tn):
    m, k = x.shape
    n = wb.shape[1]
    return pl.pallas_call(
        _in_proj_kernel,
        grid=(m // tm, n // tn),
        in_specs=[
            pl.BlockSpec((tm, k), lambda i, j: (i, 0)),
            pl.BlockSpec((None, 1, k), lambda i, j: (layer, 0, 0)),
            pl.BlockSpec((k, tn), lambda i, j: (0, j)),
        ],
        out_specs=pl.BlockSpec((tm, tn), lambda i, j: (i, j)),
        out_shape=jax.ShapeDtypeStruct((m, n), F32),
        scratch_shapes=[pltpu.VMEM((tm, k), BF16)],
        compiler_params=_cparams("parallel", "arbitrary", vmem=FFN_CONVERT_VMEM),
        name="in_proj",
    )(x, g, wb)


def _mix_dot(a, o, c, w):
    acc = _dot(a.astype(BF16), w[0:ATTN_WIDTH])
    acc = acc + _dot(o.astype(BF16), w[ATTN_WIDTH:ATTN_WIDTH + HG_WIDTH])
    return acc + _dot(c.astype(BF16), w[ATTN_WIDTH + HG_WIDTH:])


def _out_proj_sample_kernel(x_ref, a_ref, o_ref, c_ref, w_ref, y_ref, *wb_ref):
    w = w_ref[...]
    if wb_ref:
        w = w.astype(BF16)
        wb_ref[0][...] = w
    y_ref[...] = x_ref[...] + _mix_dot(a_ref[...], o_ref[...], c_ref[...], w)


def out_proj_sample(x, a, o, c, w_out, layer, tn):
    m, d = x.shape
    convert = w_out.ndim == 3
    kw = w_out.shape[-2]
    full = lambda width: pl.BlockSpec((m, width), lambda j: (0, 0))
    y_spec = pl.BlockSpec((m, tn), lambda j: (0, j))
    y_shape = jax.ShapeDtypeStruct((m, d), F32)
    if convert:
        w_spec = pl.BlockSpec((None, kw, tn), lambda j: (layer, 0, j))
        out_specs = [y_spec, pl.BlockSpec((kw, tn), lambda j: (0, j))]
        out_shape = [y_shape, jax.ShapeDtypeStruct((kw, d), BF16)]
    else:
        w_spec = pl.BlockSpec((kw, tn), lambda j: (0, j))
        out_specs, out_shape = y_spec, y_shape
    return pl.pallas_call(
        _out_proj_sample_kernel,
        grid=(d // tn,),
        in_specs=[y_spec, full(ATTN_WIDTH), full(HG_WIDTH), full(CONV_WIDTH), w_spec],
        out_specs=out_specs,
        out_shape=out_shape,
        compiler_params=_cparams("parallel"),
        name="out_proj_sample",
    )(x, a, o, c, w_out)


def _conv_taps(u, prev1, prev2, w_ref):
    return w_ref[0:1, :] * prev2 + w_ref[1:2, :] * prev1 + w_ref[2:3, :] * u


def _out_proj_kernel(tiles_per_seq, x_ref, a_ref, o_ref, cb_ref, cc_ref, ch_ref, cw_ref, w_ref,
                     y_ref, cs_ref, carry_ref):
    i = pl.program_id(0)
    tm = cc_ref.shape[0]

    @pl.when(i % tiles_per_seq == 0)
    def _():
        carry_ref[...] = jnp.zeros_like(carry_ref)

    u = cc_ref[...] * ch_ref[...]
    row = lax.broadcasted_iota(jnp.int32, u.shape, 0)
    last1 = carry_ref[7:8, :]
    last2 = carry_ref[6:7, :]
    um1 = jnp.where(row == 0, last1, pltpu.roll(u, 1, 0))
    um2 = jnp.where(row == 0, last2, jnp.where(row == 1, last1, pltpu.roll(u, 2, 0)))
    c = cb_ref[...] * _conv_taps(u, um1, um2, cw_ref)
    carry_ref[...] = u[tm - 8:tm]

    y_ref[...] = x_ref[...] + _mix_dot(a_ref[...], o_ref[...], c, w_ref)

    @pl.when(i % tiles_per_seq == tiles_per_seq - 1)
    def _():
        cs_ref[0] = carry_ref[8 - (CONV_K - 1):8, :]


def out_proj(x, a, o, proj, conv_w, wb, layer, seq_len, tm):
    m, d = x.shape
    tiles_per_seq = seq_len // tm
    col = lambda blk: pl.BlockSpec((tm, CONV_WIDTH), lambda i: (i, blk))
    return pl.pallas_call(
        functools.partial(_out_proj_kernel, tiles_per_seq),
        grid=(m // tm,),
        in_specs=[
            pl.BlockSpec((tm, d), lambda i: (i, 0)),
            pl.BlockSpec((tm, ATTN_WIDTH), lambda i: (i, 0)),
            pl.BlockSpec((tm, HG_WIDTH), lambda i: (i, 0)),
            col(CB_BLK_512), col(CC_BLK_512), col(CH_BLK_512),
            pl.BlockSpec((None, CONV_K, CONV_WIDTH), lambda i: (layer, 0, 0)),
            pl.BlockSpec(wb.shape, lambda i: (0, 0)),
        ],
        out_specs=[pl.BlockSpec((tm, d), lambda i: (i, 0)),
                   pl.BlockSpec((1, CONV_K - 1, CONV_WIDTH), lambda i: (i // tiles_per_seq, 0, 0))],
        out_shape=[jax.ShapeDtypeStruct((m, d), F32),
                   jax.ShapeDtypeStruct((m // seq_len, CONV_K - 1, CONV_WIDTH), F32)],
        scratch_shapes=[pltpu.VMEM((8, CONV_WIDTH), F32)],
        compiler_params=_cparams("arbitrary"),
        name="out_proj",
    )(x, a, o, proj, proj, proj, conv_w, wb)


def _ffn_step(h_ref, wg, wu, wd, y_ref):
    h = h_ref[...]
    gate = _dot(h, wg)
    up = _dot(h, wu)
    act = _silu(gate) * up
    y_ref[...] += _dot(act.astype(BF16), wd)


def _ffn_begin(x_ref, g_ref, y_ref, h_ref):
    x = x_ref[...]
    h_ref[...] = _rms(x, g_ref[...]).astype(BF16)
    y_ref[...] = x


def _ffn_sample_kernel(last_layer, x_ref, g_ref, wg_ref, wu_ref, wd_ref, gf_ref,
                       y_ref, wgb_ref, wub_ref, wdb_ref, h_ref):
    f = pl.program_id(0)

    @pl.when(f == 0)
    def _():
        _ffn_begin(x_ref, g_ref, y_ref, h_ref)

    wg = wg_ref[...].astype(BF16)
    wu = wu_ref[...].astype(BF16)
    wd = wd_ref[...].astype(BF16)
    wgb_ref[...] = wg
    wub_ref[...] = wu
    wdb_ref[...] = wd
    _ffn_step(h_ref, wg, wu, wd, y_ref)

    if last_layer:
        @pl.when(f == pl.num_programs(0) - 1)
        def _():
            y_ref[...] = _rms(y_ref[...], gf_ref[...])


def ffn_sample(x, g, w_gate_up, w_down, g_final, layer, tf):
    m, d = x.shape
    nf = D_FF // tf
    return pl.pallas_call(
        functools.partial(_ffn_sample_kernel, layer == DEPTH - 1),
        grid=(nf,),
        in_specs=[
            pl.BlockSpec((m, d), lambda f: (0, 0)),
            pl.BlockSpec((None, 1, d), lambda f: (layer, 0, 0)),
            pl.BlockSpec((None, d, tf), lambda f: (layer, 0, f)),
            pl.BlockSpec((None, d, tf), lambda f: (layer, 0, f + nf)),
            pl.BlockSpec((None, tf, d), lambda f: (layer, f, 0)),
            pl.BlockSpec((1, d), lambda f: (0, 0)),
        ],
        out_specs=[
            pl.BlockSpec((m, d), lambda f: (0, 0)),
            pl.BlockSpec((d, tf), lambda f: (0, f)),
            pl.BlockSpec((d, tf), lambda f: (0, f)),
            pl.BlockSpec((tf, d), lambda f: (f, 0)),
        ],
        out_shape=[jax.ShapeDtypeStruct((m, d), F32), jax.ShapeDtypeStruct((d, D_FF), BF16),
                   jax.ShapeDtypeStruct((d, D_FF), BF16), jax.ShapeDtypeStruct((D_FF, d), BF16)],
        scratch_shapes=[pltpu.VMEM((m, d), BF16)],
        compiler_params=_cparams("arbitrary"),
        name="ffn_sample",
    )(x, g, w_gate_up, w_gate_up, w_down, g_final)


def _ffn_kernel(last_layer, x_ref, g_ref, wg_ref, wu_ref, wd_ref, gf_ref, y_ref, h_ref, side=lambda: None):
    f = pl.program_id(1)
    rs = x_ref.shape[0] // NORM_ROW_SPLIT

    @pl.when(f == 0)
    def _():
        side()
        for r in range(NORM_ROW_SPLIT):
            rows = slice(r * rs, (r + 1) * rs)
            x = x_ref[rows, :]
            h = _rms(x, g_ref[...]).astype(BF16)
            h_ref[rows, :] = h
            act = _silu(_dot(h, wg_ref[...])) * _dot(h, wu_ref[...])
            y_ref[rows, :] = x + _dot(act.astype(BF16), wd_ref[...])

    @pl.when(f > 0)
    def _():
        side()
        _ffn_step(h_ref, wg_ref[...], wu_ref[...], wd_ref[...], y_ref)

    if last_layer:
        @pl.when(f == pl.num_programs(1) - 1)
        def _():
            y_ref[...] = _rms(y_ref[...], gf_ref[...])


def _side_cast(w, slab, axis, grid):
    nsteps = grid[0] * grid[1]
    shape = list(w.shape[1:])
    assert shape[axis] % nsteps == 0
    shape[axis] //= nsteps
    pos = lambda i, j: i * grid[1] + j
    idx = (lambda i, j: (pos(i, j), 0)) if axis == 0 else (lambda i, j: (0, pos(i, j)))
    src = pl.BlockSpec((None, *shape), lambda i, j: (slab, *idx(i, j)))
    return src, pl.BlockSpec(tuple(shape), idx), jax.ShapeDtypeStruct(w.shape[1:], BF16)


def _cast_sides(src_refs, dst_refs):
    for src, dst in zip(src_refs, dst_refs):
        dst[...] = src[...].astype(BF16)


FFN_CONVERT_VMEM = 60000 * 1024


def _ffn_convert_kernel(x_ref, g_ref, wg_ref, wu_ref, wd_ref, gf_ref, side_src, y_ref, side_dst, h_ref):
    _ffn_kernel(False, x_ref, g_ref, wg_ref, wu_ref, wd_ref, gf_ref, y_ref, h_ref,
                side=lambda: _cast_sides([side_src], [side_dst]))


def ffn(x, g, w_gate, gate_off, w_up, up_off, wdb, g_final, layer, tm, tf, convert=None):
    m, d = x.shape
    grid = (m // tm, D_FF // tf)
    in_specs = [
        pl.BlockSpec((tm, d), lambda i, f: (i, 0)),
        pl.BlockSpec((None, 1, d), lambda i, f: (layer, 0, 0)),
        pl.BlockSpec((d, tf), lambda i, f: (0, gate_off + f)),
        pl.BlockSpec((d, tf), lambda i, f: (0, up_off + f)),
        pl.BlockSpec((tf, d), lambda i, f: (f, 0)),
        pl.BlockSpec((1, d), lambda i, f: (0, 0)),
    ]
    y_spec = pl.BlockSpec((tm, d), lambda i, f: (i, 0))
    y_shape = jax.ShapeDtypeStruct((m, d), F32)
    if convert is None:
        return pl.pallas_call(
            functools.partial(_ffn_kernel, layer == DEPTH - 1),
            grid=grid, in_specs=in_specs, out_specs=y_spec, out_shape=y_shape,
            scratch_shapes=[pltpu.VMEM((tm, d), BF16)],
            compiler_params=_cparams("parallel", "arbitrary"),
            name="ffn",
        )(x, g, w_gate, w_up, wdb, g_final)

    src, dst, dst_shape = _side_cast(convert, layer + 1, 1, grid)
    return pl.pallas_call(
        _ffn_convert_kernel,
        grid=grid,
        in_specs=in_specs + [src],
        out_specs=[y_spec, dst],
        out_shape=[y_shape, dst_shape],
        scratch_shapes=[pltpu.VMEM((tm, d), BF16)],
        compiler_params=_cparams("parallel", "arbitrary", vmem=FFN_CONVERT_VMEM),
        name="ffn_convert",
    )(x, g, w_gate, w_up, wdb, g_final, convert)


def _rope_tables(pos):
    half = ROT_DIM // 2
    inv = ROPE_THETA ** (-jnp.arange(half, dtype=F32) * 2.0 / ROT_DIM)
    ang = pos.astype(F32)[:, None] * inv[None, :]
    cos, sin = jnp.cos(ang), jnp.sin(ang)
    n = pos.shape[0]
    ones = jnp.ones((n, HEAD_DIM - ROT_DIM), F32)
    zeros = jnp.zeros((n, HEAD_DIM - ROT_DIM), F32)
    zh = jnp.zeros((n, half), F32)
    c = jnp.concatenate([cos, cos, ones], axis=-1)
    s_lo = jnp.concatenate([-sin, zh, zeros], axis=-1)
    s_hi = jnp.concatenate([zh, sin, zeros], axis=-1)
    rep = lambda t: jnp.concatenate([t, t], axis=-1)
    return rep(c), rep(s_lo), rep(s_hi)


def _rope(x, c, s_lo, s_hi):
    half = ROT_DIM // 2
    outs = []
    for j in range(x.shape[1] // 128):
        xc = x[:, j * 128:(j + 1) * 128]
        outs.append(xc * c + pltpu.roll(xc, 128 - half, 1) * s_lo + pltpu.roll(xc, half, 1) * s_hi)
    return outs[0] if len(outs) == 1 else jnp.concatenate(outs, axis=1)


def _rope_tables_t(pos):
    half = ROT_DIM // 2
    inv = ROPE_THETA ** (-jnp.arange(half, dtype=F32) * 2.0 / ROT_DIM)
    ang = inv[:, None] * pos.astype(F32)[None, :]
    return jnp.cos(ang), jnp.sin(ang)


def _rope_t(xt, cos_t, sin_t):
    half = ROT_DIM // 2
    pieces = []
    for base in range(0, xt.shape[0], HEAD_DIM):
        x1 = xt[base:base + half]
        x2 = xt[base + half:base + ROT_DIM]
        pieces += [x1 * cos_t - x2 * sin_t, x2 * cos_t + x1 * sin_t, xt[base + ROT_DIM:base + HEAD_DIM]]
    return jnp.concatenate(pieces, axis=0)


def _softmax_sink_pv(s, sk, v):
    m = jnp.maximum(jnp.max(s, axis=-1, keepdims=True), sk)
    p = jnp.exp(s - m)
    denom = jnp.sum(p, axis=-1, keepdims=True) + jnp.exp(sk - m)
    return _dot(p.astype(BF16), v) / denom


N_ATTN_PROMPT_IN, N_ATTN_PROMPT_OUT = 9, 3


def _attn_prompt_kernel(layer, nside, *refs):
    it = iter(refs)
    take = lambda cnt: [next(it) for _ in range(cnt)]
    sink_ref, q_ref, k_ref, v_ref, c_ref, slo_ref, shi_ref, ct_ref, st_ref = take(N_ATTN_PROMPT_IN)
    side_src = take(nside)
    a_ref, ko_ref, vo_ref = take(N_ATTN_PROMPT_OUT)
    side_dst = take(nside)
    kprev_ref, vtprev_ref = take(2)

    n = pl.program_id(1)
    nstep = pl.num_programs(1)
    blk = WINDOW
    nsub = q_ref.shape[1] // blk
    lanes = Q_PER_KV * blk
    sub = lambda j: slice(j * blk, (j + 1) * blk)

    k_all = _rope(k_ref[0], c_ref[...], slo_ref[...], shi_ref[...])
    v_all = v_ref[0]
    vt_all = v_all.T
    qt_all = (_rope_t(q_ref[0].T, ct_ref[...], st_ref[...]) * ATTN_SCALE).astype(BF16)

    key = lax.broadcasted_iota(jnp.int32, (blk, lanes), 0)
    qry = lax.broadcasted_iota(jnp.int32, (blk, lanes), 1) % blk
    newer = key > qry
    zero_rows = jnp.zeros((HEAD_DIM, lanes), BF16)

    def scores(h, j, first):
        pair = slice((h // 2) * 2 * HEAD_DIM, (h // 2 + 1) * 2 * HEAD_DIM)
        qt = qt_all[:, sub(j)]
        qh = jnp.concatenate(
            [qt[(Q_PER_KV * h + g) * HEAD_DIM:(Q_PER_KV * h + g + 1) * HEAD_DIM] for g in range(Q_PER_KV)], axis=1)
        qz = jnp.concatenate([qh, zero_rows] if h % 2 == 0 else [zero_rows, qh], axis=0)
        s_cur = _dot(k_all[sub(j), pair].astype(BF16), qz)
        if first:
            return None, s_cur
        k_prev = kprev_ref[:, pair] if j == 0 else k_all[sub(j - 1), pair]
        return _dot(k_prev.astype(BF16), qz), s_cur

    def softmax(h, s_prev, s_cur):
        sink = jnp.concatenate(
            [jnp.full((1, blk), sink_ref[layer, Q_PER_KV * h + g], F32) for g in range(Q_PER_KV)], axis=1)
        if s_prev is None:
            f = jnp.where(newer, NEG_BIG, s_cur)
            m = jnp.maximum(jnp.max(f, axis=0, keepdims=True), sink)
            p = jnp.exp(f - m)
            denom = jnp.sum(p, axis=0, keepdims=True) + jnp.exp(sink - m)
            return p.astype(BF16), None, denom
        f = jnp.where(newer, s_prev, s_cur)
        d = jnp.sum(jnp.where(key == qry, s_prev, 0.0), axis=0, keepdims=True)
        m = jnp.maximum(jnp.maximum(jnp.max(f, axis=0, keepdims=True), d), sink)
        p = jnp.exp(f - m)
        pd = jnp.exp(d - m)
        denom = jnp.sum(p, axis=0, keepdims=True) + pd + jnp.exp(sink - m)
        pp = jnp.concatenate([jnp.where(newer, p, 0.0), jnp.where(newer, 0.0, p)], axis=0).astype(BF16)
        return pp, pd, denom

    def weighted_values(h, j, pp, pd, denom):
        rows = slice(h * HEAD_DIM, (h + 1) * HEAD_DIM)
        vt_cur = vt_all[rows, sub(j)]
        inv = 1.0 / denom
        if pd is None:
            return _dot(vt_cur.astype(BF16), pp) * inv
        vt_prev = vtprev_ref[rows, :] if j == 0 else vt_all[rows, sub(j - 1)]
        o = _dot(jnp.concatenate([vt_prev, vt_cur], axis=1).astype(BF16), pp)
        return (o + pd * jnp.concatenate([vt_prev] * Q_PER_KV, axis=1)) * inv

    def run(first_step):
        _cast_sides(side_src, side_dst)
        units = [(h, j) for j in range(nsub) for h in range(N_KV_HEADS)]
        s = {u: scores(*u, first_step and u[1] == 0) for u in units}
        p = {u: softmax(u[0], *s[u]) for u in units}
        o = {u: weighted_values(*u, *p[u]) for u in units}
        for j in range(nsub):
            outs = [o[h, j][:, g * blk:(g + 1) * blk] for h in range(N_KV_HEADS) for g in range(Q_PER_KV)]
            a_ref[0, sub(j), :] = jnp.concatenate(outs, axis=0).T.astype(a_ref.dtype)

    @pl.when(n == 0)
    def _():
        run(True)

    @pl.when(n > 0)
    def _():
        run(False)

    kprev_ref[...] = k_all[sub(nsub - 1)]
    vtprev_ref[...] = vt_all[:, sub(nsub - 1)]

    @pl.when(n == nstep - 1)
    def _():
        ko_ref[0] = k_all[sub(nsub - 1)]
        vo_ref[0] = v_all[sub(nsub - 1)]


def attn_prompt(proj, sink, tables, tables_t, layer, rows, convert=()):
    b, l, _ = proj.shape
    blk = WINDOW
    grid = (b, l // rows)
    tab_spec = pl.BlockSpec((rows, 128), lambda i, n: (n, 0))
    tab_t_spec = pl.BlockSpec((ROT_DIM // 2, rows), lambda i, n: (0, n))
    kv_out = jax.ShapeDtypeStruct((b, blk, KV_WIDTH), F32)
    sides = [_side_cast(w, layer + 1, 0, grid) for w in convert]
    in_specs = [
        pl.BlockSpec(memory_space=pltpu.SMEM),
        pl.BlockSpec((1, rows, ATTN_WIDTH), lambda i, n: (i, n, Q_BLK_1024)),
        pl.BlockSpec((1, rows, KV_WIDTH), lambda i, n: (i, n, K_BLK_256)),
        pl.BlockSpec((1, rows, KV_WIDTH), lambda i, n: (i, n, V_BLK_256)),
        tab_spec, tab_spec, tab_spec, tab_t_spec, tab_t_spec,
    ]
    out_specs = [
        pl.BlockSpec((1, rows, ATTN_WIDTH), lambda i, n: (i, n, 0)),
        pl.BlockSpec((1, blk, KV_WIDTH), lambda i, n: (i, 0, 0)),
        pl.BlockSpec((1, blk, KV_WIDTH), lambda i, n: (i, 0, 0)),
    ]
    assert (len(in_specs), len(out_specs)) == (N_ATTN_PROMPT_IN, N_ATTN_PROMPT_OUT)
    return pl.pallas_call(
        functools.partial(_attn_prompt_kernel, layer, len(sides)),
        grid=grid,
        in_specs=in_specs + [s[0] for s in sides],
        out_specs=out_specs + [s[1] for s in sides],
        out_shape=[jax.ShapeDtypeStruct((b, l, ATTN_WIDTH), BF16), kv_out, kv_out] + [s[2] for s in sides],
        scratch_shapes=[pltpu.VMEM((blk, KV_WIDTH), F32), pltpu.VMEM((KV_WIDTH, blk), F32)],
        compiler_params=_cparams("parallel", "arbitrary"),
        name="attn_prompt",
    )(sink, proj, proj, proj, *tables, *tables_t, *convert)


def _attn_sample_kernel(layer, sink_ref, q_ref, k_ref, v_ref, kc_ref, vc_ref, c_ref, slo_ref, shi_ref,
                        k_carried_ref, v_carried_ref, a_ref, ko_ref, vo_ref):
    del k_carried_ref, v_carried_ref
    r8 = SAMPLE_ROWS
    nseq = r8 // SAMPLE_T
    lb = kc_ref.shape[1]
    c, s_lo, s_hi = c_ref[...], slo_ref[...], shi_ref[...]
    q = _rope(q_ref[...], c, s_lo, s_hi)
    k = _rope(k_ref[...], c, s_lo, s_hi)
    v = v_ref[...]
    pad = jnp.zeros((lb - r8, KV_WIDTH), F32)
    k_pad = jnp.concatenate([k, pad], axis=0)
    v_pad = jnp.concatenate([v, pad], axis=0)

    zeros = jnp.zeros((r8, HEAD_DIM), F32)
    qx = jnp.concatenate(
        [jnp.concatenate([q[:, hg * HEAD_DIM:(hg + 1) * HEAD_DIM] if slot == hg // Q_PER_KV else zeros
                          for slot in range(N_KV_HEADS)], axis=1) for hg in range(N_Q_HEADS)],
        axis=0).astype(BF16)
    sk = jnp.concatenate([jnp.full((r8, 1), sink_ref[layer, hg], F32) for hg in range(N_Q_HEADS)], axis=0)

    rows = N_Q_HEADS * r8
    ri = lax.broadcasted_iota(jnp.int32, (rows, 2 * lb), 0) % r8
    t = ri % SAMPLE_T
    kj = lax.broadcasted_iota(jnp.int32, (rows, 2 * lb), 1)
    cj = kj - lb
    row_seq = lax.broadcasted_iota(jnp.int32, (rows, KV_WIDTH), 0) % r8 // SAMPLE_T

    s = []
    for e in range(nseq):
        kk = jnp.concatenate([kc_ref[e], k_pad], axis=0).astype(BF16)
        s.append(_dot_nt(qx, kk) * ATTN_SCALE)
    o = None
    for e in range(nseq):
        mask = ((kj < lb) & (kj >= t)) | (
            (cj >= e * SAMPLE_T) & (cj < (e + 1) * SAMPLE_T) & (cj - e * SAMPLE_T <= t))
        vv = jnp.concatenate([vc_ref[e], v_pad], axis=0).astype(BF16)
        oe = _softmax_sink_pv(jnp.where(mask, s[e], NEG_BIG), sk, vv)
        o = oe if o is None else jnp.where(row_seq == e, oe, o)
    a_ref[...] = jnp.concatenate(
        [o[hg * r8:(hg + 1) * r8, (hg // Q_PER_KV) * HEAD_DIM:(hg // Q_PER_KV + 1) * HEAD_DIM]
         for hg in range(N_Q_HEADS)], axis=1)

    row = lax.broadcasted_iota(jnp.int32, (lb, KV_WIDTH), 0)
    for e in range(nseq):
        shift = (lb - SAMPLE_T - e * SAMPLE_T) % lb
        for new_pad, cache_ref, out_ref in ((k_pad, kc_ref, ko_ref), (v_pad, vc_ref, vo_ref)):
            new_rows = new_pad if shift == 0 else pltpu.roll(new_pad, shift, 0)
            old_rows = pltpu.roll(cache_ref[e], lb - SAMPLE_T, 0)
            out_ref[e] = jnp.where(row >= lb - SAMPLE_T, new_rows, old_rows)


def _lower_bound(lbr_ref, layer, hs):
    rows = [lbr_ref[r:r + 1, hs] for r in range(DEPTH)]
    mx = functools.reduce(jnp.maximum, rows)
    es = [jnp.exp(r - mx) for r in rows]
    tot = functools.reduce(lambda a, b: a + b, es)
    lbp = [e / tot for e in es]
    acc = lbp[0]
    for r in range(1, layer + 1):
        acc = acc + lbp[r]
    return acc - lbp[0]


def _hgrn_gates(hq, hf, lb):
    q = _silu(hq) * (HG_DK ** -0.5)
    e = jnp.exp(-jnp.abs(hf))
    r = 1.0 / (1.0 + e)
    pos = hf >= 0.0
    sig_pos = jnp.where(pos, r, e * r)
    sig_neg = jnp.where(pos, e * r, r)
    log_f = jnp.log(jnp.maximum(lb, LB_TINY) + (1.0 - lb) * sig_pos)
    k = (1.0 - lb) * sig_neg
    return q, log_f, k


def _hgrn_finish(o, g_norm, hg):
    return _rms(o, g_norm) * _silu(hg)


def _hgrn_levels():
    s = HG_CHUNK // 2
    while s >= 1:
        yield s
        s //= 2


def _hgrn_level_masks():
    t = np.arange(HG_CHUNK)[:, None]
    s = np.arange(HG_CHUNK)[None, :]
    return np.stack([((t % (2 * h) >= h) & (t // (2 * h) == s // (2 * h)) & (s % (2 * h) < h))
                     for h in _hgrn_levels()]).astype(np.float32)


def _hgrn_pair_reference(g_ref, gcum, half, sub8):
    c = HG_CHUNK
    pair = 2 * half
    bcast = lambda r: jnp.broadcast_to(g_ref[r:r + 1, :], (8, HG_DK))
    if pair >= 8:
        return jnp.concatenate([bcast((8 * v // pair) * pair + half - 1) for v in range(c // 8)], axis=0)
    if pair == 4:
        lo = jnp.concatenate([bcast(8 * v + 1) for v in range(c // 8)], axis=0)
        hi = jnp.concatenate([bcast(8 * v + 5) for v in range(c // 8)], axis=0)
        return jnp.where(sub8 < 4, lo, hi)
    return jnp.where(sub8 % 2 == 1, pltpu.roll(gcum, 1, 0), gcum)


def _hgrn_level_operand(q, k, gcum, g_ref, half, row):
    ref = _hgrn_pair_reference(g_ref, gcum, half, row % 8)
    right = row % (2 * half) >= half
    return (jnp.where(right, q, k) * jnp.exp(-jnp.abs(gcum - ref))).astype(BF16)


def _chunk_cumsum(x, buf, row):
    sh = 1
    while sh < HG_CHUNK:
        x = x + jnp.where(row >= sh, pltpu.roll(x, sh, 0), 0.0)
        sh *= 2
    buf[...] = x
    return x


def _hgrn_tile(layer, hq_ref, hf_ref, hi_ref, hg_ref, lbr_ref, gn_ref, m_ref, g_ref, state, emit):
    c = HG_CHUNK
    nchunk = hq_ref.shape[1] // c
    heads = range(HG_HEADS)
    units = [(h, cc) for cc in range(nchunk) for h in heads]
    row = lax.broadcasted_iota(jnp.int32, (c, HG_DK), 0)
    cols = lambda h: slice(h * HG_DK, (h + 1) * HG_DK)
    rws = lambda cc: slice(cc * c, (cc + 1) * c)
    gbuf = lambda h, cc: g_ref.at[h * nchunk + cc]
    lbs = [_lower_bound(lbr_ref, layer, cols(h)) for h in heads]

    q, k, v, gcum = {}, {}, {}, {}
    for u in units:
        h, cc = u
        q[u], log_f, k[u] = _hgrn_gates(hq_ref[0, rws(cc), cols(h)], hf_ref[0, rws(cc), cols(h)], lbs[h])
        v[u] = hi_ref[0, rws(cc), cols(h)]
        gcum[u] = _chunk_cumsum(log_f, gbuf(h, cc), row)

    a = {u: jnp.zeros((c, c), F32) for u in units}
    for lvl, half in enumerate(_hgrn_levels()):
        for u in units:
            x = _hgrn_level_operand(q[u], k[u], gcum[u], gbuf(*u), half, row)
            a[u] = a[u] + _dot_nt(x, x) * m_ref[lvl]

    o_intra, qs, ks, decay = {}, {}, {}, {}
    for u in units:
        vb = v[u].astype(BF16)
        o_intra[u] = _dot(a[u].astype(BF16), vb) + jnp.sum(q[u] * k[u], axis=-1, keepdims=True) * v[u]
        g_last = gbuf(*u)[c - 1:c, :]
        qs[u] = (q[u] * jnp.exp(gcum[u])).astype(BF16)
        ks[u] = (k[u] * jnp.exp(g_last - gcum[u])).astype(BF16)
        decay[u] = jnp.transpose(jnp.broadcast_to(jnp.exp(g_last), (HG_DK, HG_DK)))

    state = list(state)
    for cc in range(nchunk):
        o_inter = [_dot(qs[h, cc], state[h].astype(BF16)) for h in heads]
        update = [_dot_tn(ks[h, cc], v[h, cc].astype(BF16)) for h in heads]
        for h in heads:
            o = o_inter[h] + o_intra[h, cc]
            emit(h, rws(cc), _hgrn_finish(o, gn_ref[h:h + 1, :], hg_ref[0, rws(cc), cols(h)]))
            state[h] = state[h] * decay[h, cc] + update[h]
    return state


N_HGRN_PROMPT_IN, N_HGRN_PROMPT_OUT = 7, 2


def _hgrn_prompt_kernel(layer, nside, *refs):
    it = iter(refs)
    take = lambda cnt: [next(it) for _ in range(cnt)]
    hq_ref, hf_ref, hi_ref, hg_ref, lbr_ref, gn_ref, m_ref = take(N_HGRN_PROMPT_IN)
    side_src = take(nside)
    o_ref, so_ref = take(N_HGRN_PROMPT_OUT)
    side_dst = take(nside)
    s_ref, g_ref = take(2)
    step = pl.program_id(1)
    heads = range(HG_HEADS)

    @pl.when(step == 0)
    def _():
        s_ref[...] = jnp.zeros_like(s_ref)

    def emit(h, rs, o):
        o_ref[0, rs, h * HG_DV:(h + 1) * HG_DV] = o.astype(o_ref.dtype)

    _cast_sides(side_src, side_dst)
    state = _hgrn_tile(layer, hq_ref, hf_ref, hi_ref, hg_ref, lbr_ref, gn_ref, m_ref, g_ref,
                       [s_ref[h] for h in heads], emit)
    for h in heads:
        s_ref[h] = state[h]

    @pl.when(step == pl.num_programs(1) - 1)
    def _():
        so_ref[0] = s_ref[...]


def hgrn_prompt(proj, lb_raw, g_norm, layer, rows, convert=()):
    b, l, _ = proj.shape
    c = HG_CHUNK
    grid = (b, l // rows)
    masks = jnp.asarray(_hgrn_level_masks())
    col = lambda blk: pl.BlockSpec((1, rows, HG_WIDTH), lambda i, n: (i, n, blk))
    sides = [_side_cast(w, layer + 1, 0, grid) for w in convert]
    in_specs = [
        col(HQ_BLK_512), col(HF_BLK_512), col(HI_BLK_512), col(HG_BLK_512),
        pl.BlockSpec((DEPTH, HG_WIDTH), lambda i, n: (0, 0)),
        pl.BlockSpec((None, HG_HEADS, HG_DV), lambda i, n: (layer, 0, 0)),
        pl.BlockSpec(masks.shape, lambda i, n: (0, 0, 0)),
    ]
    out_specs = [
        pl.BlockSpec((1, rows, HG_WIDTH), lambda i, n: (i, n, 0)),
        pl.BlockSpec((1, HG_HEADS, HG_DK, HG_DV), lambda i, n: (i, 0, 0, 0)),
    ]
    assert (len(in_specs), len(out_specs)) == (N_HGRN_PROMPT_IN, N_HGRN_PROMPT_OUT)
    return pl.pallas_call(
        functools.partial(_hgrn_prompt_kernel, layer, len(sides)),
        grid=grid,
        in_specs=in_specs + [s[0] for s in sides],
        out_specs=out_specs + [s[1] for s in sides],
        out_shape=[jax.ShapeDtypeStruct((b, l, HG_WIDTH), BF16),
                   jax.ShapeDtypeStruct((b, HG_HEADS, HG_DK, HG_DV), F32)] + [s[2] for s in sides],
        scratch_shapes=[pltpu.VMEM((HG_HEADS, HG_DK, HG_DV), F32),
                        pltpu.VMEM((HG_HEADS * (rows // c), c, HG_DK), F32)],
        compiler_params=_cparams("parallel", "arbitrary"),
        name="hgrn_prompt",
    )(proj, proj, proj, proj, lb_raw, g_norm, masks, *convert)


def _hgrn_sample_kernel(layer, hq_ref, hf_ref, hi_ref, hg_ref, lbr_ref, gn_ref, s_ref, carried_ref,
                        o_ref, so_ref, q_ref, k_ref, g_ref):
    del carried_ref
    r8, tt = SAMPLE_ROWS, SAMPLE_T
    nseq = r8 // tt
    row = lax.broadcasted_iota(jnp.int32, (r8, HG_DK), 0)
    t_idx = row % tt
    seq = row // tt

    def pick(ref, s):
        out = ref[s:s + 1, :]
        for e in range(1, nseq):
            out = jnp.where(seq == e, ref[e * tt + s:e * tt + s + 1, :], out)
        return out

    for h in range(HG_HEADS):
        hs = slice(h * HG_DK, (h + 1) * HG_DK)
        lb = _lower_bound(lbr_ref, layer, hs)
        q, log_f, k = _hgrn_gates(hq_ref[:, hs], hf_ref[:, hs], lb)
        gcum = log_f
        sh = 1
        while sh < tt:
            gcum = gcum + jnp.where(t_idx >= sh, pltpu.roll(gcum, sh, 0), 0.0)
            sh *= 2
        q_ref[...] = q
        k_ref[...] = k
        g_ref[...] = gcum
        v = hi_ref[:, hs]
        g_last = pick(g_ref, tt - 1)
        qs = (q * jnp.exp(gcum)).astype(BF16)
        ks_all = k * jnp.exp(g_last - gcum)

        o = jnp.zeros((r8, HG_DV), F32)
        for e in range(nseq):
            state = s_ref[e, h]
            o = jnp.where(seq == e, _dot(qs, state.astype(BF16)), o)
            g_last_e = g_ref[e * tt + tt - 1:e * tt + tt, :]
            decay = jnp.transpose(jnp.broadcast_to(jnp.exp(g_last_e), (HG_DK, HG_DK)))
            ks_e = jnp.where(seq == e, ks_all, 0.0).astype(BF16)
            so_ref[e, h] = state * decay + _dot_tn(ks_e, v.astype(BF16))
        for s in range(tt):
            dec = jnp.exp(jnp.where(t_idx >= s, gcum - pick(g_ref, s), NEG_BIG))
            col = jnp.sum(q * (dec * pick(k_ref, s)), axis=-1, keepdims=True)
            o = o + col * pick(hi_ref.at[:, hs], s)
        o_ref[:, hs] = _hgrn_finish(o, gn_ref[h:h + 1, :], hg_ref[:, hs])


def _conv_sample_kernel(cb_ref, cc_ref, ch_ref, w_ref, st_ref, y_ref, so_ref, u_ref):
    r8, tt = SAMPLE_ROWS, SAMPLE_T
    nseq = r8 // tt
    u = cc_ref[...] * ch_ref[...]
    u_ref[...] = u
    row = lax.broadcasted_iota(jnp.int32, u.shape, 0)
    t_idx = row % tt
    seq = row // tt
    last1 = st_ref[0, 1:2, :]
    last2 = st_ref[0, 0:1, :]
    for e in range(1, nseq):
        last1 = jnp.where(seq == e, st_ref[e, 1:2, :], last1)
        last2 = jnp.where(seq == e, st_ref[e, 0:1, :], last2)
    um1 = jnp.where(t_idx == 0, last1, pltpu.roll(u, 1, 0))
    um2 = jnp.where(t_idx == 0, last2, jnp.where(t_idx == 1, last1, pltpu.roll(u, 2, 0)))
    y_ref[...] = cb_ref[...] * _conv_taps(u, um1, um2, w_ref)
    for e in range(nseq):
        so_ref[e] = u_ref[(e + 1) * tt - (CONV_K - 1):(e + 1) * tt, :]


N_ATTN_SAMPLE_IN, N_HGRN_SAMPLE_IN, N_CONV_SAMPLE_IN = 11, 8, 5
N_ATTN_SAMPLE_OUT, N_HGRN_SAMPLE_OUT, N_CONV_SAMPLE_OUT = 3, 2, 2


def _mix_sample_kernel(layer, *refs):
    it = iter(refs)
    take = lambda n: [next(it) for _ in range(n)]
    attn_in, hgrn_in, conv_in = take(N_ATTN_SAMPLE_IN), take(N_HGRN_SAMPLE_IN), take(N_CONV_SAMPLE_IN)
    attn_out, hgrn_out, conv_out = take(N_ATTN_SAMPLE_OUT), take(N_HGRN_SAMPLE_OUT), take(N_CONV_SAMPLE_OUT)
    hgrn_scratch, conv_scratch = take(3), take(1)
    _attn_sample_kernel(layer, *attn_in, *attn_out)
    _hgrn_sample_kernel(layer, *hgrn_in, *hgrn_out, *hgrn_scratch)
    _conv_sample_kernel(*conv_in, *conv_out, *conv_scratch)


def mix_sample(proj, k_cache, v_cache, k_new, v_new, sink, tables, state, state_new, lb_raw, g_norm,
               conv_state, conv_w, layer):
    m = proj.shape[0]
    r8 = SAMPLE_ROWS
    nseq = r8 // SAMPLE_T
    lb = k_cache.shape[2]
    rows = lambda width, blk: pl.BlockSpec((r8, width), lambda i: (i, blk))
    tab_spec = pl.BlockSpec((r8, 128), lambda i: (0, 0))
    cache_spec = pl.BlockSpec((None, nseq, lb, KV_WIDTH), lambda i: (layer, i, 0, 0))
    st_spec = pl.BlockSpec((None, nseq, HG_HEADS, HG_DK, HG_DV), lambda i: (layer, i, 0, 0, 0))
    carried = pl.BlockSpec(memory_space=pl.ANY)
    attn_in = [pl.BlockSpec(memory_space=pltpu.SMEM),
               rows(ATTN_WIDTH, Q_BLK_1024), rows(KV_WIDTH, K_BLK_256), rows(KV_WIDTH, V_BLK_256),
               cache_spec, cache_spec, tab_spec, tab_spec, tab_spec, carried, carried]
    hgrn_in = [rows(HG_WIDTH, HQ_BLK_512), rows(HG_WIDTH, HF_BLK_512), rows(HG_WIDTH, HI_BLK_512),
               rows(HG_WIDTH, HG_BLK_512),
               pl.BlockSpec((DEPTH, HG_WIDTH), lambda i: (0, 0)),
               pl.BlockSpec((None, HG_HEADS, HG_DV), lambda i: (layer, 0, 0)),
               st_spec, carried]
    conv_in = [rows(CONV_WIDTH, CB_BLK_512), rows(CONV_WIDTH, CC_BLK_512), rows(CONV_WIDTH, CH_BLK_512),
               pl.BlockSpec((None, CONV_K, CONV_WIDTH), lambda i: (layer, 0, 0)),
               pl.BlockSpec((None, nseq, CONV_K - 1, CONV_WIDTH), lambda i: (layer, i, 0, 0))]
    assert (len(attn_in), len(hgrn_in), len(conv_in)) == (N_ATTN_SAMPLE_IN, N_HGRN_SAMPLE_IN, N_CONV_SAMPLE_IN)
    out_specs = [rows(ATTN_WIDTH, 0), cache_spec, cache_spec,
                 rows(HG_WIDTH, 0), st_spec,
                 rows(CONV_WIDTH, 0), pl.BlockSpec((nseq, CONV_K - 1, CONV_WIDTH), lambda i: (i, 0, 0))]
    out_shape = [jax.ShapeDtypeStruct((m, ATTN_WIDTH), F32),
                 jax.ShapeDtypeStruct(k_new.shape, F32), jax.ShapeDtypeStruct(v_new.shape, F32),
                 jax.ShapeDtypeStruct((m, HG_WIDTH), F32), jax.ShapeDtypeStruct(state_new.shape, F32),
                 jax.ShapeDtypeStruct((m, CONV_WIDTH), F32), jax.ShapeDtypeStruct(conv_state.shape[1:], F32)]
    return pl.pallas_call(
        functools.partial(_mix_sample_kernel, layer),
        grid=(m // r8,),
        in_specs=attn_in + hgrn_in + conv_in,
        out_specs=out_specs,
        out_shape=out_shape,
        input_output_aliases={9: 1, 10: 2, N_ATTN_SAMPLE_IN + N_HGRN_SAMPLE_IN - 1: 4},
        scratch_shapes=[pltpu.VMEM((r8, HG_DK), F32), pltpu.VMEM((r8, HG_DK), F32), pltpu.VMEM((r8, HG_DK), F32),
                        pltpu.VMEM((r8, CONV_WIDTH), F32)],
        compiler_params=_cparams("parallel"),
        name="mix_sample",
    )(sink, proj, proj, proj, k_cache, v_cache, *tables, k_new, v_new,
      proj, proj, proj, proj, lb_raw, g_norm, state, state_new,
      proj, proj, proj, conv_w, conv_state)


def kernel(x_prompt, x_sample, cache_attn_k, cache_attn_v, state_hgrn, state_conv, w_in, attn_sink,
           hgrn_lower_bounds, hgrn_norm, conv_w, w_out, norm_mix, norm_ffn, w_gate_up, w_down, norm_final):
    bp, lp, d = x_prompt.shape
    bs, ls, _ = x_sample.shape
    assert ls == SAMPLE_T and (bs * ls) % SAMPLE_ROWS == 0
    lb = cache_attn_k.shape[2]
    mp, ms = bp * lp, bs * ls

    pos_p = jnp.arange(lp, dtype=jnp.int32)
    tab_p = _rope_tables(pos_p)
    tab_pt = _rope_tables_t(pos_p)
    pos_s = PAST_LEN + jnp.arange(SAMPLE_ROWS, dtype=jnp.int32) % SAMPLE_T
    tab_s = _rope_tables(pos_s)

    xp = x_prompt.reshape(mp, d)
    xs = x_sample.reshape(ms, d)
    kc = cache_attn_k.reshape(DEPTH, bs, lb, KV_WIDTH)
    vc = cache_attn_v.reshape(DEPTH, bs, lb, KV_WIDTH)

    g_mix = norm_mix.reshape(DEPTH, 1, d)
    g_ffn = norm_ffn.reshape(DEPTH, 1, d)
    g_final = norm_final.reshape(1, d)

    k_s = jnp.zeros(kc.shape, F32)
    v_s = jnp.zeros(vc.shape, F32)
    s_s = jnp.zeros(state_hgrn.shape, F32)

    in_tiles, out_tm, ffn_tiles = (1024, 1280), 512, (1024, 512)
    attn_rows, attn_rows_last, hgrn_rows = 4 * WINDOW, 8 * WINDOW, 8 * HG_CHUNK
    s_in_tn_f32, s_in_tn, s_out_tn_f32, s_out_tn, s_ffn_tf_f32, s_ffn_tf = 1024, 1024, 512, 1024, 512, 512

    def gate_up_blocks(w_gate, w_up, tf):
        return (w_gate, 0, w_up, 0 if w_up is not w_gate else D_FF // tf)

    outs = {name: [] for name in ("kp", "vp", "sp", "cp", "cs")}
    for l in range(DEPTH):
        if l == 0:
            proj_s, w_in_b = in_proj_sample(xs, g_mix, w_in, l, s_in_tn_f32)
        else:
            proj_s = in_proj(xs, g_mix, w_in_b, l, ms, s_in_tn)
        a_s, k_s, v_s, o_s, s_s, c_s, cst_s = mix_sample(
            proj_s, kc, vc, k_s, v_s, attn_sink, tab_s, state_hgrn, s_s, hgrn_lower_bounds, hgrn_norm,
            state_conv, conv_w, l)
        if l == 0:
            xs, w_out_b = out_proj_sample(xs, a_s, o_s, c_s, w_out, l, s_out_tn_f32)
            xs, w_gate_b, w_up_b, w_down_b = ffn_sample(xs, g_ffn, w_gate_up, w_down, g_final, l, s_ffn_tf_f32)
        else:
            xs = out_proj_sample(xs, a_s, o_s, c_s, w_out_b, l, s_out_tn)
            xs = ffn(xs, g_ffn, *gate_up_blocks(w_gate_b, w_up_b, s_ffn_tf), w_down_b, g_final, l, ms, s_ffn_tf)

        more = l < DEPTH - 1
        proj_p = in_proj(xp, g_mix, w_in_b, l, *in_tiles)
        proj_p3 = proj_p.reshape(bp, lp, IN_WIDTH)
        a_p, k_p, v_p, *cast_a = attn_prompt(proj_p3, attn_sink, tab_p, tab_pt, l,
                                             attn_rows if more else attn_rows_last,
                                             convert=(w_in, w_out) if more else ())
        o_p, s_p, *cast_h = hgrn_prompt(proj_p3, hgrn_lower_bounds, hgrn_norm, l, hgrn_rows,
                                        convert=(w_down,) if more else ())
        xp, cst_p = out_proj(xp, a_p.reshape(mp, ATTN_WIDTH), o_p.reshape(mp, HG_WIDTH), proj_p,
                             conv_w, w_out_b, l, lp, out_tm)
        gate_up = gate_up_blocks(w_gate_b, w_up_b, ffn_tiles[1])
        if more:
            xp, w_gu_b = ffn(xp, g_ffn, *gate_up, w_down_b, g_final, l, *ffn_tiles, convert=w_gate_up)
            (w_in_b, w_out_b), (w_down_b,) = cast_a, cast_h
            w_gate_b = w_up_b = w_gu_b
        else:
            xp = ffn(xp, g_ffn, *gate_up, w_down_b, g_final, l, *ffn_tiles)

        outs["kp"].append(k_p.reshape(bp, WINDOW, N_KV_HEADS, HEAD_DIM))
        outs["vp"].append(v_p.reshape(bp, WINDOW, N_KV_HEADS, HEAD_DIM))
        outs["sp"].append(s_p)
        outs["cp"].append(cst_p)
        outs["cs"].append(cst_s)

    st = lambda name: jnp.stack(outs[name])
    kv_shape = (DEPTH, bs, lb, N_KV_HEADS, HEAD_DIM)
    return (xp.reshape(bp, lp, d), xs.reshape(bs, ls, d), st("kp"), st("vp"), st("sp"), st("cp"),
            k_s.reshape(kv_shape), v_s.reshape(kv_shape), s_s, st("cs"))
```

```python
import functools

import jax
import jax.numpy as jnp
import numpy as np
from jax import lax
from jax.experimental import pallas as pl
from jax.experimental.pallas import tpu as pltpu

F32 = jnp.float32
BF16 = jnp.bfloat16

D_MODEL = 2048
DEPTH = 4
PAST_LEN = 16384
HEAD_DIM = 64
N_Q_HEADS = 16
N_KV_HEADS = 4
Q_PER_KV = 4
ATTN_WIDTH = 1024
KV_WIDTH = 256
WINDOW = 128
ROPE_THETA = 500000.0
ROT_DIM = 16
ATTN_SCALE = HEAD_DIM ** -0.5
NEG_BIG = -1e30
HG_HEADS = 4
HG_DK = 128
HG_DV = 128
HG_WIDTH = 512
HG_CHUNK = 64
LB_TINY = 1e-30
CONV_K = 3
CONV_WIDTH = 512
D_FF = 5632
IN_WIDTH = 5120
NORM_EPS = 1e-6
SAMPLE_T = 4
SAMPLE_ROWS = 8

Q_BLK_1024 = 0
K_BLK_256 = 4
V_BLK_256 = 5
HQ_BLK_512, HF_BLK_512, HI_BLK_512, HG_BLK_512 = 3, 4, 5, 6
CB_BLK_512, CC_BLK_512, CH_BLK_512 = 7, 8, 9

VMEM_LIMIT = 56 * 1024 * 1024


def _cparams(*sem, vmem=VMEM_LIMIT):
    return pltpu.CompilerParams(dimension_semantics=sem, vmem_limit_bytes=vmem)


def _rms(x, g):
    ms = jnp.mean(x * x, axis=-1, keepdims=True)
    return x * lax.rsqrt(ms + NORM_EPS) * g


def _silu(x):
    return x * (1.0 / (1.0 + jnp.exp(-x)))


def _dot(a, b):
    return jnp.dot(a, b, preferred_element_type=F32)


def _dot_nt(a, b):
    return lax.dot_general(a, b, (((1,), (1,)), ((), ())), preferred_element_type=F32)


def _dot_tn(a, b):
    return lax.dot_general(a, b, (((0,), (0,)), ((), ())), preferred_element_type=F32)


def _in_proj_sample_kernel(x_ref, g_ref, w_ref, o_ref, wb_ref, h_ref):
    @pl.when(pl.program_id(0) == 0)
    def _():
        h_ref[...] = _rms(x_ref[...], g_ref[...]).astype(BF16)

    wb = w_ref[...].astype(BF16)
    wb_ref[...] = wb
    o_ref[...] = _dot(h_ref[...], wb)


def in_proj_sample(x, g, w, layer, tn):
    m, k = x.shape
    n = w.shape[2]
    return pl.pallas_call(
        _in_proj_sample_kernel,
        grid=(n // tn,),
        in_specs=[
            pl.BlockSpec((m, k), lambda j: (0, 0)),
            pl.BlockSpec((None, 1, k), lambda j: (layer, 0, 0)),
            pl.BlockSpec((None, k, tn), lambda j: (layer, 0, j)),
        ],
        out_specs=[pl.BlockSpec((m, tn), lambda j: (0, j)), pl.BlockSpec((k, tn), lambda j: (0, j))],
        out_shape=[jax.ShapeDtypeStruct((m, n), F32), jax.ShapeDtypeStruct((k, n), BF16)],
        scratch_shapes=[pltpu.VMEM((m, k), BF16)],
        compiler_params=_cparams("arbitrary"),
        name="in_proj_sample",
    )(x, g, w)


NORM_ROW_SPLIT = 4


def _in_proj_kernel(x_ref, g_ref, w_ref, o_ref, h_ref):
    j = pl.program_id(1)
    rs = x_ref.shape[0] // NORM_ROW_SPLIT

    @pl.when(j == 0)
    def _():
        for r in range(NORM_ROW_SPLIT):
            rows = slice(r * rs, (r + 1) * rs)
            h = _rms(x_ref[rows, :], g_ref[...]).astype(BF16)
            h_ref[rows, :] = h
            o_ref[rows, :] = _dot(h, w_ref[...])

    @pl.when(j > 0)
    def _():
        o_ref[...] = _dot(h_ref[...], w_ref[...])


def in_proj(x, g, wb, layer, tm, tn):
    m, k = x.shape
    n = wb.shape[1]
    return pl.pallas_call(
        _in_proj_kernel,
        grid=(m // tm, n // tn),
        in_specs=[
            pl.BlockSpec((tm, k), lambda i, j: (i, 0)),
            pl.BlockSpec((None, 1, k), lambda i, j: (layer, 0, 0)),
            pl.BlockSpec((k, tn), lambda i, j: (0, j)),
        ],
        out_specs=pl.BlockSpec((tm, tn), lambda i, j: (i, j)),
        out_shape=jax.ShapeDtypeStruct((m, n), F32),
        scratch_shapes=[pltpu.VMEM((tm, k), BF16)],
        compiler_params=_cparams("parallel", "arbitrary", vmem=FFN_CONVERT_VMEM),
        name="in_proj",
    )(x, g, wb)


def _mix_dot(a, o, c, w):
    acc = _dot(a.astype(BF16), w[0:ATTN_WIDTH])
    acc = acc + _dot(o.astype(BF16), w[ATTN_WIDTH:ATTN_WIDTH + HG_WIDTH])
    return acc + _dot(c.astype(BF16), w[ATTN_WIDTH + HG_WIDTH:])


def _out_proj_sample_kernel(x_ref, a_ref, o_ref, c_ref, w_ref, y_ref, *wb_ref):
    w = w_ref[...]
    if wb_ref:
        w = w.astype(BF16)
        wb_ref[0][...] = w
    y_ref[...] = x_ref[...] + _mix_dot(a_ref[...], o_ref[...], c_ref[...], w)


def out_proj_sample(x, a, o, c, w_out, layer, tn):
    m, d = x.shape
    convert = w_out.ndim == 3
    kw = w_out.shape[-2]
    full = lambda width: pl.BlockSpec((m, width), lambda j: (0, 0))
    y_spec = pl.BlockSpec((m, tn), lambda j: (0, j))
    y_shape = jax.ShapeDtypeStruct((m, d), F32)
    if convert:
        w_spec = pl.BlockSpec((None, kw, tn), lambda j: (layer, 0, j))
        out_specs = [y_spec, pl.BlockSpec((kw, tn), lambda j: (0, j))]
        out_shape = [y_shape, jax.ShapeDtypeStruct((kw, d), BF16)]
    else:
        w_spec = pl.BlockSpec((kw, tn), lambda j: (0, j))
        out_specs, out_shape = y_spec, y_shape
    return pl.pallas_call(
        _out_proj_sample_kernel,
        grid=(d // tn,),
        in_specs=[y_spec, full(ATTN_WIDTH), full(HG_WIDTH), full(CONV_WIDTH), w_spec],
        out_specs=out_specs,
        out_shape=out_shape,
        compiler_params=_cparams("parallel"),
        name="out_proj_sample",
    )(x, a, o, c, w_out)


def _conv_taps(u, prev1, prev2, w_ref):
    return w_ref[0:1, :] * prev2 + w_ref[1:2, :] * prev1 + w_ref[2:3, :] * u


def _out_proj_kernel(tiles_per_seq, x_ref, a_ref, o_ref, cb_ref, cc_ref, ch_ref, cw_ref, w_ref,
                     y_ref, cs_ref, carry_ref):
    i = pl.program_id(0)
    tm = cc_ref.shape[0]

    @pl.when(i % tiles_per_seq == 0)
    def _():
        carry_ref[...] = jnp.zeros_like(carry_ref)

    u = cc_ref[...] * ch_ref[...]
    row = lax.broadcasted_iota(jnp.int32, u.shape, 0)
    last1 = carry_ref[7:8, :]
    last2 = carry_ref[6:7, :]
    um1 = jnp.where(row == 0, last1, pltpu.roll(u, 1, 0))
    um2 = jnp.where(row == 0, last2, jnp.where(row == 1, last1, pltpu.roll(u, 2, 0)))
    c = cb_ref[...] * _conv_taps(u, um1, um2, cw_ref)
    carry_ref[...] = u[tm - 8:tm]

    y_ref[...] = x_ref[...] + _mix_dot(a_ref[...], o_ref[...], c, w_ref)

    @pl.when(i % tiles_per_seq == tiles_per_seq - 1)
    def _():
        cs_ref[0] = carry_ref[8 - (CONV_K - 1):8, :]


def out_proj(x, a, o, proj, conv_w, wb, layer, seq_len, tm):
    m, d = x.shape
    tiles_per_seq = seq_len // tm
    col = lambda blk: pl.BlockSpec((tm, CONV_WIDTH), lambda i: (i, blk))
    return pl.pallas_call(
        functools.partial(_out_proj_kernel, tiles_per_seq),
        grid=(m // tm,),
        in_specs=[
            pl.BlockSpec((tm, d), lambda i: (i, 0)),
            pl.BlockSpec((tm, ATTN_WIDTH), lambda i: (i, 0)),
            pl.BlockSpec((tm, HG_WIDTH), lambda i: (i, 0)),
            col(CB_BLK_512), col(CC_BLK_512), col(CH_BLK_512),
            pl.BlockSpec((None, CONV_K, CONV_WIDTH), lambda i: (layer, 0, 0)),
            pl.BlockSpec(wb.shape, lambda i: (0, 0)),
        ],
        out_specs=[pl.BlockSpec((tm, d), lambda i: (i, 0)),
                   pl.BlockSpec((1, CONV_K - 1, CONV_WIDTH), lambda i: (i // tiles_per_seq, 0, 0))],
        out_shape=[jax.ShapeDtypeStruct((m, d), F32),
                   jax.ShapeDtypeStruct((m // seq_len, CONV_K - 1, CONV_WIDTH), F32)],
        scratch_shapes=[pltpu.VMEM((8, CONV_WIDTH), F32)],
        compiler_params=_cparams("arbitrary"),
        name="out_proj",
    )(x, a, o, proj, proj, proj, conv_w, wb)


def _ffn_step(h_ref, wg, wu, wd, y_ref):
    h = h_ref[...]
    gate = _dot(h, wg)
    up = _dot(h, wu)
    act = _silu(gate) * up
    y_ref[...] += _dot(act.astype(BF16), wd)


def _ffn_begin(x_ref, g_ref, y_ref, h_ref):
    x = x_ref[...]
    h_ref[...] = _rms(x, g_ref[...]).astype(BF16)
    y_ref[...] = x


def _ffn_sample_kernel(last_layer, x_ref, g_ref, wg_ref, wu_ref, wd_ref, gf_ref,
                       y_ref, wgb_ref, wub_ref, wdb_ref, h_ref):
    f = pl.program_id(0)

    @pl.when(f == 0)
    def _():
        _ffn_begin(x_ref, g_ref, y_ref, h_ref)

    wg = wg_ref[...].astype(BF16)
    wu = wu_ref[...].astype(BF16)
    wd = wd_ref[...].astype(BF16)
    wgb_ref[...] = wg
    wub_ref[...] = wu
    wdb_ref[...] = wd
    _ffn_step(h_ref, wg, wu, wd, y_ref)

    if last_layer:
        @pl.when(f == pl.num_programs(0) - 1)
        def _():
            y_ref[...] = _rms(y_ref[...], gf_ref[...])


def ffn_sample(x, g, w_gate_up, w_down, g_final, layer, tf):
    m, d = x.shape
    nf = D_FF // tf
    return pl.pallas_call(
        functools.partial(_ffn_sample_kernel, layer == DEPTH - 1),
        grid=(nf,),
        in_specs=[
            pl.BlockSpec((m, d), lambda f: (0, 0)),
            pl.BlockSpec((None, 1, d), lambda f: (layer, 0, 0)),
            pl.BlockSpec((None, d, tf), lambda f: (layer, 0, f)),
            pl.BlockSpec((None, d, tf), lambda f: (layer, 0, f + nf)),
            pl.BlockSpec((None, tf, d), lambda f: (layer, f, 0)),
            pl.BlockSpec((1, d), lambda f: (0, 0)),
        ],
        out_specs=[
            pl.BlockSpec((m, d), lambda f: (0, 0)),
            pl.BlockSpec((d, tf), lambda f: (0, f)),
            pl.BlockSpec((d, tf), lambda f: (0, f)),
            pl.BlockSpec((tf, d), lambda f: (f, 0)),
        ],
        out_shape=[jax.ShapeDtypeStruct((m, d), F32), jax.ShapeDtypeStruct((d, D_FF), BF16),
                   jax.ShapeDtypeStruct((d, D_FF), BF16), jax.ShapeDtypeStruct((D_FF, d), BF16)],
        scratch_shapes=[pltpu.VMEM((m, d), BF16)],
        compiler_params=_cparams("arbitrary"),
        name="ffn_sample",
    )(x, g, w_gate_up, w_gate_up, w_down, g_final)


def _ffn_kernel(last_layer, x_ref, g_ref, wg_ref, wu_ref, wd_ref, gf_ref, y_ref, h_ref, side=lambda: None):
    f = pl.program_id(1)
    rs = x_ref.shape[0] // NORM_ROW_SPLIT

    @pl.when(f == 0)
    def _():
        side()
        for r in range(NORM_ROW_SPLIT):
            rows = slice(r * rs, (r + 1) * rs)
            x = x_ref[rows, :]
            h = _rms(x, g_ref[...]).astype(BF16)
            h_ref[rows, :] = h
            act = _silu(_dot(h, wg_ref[...])) * _dot(h, wu_ref[...])
            y_ref[rows, :] = x + _dot(act.astype(BF16), wd_ref[...])

    @pl.when(f > 0)
    def _():
        side()
        _ffn_step(h_ref, wg_ref[...], wu_ref[...], wd_ref[...], y_ref)

    if last_layer:
        @pl.when(f == pl.num_programs(1) - 1)
        def _():
            y_ref[...] = _rms(y_ref[...], gf_ref[...])


def _side_cast(w, slab, axis, grid):
    nsteps = grid[0] * grid[1]
    shape = list(w.shape[1:])
    assert shape[axis] % nsteps == 0
    shape[axis] //= nsteps
    pos = lambda i, j: i * grid[1] + j
    idx = (lambda i, j: (pos(i, j), 0)) if axis == 0 else (lambda i, j: (0, pos(i, j)))
    src = pl.BlockSpec((None, *shape), lambda i, j: (slab, *idx(i, j)))
    return src, pl.BlockSpec(tuple(shape), idx), jax.ShapeDtypeStruct(w.shape[1:], BF16)


def _cast_sides(src_refs, dst_refs):
    for src, dst in zip(src_refs, dst_refs):
        dst[...] = src[...].astype(BF16)


FFN_CONVERT_VMEM = 60000 * 1024


def _ffn_convert_kernel(x_ref, g_ref, wg_ref, wu_ref, wd_ref, gf_ref, side_src, y_ref, side_dst, h_ref):
    _ffn_kernel(False, x_ref, g_ref, wg_ref, wu_ref, wd_ref, gf_ref, y_ref, h_ref,
                side=lambda: _cast_sides([side_src], [side_dst]))


def ffn(x, g, w_gate, gate_off, w_up, up_off, wdb, g_final, layer, tm, tf, convert=None):
    m, d = x.shape
    grid = (m // tm, D_FF // tf)
    in_specs = [
        pl.BlockSpec((tm, d), lambda i, f: (i, 0)),
        pl.BlockSpec((None, 1, d), lambda i, f: (layer, 0, 0)),
        pl.BlockSpec((d, tf), lambda i, f: (0, gate_off + f)),
        pl.BlockSpec((d, tf), lambda i, f: (0, up_off + f)),
        pl.BlockSpec((tf, d), lambda i, f: (f, 0)),
        pl.BlockSpec((1, d), lambda i, f: (0, 0)),
    ]
    y_spec = pl.BlockSpec((tm, d), lambda i, f: (i, 0))
    y_shape = jax.ShapeDtypeStruct((m, d), F32)
    if convert is None:
        return pl.pallas_call(
            functools.partial(_ffn_kernel, layer == DEPTH - 1),
            grid=grid, in_specs=in_specs, out_specs=y_spec, out_shape=y_shape,
            scratch_shapes=[pltpu.VMEM((tm, d), BF16)],
            compiler_params=_cparams("parallel", "arbitrary"),
            name="ffn",
        )(x, g, w_gate, w_up, wdb, g_final)

    src, dst, dst_shape = _side_cast(convert, layer + 1, 1, grid)
    return pl.pallas_call(
        _ffn_convert_kernel,
        grid=grid,
        in_specs=in_specs + [src],
        out_specs=[y_spec, dst],
        out_shape=[y_shape, dst_shape],
        scratch_shapes=[pltpu.VMEM((tm, d), BF16)],
        compiler_params=_cparams("parallel", "arbitrary", vmem=FFN_CONVERT_VMEM),
        name="ffn_convert",
    )(x, g, w_gate, w_up, wdb, g_final, convert)


def _rope_tables(pos):
    half = ROT_DIM // 2
    inv = ROPE_THETA ** (-jnp.arange(half, dtype=F32) * 2.0 / ROT_DIM)
    ang = pos.astype(F32)[:, None] * inv[None, :]
    cos, sin = jnp.cos(ang), jnp.sin(ang)
    n = pos.shape[0]
    ones = jnp.ones((n, HEAD_DIM - ROT_DIM), F32)
    zeros = jnp.zeros((n, HEAD_DIM - ROT_DIM), F32)
    zh = jnp.zeros((n, half), F32)
    c = jnp.concatenate([cos, cos, ones], axis=-1)
    s_lo = jnp.concatenate([-sin, zh, zeros], axis=-1)
    s_hi = jnp.concatenate([zh, sin, zeros], axis=-1)
    rep = lambda t: jnp.concatenate([t, t], axis=-1)
    return rep(c), rep(s_lo), rep(s_hi)


def _rope(x, c, s_lo, s_hi):
    half = ROT_DIM // 2
    outs = []
    for j in range(x.shape[1] // 128):
        xc = x[:, j * 128:(j + 1) * 128]
        outs.append(xc * c + pltpu.roll(xc, 128 - half, 1) * s_lo + pltpu.roll(xc, half, 1) * s_hi)
    return outs[0] if len(outs) == 1 else jnp.concatenate(outs, axis=1)


def _rope_tables_t(pos):
    half = ROT_DIM // 2
    inv = ROPE_THETA ** (-jnp.arange(half, dtype=F32) * 2.0 / ROT_DIM)
    ang = inv[:, None] * pos.astype(F32)[None, :]
    return jnp.cos(ang), jnp.sin(ang)


def _rope_t(xt, cos_t, sin_t):
    half = ROT_DIM // 2
    pieces = []
    for base in range(0, xt.shape[0], HEAD_DIM):
        x1 = xt[base:base + half]
        x2 = xt[base + half:base + ROT_DIM]
        pieces += [x1 * cos_t - x2 * sin_t, x2 * cos_t + x1 * sin_t, xt[base + ROT_DIM:base + HEAD_DIM]]
    return jnp.concatenate(pieces, axis=0)


def _softmax_sink_pv(s, sk, v):
    m = jnp.maximum(jnp.max(s, axis=-1, keepdims=True), sk)
    p = jnp.exp(s - m)
    denom = jnp.sum(p, axis=-1, keepdims=True) + jnp.exp(sk - m)
    return _dot(p.astype(BF16), v) / denom


N_ATTN_PROMPT_IN, N_ATTN_PROMPT_OUT = 9, 3


def _attn_prompt_kernel(layer, nside, *refs):
    it = iter(refs)
    take = lambda cnt: [next(it) for _ in range(cnt)]
    sink_ref, q_ref, k_ref, v_ref, c_ref, slo_ref, shi_ref, ct_ref, st_ref = take(N_ATTN_PROMPT_IN)
    side_src = take(nside)
    a_ref, ko_ref, vo_ref = take(N_ATTN_PROMPT_OUT)
    side_dst = take(nside)
    kprev_ref, vtprev_ref = take(2)

    n = pl.program_id(1)
    nstep = pl.num_programs(1)
    blk = WINDOW
    nsub = q_ref.shape[1] // blk
    lanes = Q_PER_KV * blk
    sub = lambda j: slice(j * blk, (j + 1) * blk)

    k_all = _rope(k_ref[0], c_ref[...], slo_ref[...], shi_ref[...])
    v_all = v_ref[0]
    vt_all = v_all.T
    qt_all = (_rope_t(q_ref[0].T, ct_ref[...], st_ref[...]) * ATTN_SCALE).astype(BF16)

    key = lax.broadcasted_iota(jnp.int32, (blk, lanes), 0)
    qry = lax.broadcasted_iota(jnp.int32, (blk, lanes), 1) % blk
    newer = key > qry
    zero_rows = jnp.zeros((HEAD_DIM, lanes), BF16)

    def scores(h, j, first):
        pair = slice((h // 2) * 2 * HEAD_DIM, (h // 2 + 1) * 2 * HEAD_DIM)
        qt = qt_all[:, sub(j)]
        qh = jnp.concatenate(
            [qt[(Q_PER_KV * h + g) * HEAD_DIM:(Q_PER_KV * h + g + 1) * HEAD_DIM] for g in range(Q_PER_KV)], axis=1)
        qz = jnp.concatenate([qh, zero_rows] if h % 2 == 0 else [zero_rows, qh], axis=0)
        s_cur = _dot(k_all[sub(j), pair].astype(BF16), qz)
        if first:
            return None, s_cur
        k_prev = kprev_ref[:, pair] if j == 0 else k_all[sub(j - 1), pair]
        return _dot(k_prev.astype(BF16), qz), s_cur

    def softmax(h, s_prev, s_cur):
        sink = jnp.concatenate(
            [jnp.full((1, blk), sink_ref[layer, Q_PER_KV * h + g], F32) for g in range(Q_PER_KV)], axis=1)
        if s_prev is None:
            f = jnp.where(newer, NEG_BIG, s_cur)
            m = jnp.maximum(jnp.max(f, axis=0, keepdims=True), sink)
            p = jnp.exp(f - m)
            denom = jnp.sum(p, axis=0, keepdims=True) + jnp.exp(sink - m)
            return p.astype(BF16), None, denom
        f = jnp.where(newer, s_prev, s_cur)
        d = jnp.sum(jnp.where(key == qry, s_prev, 0.0), axis=0, keepdims=True)
        m = jnp.maximum(jnp.maximum(jnp.max(f, axis=0, keepdims=True), d), sink)
        p = jnp.exp(f - m)
        pd = jnp.exp(d - m)
        denom = jnp.sum(p, axis=0, keepdims=True) + pd + jnp.exp(sink - m)
        pp = jnp.concatenate([jnp.where(newer, p, 0.0), jnp.where(newer, 0.0, p)], axis=0).astype(BF16)
        return pp, pd, denom

    def weighted_values(h, j, pp, pd, denom):
        rows = slice(h * HEAD_DIM, (h + 1) * HEAD_DIM)
        vt_cur = vt_all[rows, sub(j)]
        inv = 1.0 / denom
        if pd is None:
            return _dot(vt_cur.astype(BF16), pp) * inv
        vt_prev = vtprev_ref[rows, :] if j == 0 else vt_all[rows, sub(j - 1)]
        o = _dot(jnp.concatenate([vt_prev, vt_cur], axis=1).astype(BF16), pp)
        return (o + pd * jnp.concatenate([vt_prev] * Q_PER_KV, axis=1)) * inv

    def run(first_step):
        _cast_sides(side_src, side_dst)
        units = [(h, j) for j in range(nsub) for h in range(N_KV_HEADS)]
        s = {u: scores(*u, first_step and u[1] == 0) for u in units}
        p = {u: softmax(u[0], *s[u]) for u in units}
        o = {u: weighted_values(*u, *p[u]) for u in units}
        for j in range(nsub):
            outs = [o[h, j][:, g * blk:(g + 1) * blk] for h in range(N_KV_HEADS) for g in range(Q_PER_KV)]
            a_ref[0, sub(j), :] = jnp.concatenate(outs, axis=0).T.astype(a_ref.dtype)

    @pl.when(n == 0)
    def _():
        run(True)

    @pl.when(n > 0)
    def _():
        run(False)

    kprev_ref[...] = k_all[sub(nsub - 1)]
    vtprev_ref[...] = vt_all[:, sub(nsub - 1)]

    @pl.when(n == nstep - 1)
    def _():
        ko_ref[0] = k_all[sub(nsub - 1)]
        vo_ref[0] = v_all[sub(nsub - 1)]


def attn_prompt(proj, sink, tables, tables_t, layer, rows, convert=()):
    b, l, _ = proj.shape
    blk = WINDOW
    grid = (b, l // rows)
    tab_spec = pl.BlockSpec((rows, 128), lambda i, n: (n, 0))
    tab_t_spec = pl.BlockSpec((ROT_DIM // 2, rows), lambda i, n: (0, n))
    kv_out = jax.ShapeDtypeStruct((b, blk, KV_WIDTH), F32)
    sides = [_side_cast(w, layer + 1, 0, grid) for w in convert]
    in_specs = [
        pl.BlockSpec(memory_space=pltpu.SMEM),
        pl.BlockSpec((1, rows, ATTN_WIDTH), lambda i, n: (i, n, Q_BLK_1024)),
        pl.BlockSpec((1, rows, KV_WIDTH), lambda i, n: (i, n, K_BLK_256)),
        pl.BlockSpec((1, rows, KV_WIDTH), lambda i, n: (i, n, V_BLK_256)),
        tab_spec, tab_spec, tab_spec, tab_t_spec, tab_t_spec,
    ]
    out_specs = [
        pl.BlockSpec((1, rows, ATTN_WIDTH), lambda i, n: (i, n, 0)),
        pl.BlockSpec((1, blk, KV_WIDTH), lambda i, n: (i, 0, 0)),
        pl.BlockSpec((1, blk, KV_WIDTH), lambda i, n: (i, 0, 0)),
    ]
    assert (len(in_specs), len(out_specs)) == (N_ATTN_PROMPT_IN, N_ATTN_PROMPT_OUT)
    return pl.pallas_call(
        functools.partial(_attn_prompt_kernel, layer, len(sides)),
        grid=grid,
        in_specs=in_specs + [s[0] for s in sides],
        out_specs=out_specs + [s[1] for s in sides],
        out_shape=[jax.ShapeDtypeStruct((b, l, ATTN_WIDTH), BF16), kv_out, kv_out] + [s[2] for s in sides],
        scratch_shapes=[pltpu.VMEM((blk, KV_WIDTH), F32), pltpu.VMEM((KV_WIDTH, blk), F32)],
        compiler_params=_cparams("parallel", "arbitrary"),
        name="attn_prompt",
    )(sink, proj, proj, proj, *tables, *tables_t, *convert)


def _attn_sample_kernel(layer, sink_ref, q_ref, k_ref, v_ref, kc_ref, vc_ref, c_ref, slo_ref, shi_ref,
                        k_carried_ref, v_carried_ref, a_ref, ko_ref, vo_ref):
    del k_carried_ref, v_carried_ref
    r8 = SAMPLE_ROWS
    nseq = r8 // SAMPLE_T
    lb = kc_ref.shape[1]
    c, s_lo, s_hi = c_ref[...], slo_ref[...], shi_ref[...]
    q = _rope(q_ref[...], c, s_lo, s_hi)
    k = _rope(k_ref[...], c, s_lo, s_hi)
    v = v_ref[...]
    pad = jnp.zeros((lb - r8, KV_WIDTH), F32)
    k_pad = jnp.concatenate([k, pad], axis=0)
    v_pad = jnp.concatenate([v, pad], axis=0)

    zeros = jnp.zeros((r8, HEAD_DIM), F32)
    qx = jnp.concatenate(
        [jnp.concatenate([q[:, hg * HEAD_DIM:(hg + 1) * HEAD_DIM] if slot == hg // Q_PER_KV else zeros
                          for slot in range(N_KV_HEADS)], axis=1) for hg in range(N_Q_HEADS)],
        axis=0).astype(BF16)
    sk = jnp.concatenate([jnp.full((r8, 1), sink_ref[layer, hg], F32) for hg in range(N_Q_HEADS)], axis=0)

    rows = N_Q_HEADS * r8
    ri = lax.broadcasted_iota(jnp.int32, (rows, 2 * lb), 0) % r8
    t = ri % SAMPLE_T
    kj = lax.broadcasted_iota(jnp.int32, (rows, 2 * lb), 1)
    cj = kj - lb
    row_seq = lax.broadcasted_iota(jnp.int32, (rows, KV_WIDTH), 0) % r8 // SAMPLE_T

    s = []
    for e in range(nseq):
        kk = jnp.concatenate([kc_ref[e], k_pad], axis=0).astype(BF16)
        s.append(_dot_nt(qx, kk) * ATTN_SCALE)
    o = None
    for e in range(nseq):
        mask = ((kj < lb) & (kj >= t)) | (
            (cj >= e * SAMPLE_T) & (cj < (e + 1) * SAMPLE_T) & (cj - e * SAMPLE_T <= t))
        vv = jnp.concatenate([vc_ref[e], v_pad], axis=0).astype(BF16)
        oe = _softmax_sink_pv(jnp.where(mask, s[e], NEG_BIG), sk, vv)
        o = oe if o is None else jnp.where(row_seq == e, oe, o)
    a_ref[...] = jnp.concatenate(
        [o[hg * r8:(hg + 1) * r8, (hg // Q_PER_KV) * HEAD_DIM:(hg // Q_PER_KV + 1) * HEAD_DIM]
         for hg in range(N_Q_HEADS)], axis=1)

    row = lax.broadcasted_iota(jnp.int32, (lb, KV_WIDTH), 0)
    for e in range(nseq):
        shift = (lb - SAMPLE_T - e * SAMPLE_T) % lb
        for new_pad, cache_ref, out_ref in ((k_pad, kc_ref, ko_ref), (v_pad, vc_ref, vo_ref)):
            new_rows = new_pad if shift == 0 else pltpu.roll(new_pad, shift, 0)
            old_rows = pltpu.roll(cache_ref[e], lb - SAMPLE_T, 0)
            out_ref[e] = jnp.where(row >= lb - SAMPLE_T, new_rows, old_rows)


def _lower_bound(lbr_ref, layer, hs):
    rows = [lbr_ref[r:r + 1, hs] for r in range(DEPTH)]
    mx = functools.reduce(jnp.maximum, rows)
    es = [jnp.exp(r - mx) for r in rows]
    tot = functools.reduce(lambda a, b: a + b, es)
    lbp = [e / tot for e in es]
    acc = lbp[0]
    for r in range(1, layer + 1):
        acc = acc + lbp[r]
    return acc - lbp[0]


def _hgrn_gates(hq, hf, lb):
    q = _silu(hq) * (HG_DK ** -0.5)
    e = jnp.exp(-jnp.abs(hf))
    r = 1.0 / (1.0 + e)
    pos = hf >= 0.0
    sig_pos = jnp.where(pos, r, e * r)
    sig_neg = jnp.where(pos, e * r, r)
    log_f = jnp.log(jnp.maximum(lb, LB_TINY) + (1.0 - lb) * sig_pos)
    k = (1.0 - lb) * sig_neg
    return q, log_f, k


def _hgrn_finish(o, g_norm, hg):
    return _rms(o, g_norm) * _silu(hg)


def _hgrn_levels():
    s = HG_CHUNK // 2
    while s >= 1:
        yield s
        s //= 2


def _hgrn_level_masks():
    t = np.arange(HG_CHUNK)[:, None]
    s = np.arange(HG_CHUNK)[None, :]
    return np.stack([((t % (2 * h) >= h) & (t // (2 * h) == s // (2 * h)) & (s % (2 * h) < h))
                     for h in _hgrn_levels()]).astype(np.float32)


def _hgrn_pair_reference(g_ref, gcum, half, sub8):
    c = HG_CHUNK
    pair = 2 * half
    bcast = lambda r: jnp.broadcast_to(g_ref[r:r + 1, :], (8, HG_DK))
    if pair >= 8:
        return jnp.concatenate([bcast((8 * v // pair) * pair + half - 1) for v in range(c // 8)], axis=0)
    if pair == 4:
        lo = jnp.concatenate([bcast(8 * v + 1) for v in range(c // 8)], axis=0)
        hi = jnp.concatenate([bcast(8 * v + 5) for v in range(c // 8)], axis=0)
        return jnp.where(sub8 < 4, lo, hi)
    return jnp.where(sub8 % 2 == 1, pltpu.roll(gcum, 1, 0), gcum)


def _hgrn_level_operand(q, k, gcum, g_ref, half, row):
    ref = _hgrn_pair_reference(g_ref, gcum, half, row % 8)
    right = row % (2 * half) >= half
    return (jnp.where(right, q, k) * jnp.exp(-jnp.abs(gcum - ref))).astype(BF16)


def _chunk_cumsum(x, buf, row):
    sh = 1
    while sh < HG_CHUNK:
        x = x + jnp.where(row >= sh, pltpu.roll(x, sh, 0), 0.0)
        sh *= 2
    buf[...] = x
    return x


def _hgrn_tile(layer, hq_ref, hf_ref, hi_ref, hg_ref, lbr_ref, gn_ref, m_ref, g_ref, state, emit):
    c = HG_CHUNK
    nchunk = hq_ref.shape[1] // c
    heads = range(HG_HEADS)
    units = [(h, cc) for cc in range(nchunk) for h in heads]
    row = lax.broadcasted_iota(jnp.int32, (c, HG_DK), 0)
    cols = lambda h: slice(h * HG_DK, (h + 1) * HG_DK)
    rws = lambda cc: slice(cc * c, (cc + 1) * c)
    gbuf = lambda h, cc: g_ref.at[h * nchunk + cc]
    lbs = [_lower_bound(lbr_ref, layer, cols(h)) for h in heads]

    q, k, v, gcum = {}, {}, {}, {}
    for u in units:
        h, cc = u
        q[u], log_f, k[u] = _hgrn_gates(hq_ref[0, rws(cc), cols(h)], hf_ref[0, rws(cc), cols(h)], lbs[h])
        v[u] = hi_ref[0, rws(cc), cols(h)]
        gcum[u] = _chunk_cumsum(log_f, gbuf(h, cc), row)

    a = {u: jnp.zeros((c, c), F32) for u in units}
    for lvl, half in enumerate(_hgrn_levels()):
        for u in units:
            x = _hgrn_level_operand(q[u], k[u], gcum[u], gbuf(*u), half, row)
            a[u] = a[u] + _dot_nt(x, x) * m_ref[lvl]

    o_intra, qs, ks, decay = {}, {}, {}, {}
    for u in units:
        vb = v[u].astype(BF16)
        o_intra[u] = _dot(a[u].astype(BF16), vb) + jnp.sum(q[u] * k[u], axis=-1, keepdims=True) * v[u]
        g_last = gbuf(*u)[c - 1:c, :]
        qs[u] = (q[u] * jnp.exp(gcum[u])).astype(BF16)
        ks[u] = (k[u] * jnp.exp(g_last - gcum[u])).astype(BF16)
        decay[u] = jnp.transpose(jnp.broadcast_to(jnp.exp(g_last), (HG_DK, HG_DK)))

    state = list(state)
    for cc in range(nchunk):
        o_inter = [_dot(qs[h, cc], state[h].astype(BF16)) for h in heads]
        update = [_dot_tn(ks[h, cc], v[h, cc].astype(BF16)) for h in heads]
        for h in heads:
            o = o_inter[h] + o_intra[h, cc]
            emit(h, rws(cc), _hgrn_finish(o, gn_ref[h:h + 1, :], hg_ref[0, rws(cc), cols(h)]))
            state[h] = state[h] * decay[h, cc] + update[h]
    return state


N_HGRN_PROMPT_IN, N_HGRN_PROMPT_OUT = 7, 2


def _hgrn_prompt_kernel(layer, nside, *refs):
    it = iter(refs)
    take = lambda cnt: [next(it) for _ in range(cnt)]
    hq_ref, hf_ref, hi_ref, hg_ref, lbr_ref, gn_ref, m_ref = take(N_HGRN_PROMPT_IN)
    side_src = take(nside)
    o_ref, so_ref = take(N_HGRN_PROMPT_OUT)
    side_dst = take(nside)
    s_ref, g_ref = take(2)
    step = pl.program_id(1)
    heads = range(HG_HEADS)

    @pl.when(step == 0)
    def _():
        s_ref[...] = jnp.zeros_like(s_ref)

    def emit(h, rs, o):
        o_ref[0, rs, h * HG_DV:(h + 1) * HG_DV] = o.astype(o_ref.dtype)

    _cast_sides(side_src, side_dst)
    state = _hgrn_tile(layer, hq_ref, hf_ref, hi_ref, hg_ref, lbr_ref, gn_ref, m_ref, g_ref,
                       [s_ref[h] for h in heads], emit)
    for h in heads:
        s_ref[h] = state[h]

    @pl.when(step == pl.num_programs(1) - 1)
    def _():
        so_ref[0] = s_ref[...]


def hgrn_prompt(proj, lb_raw, g_norm, layer, rows, convert=()):
    b, l, _ = proj.shape
    c = HG_CHUNK
    grid = (b, l // rows)
    masks = jnp.asarray(_hgrn_level_masks())
    col = lambda blk: pl.BlockSpec((1, rows, HG_WIDTH), lambda i, n: (i, n, blk))
    sides = [_side_cast(w, layer + 1, 0, grid) for w in convert]
    in_specs = [
        col(HQ_BLK_512), col(HF_BLK_512), col(HI_BLK_512), col(HG_BLK_512),
        pl.BlockSpec((DEPTH, HG_WIDTH), lambda i, n: (0, 0)),
        pl.BlockSpec((None, HG_HEADS, HG_DV), lambda i, n: (layer, 0, 0)),
        pl.BlockSpec(masks.shape, lambda i, n: (0, 0, 0)),
    ]
    out_specs = [
        pl.BlockSpec((1, rows, HG_WIDTH), lambda i, n: (i, n, 0)),
        pl.BlockSpec((1, HG_HEADS, HG_DK, HG_DV), lambda i, n: (i, 0, 0, 0)),
    ]
    assert (len(in_specs), len(out_specs)) == (N_HGRN_PROMPT_IN, N_HGRN_PROMPT_OUT)
    return pl.pallas_call(
        functools.partial(_hgrn_prompt_kernel, layer, len(sides)),
        grid=grid,
        in_specs=in_specs + [s[0] for s in sides],
        out_specs=out_specs + [s[1] for s in sides],
        out_shape=[jax.ShapeDtypeStruct((b, l, HG_WIDTH), BF16),
                   jax.ShapeDtypeStruct((b, HG_HEADS, HG_DK, HG_DV), F32)] + [s[2] for s in sides],
        scratch_shapes=[pltpu.VMEM((HG_HEADS, HG_DK, HG_DV), F32),
                        pltpu.VMEM((HG_HEADS * (rows // c), c, HG_DK), F32)],
        compiler_params=_cparams("parallel", "arbitrary"),
        name="hgrn_prompt",
    )(proj, proj, proj, proj, lb_raw, g_norm, masks, *convert)


def _hgrn_sample_kernel(layer, hq_ref, hf_ref, hi_ref, hg_ref, lbr_ref, gn_ref, s_ref, carried_ref,
                        o_ref, so_ref, q_ref, k_ref, g_ref):
    del carried_ref
    r8, tt = SAMPLE_ROWS, SAMPLE_T
    nseq = r8 // tt
    row = lax.broadcasted_iota(jnp.int32, (r8, HG_DK), 0)
    t_idx = row % tt
    seq = row // tt

    def pick(ref, s):
        out = ref[s:s + 1, :]
        for e in range(1, nseq):
            out = jnp.where(seq == e, ref[e * tt + s:e * tt + s + 1, :], out)
        return out

    for h in range(HG_HEADS):
        hs = slice(h * HG_DK, (h + 1) * HG_DK)
        lb = _lower_bound(lbr_ref, layer, hs)
        q, log_f, k = _hgrn_gates(hq_ref[:, hs], hf_ref[:, hs], lb)
        gcum = log_f
        sh = 1
        while sh < tt:
            gcum = gcum + jnp.where(t_idx >= sh, pltpu.roll(gcum, sh, 0), 0.0)
            sh *= 2
        q_ref[...] = q
        k_ref[...] = k
        g_ref[...] = gcum
        v = hi_ref[:, hs]
        g_last = pick(g_ref, tt - 1)
        qs = (q * jnp.exp(gcum)).astype(BF16)
        ks_all = k * jnp.exp(g_last - gcum)

        o = jnp.zeros((r8, HG_DV), F32)
        for e in range(nseq):
            state = s_ref[e, h]
            o = jnp.where(seq == e, _dot(qs, state.astype(BF16)), o)
            g_last_e = g_ref[e * tt + tt - 1:e * tt + tt, :]
            decay = jnp.transpose(jnp.broadcast_to(jnp.exp(g_last_e), (HG_DK, HG_DK)))
            ks_e = jnp.where(seq == e, ks_all, 0.0).astype(BF16)
            so_ref[e, h] = state * decay + _dot_tn(ks_e, v.astype(BF16))
        for s in range(tt):
            dec = jnp.exp(jnp.where(t_idx >= s, gcum - pick(g_ref, s), NEG_BIG))
            col = jnp.sum(q * (dec * pick(k_ref, s)), axis=-1, keepdims=True)
            o = o + col * pick(hi_ref.at[:, hs], s)
        o_ref[:, hs] = _hgrn_finish(o, gn_ref[h:h + 1, :], hg_ref[:, hs])


def _conv_sample_kernel(cb_ref, cc_ref, ch_ref, w_ref, st_ref, y_ref, so_ref, u_ref):
    r8, tt = SAMPLE_ROWS, SAMPLE_T
    nseq = r8 // tt
    u = cc_ref[...] * ch_ref[...]
    u_ref[...] = u
    row = lax.broadcasted_iota(jnp.int32, u.shape, 0)
    t_idx = row % tt
    seq = row // tt
    last1 = st_ref[0, 1:2, :]
    last2 = st_ref[0, 0:1, :]
    for e in range(1, nseq):
        last1 = jnp.where(seq == e, st_ref[e, 1:2, :], last1)
        last2 = jnp.where(seq == e, st_ref[e, 0:1, :], last2)
    um1 = jnp.where(t_idx == 0, last1, pltpu.roll(u, 1, 0))
    um2 = jnp.where(t_idx == 0, last2, jnp.where(t_idx == 1, last1, pltpu.roll(u, 2, 0)))
    y_ref[...] = cb_ref[...] * _conv_taps(u, um1, um2, w_ref)
    for e in range(nseq):
        so_ref[e] = u_ref[(e + 1) * tt - (CONV_K - 1):(e + 1) * tt, :]


N_ATTN_SAMPLE_IN, N_HGRN_SAMPLE_IN, N_CONV_SAMPLE_IN = 11, 8, 5
N_ATTN_SAMPLE_OUT, N_HGRN_SAMPLE_OUT, N_CONV_SAMPLE_OUT = 3, 2, 2


def _mix_sample_kernel(layer, *refs):
    it = iter(refs)
    take = lambda n: [next(it) for _ in range(n)]
    attn_in, hgrn_in, conv_in = take(N_ATTN_SAMPLE_IN), take(N_HGRN_SAMPLE_IN), take(N_CONV_SAMPLE_IN)
    attn_out, hgrn_out, conv_out = take(N_ATTN_SAMPLE_OUT), take(N_HGRN_SAMPLE_OUT), take(N_CONV_SAMPLE_OUT)
    hgrn_scratch, conv_scratch = take(3), take(1)
    _attn_sample_kernel(layer, *attn_in, *attn_out)
    _hgrn_sample_kernel(layer, *hgrn_in, *hgrn_out, *hgrn_scratch)
    _conv_sample_kernel(*conv_in, *conv_out, *conv_scratch)


def mix_sample(proj, k_cache, v_cache, k_new, v_new, sink, tables, state, state_new, lb_raw, g_norm,
               conv_state, conv_w, layer):
    m = proj.shape[0]
    r8 = SAMPLE_ROWS
    nseq = r8 // SAMPLE_T
    lb = k_cache.shape[2]
    rows = lambda width, blk: pl.BlockSpec((r8, width), lambda i: (i, blk))
    tab_spec = pl.BlockSpec((r8, 128), lambda i: (0, 0))
    cache_spec = pl.BlockSpec((None, nseq, lb, KV_WIDTH), lambda i: (layer, i, 0, 0))
    st_spec = pl.BlockSpec((None, nseq, HG_HEADS, HG_DK, HG_DV), lambda i: (layer, i, 0, 0, 0))
    carried = pl.BlockSpec(memory_space=pl.ANY)
    attn_in = [pl.BlockSpec(memory_space=pltpu.SMEM),
               rows(ATTN_WIDTH, Q_BLK_1024), rows(KV_WIDTH, K_BLK_256), rows(KV_WIDTH, V_BLK_256),
               cache_spec, cache_spec, tab_spec, tab_spec, tab_spec, carried, carried]
    hgrn_in = [rows(HG_WIDTH, HQ_BLK_512), rows(HG_WIDTH, HF_BLK_512), rows(HG_WIDTH, HI_BLK_512),
               rows(HG_WIDTH, HG_BLK_512),
               pl.BlockSpec((DEPTH, HG_WIDTH), lambda i: (0, 0)),
               pl.BlockSpec((None, HG_HEADS, HG_DV), lambda i: (layer, 0, 0)),
               st_spec, carried]
    conv_in = [rows(CONV_WIDTH, CB_BLK_512), rows(CONV_WIDTH, CC_BLK_512), rows(CONV_WIDTH, CH_BLK_512),
               pl.BlockSpec((None, CONV_K, CONV_WIDTH), lambda i: (layer, 0, 0)),
               pl.BlockSpec((None, nseq, CONV_K - 1, CONV_WIDTH), lambda i: (layer, i, 0, 0))]
    assert (len(attn_in), len(hgrn_in), len(conv_in)) == (N_ATTN_SAMPLE_IN, N_HGRN_SAMPLE_IN, N_CONV_SAMPLE_IN)
    out_specs = [rows(ATTN_WIDTH, 0), cache_spec, cache_spec,
                 rows(HG_WIDTH, 0), st_spec,
                 rows(CONV_WIDTH, 0), pl.BlockSpec((nseq, CONV_K - 1, CONV_WIDTH), lambda i: (i, 0, 0))]
    out_shape = [jax.ShapeDtypeStruct((m, ATTN_WIDTH), F32),
                 jax.ShapeDtypeStruct(k_new.shape, F32), jax.ShapeDtypeStruct(v_new.shape, F32),
                 jax.ShapeDtypeStruct((m, HG_WIDTH), F32), jax.ShapeDtypeStruct(state_new.shape, F32),
                 jax.ShapeDtypeStruct((m, CONV_WIDTH), F32), jax.ShapeDtypeStruct(conv_state.shape[1:], F32)]
    return pl.pallas_call(
        functools.partial(_mix_sample_kernel, layer),
        grid=(m // r8,),
        in_specs=attn_in + hgrn_in + conv_in,
        out_specs=out_specs,
        out_shape=out_shape,
        input_output_aliases={9: 1, 10: 2, N_ATTN_SAMPLE_IN + N_HGRN_SAMPLE_IN - 1: 4},
        scratch_shapes=[pltpu.VMEM((r8, HG_DK), F32), pltpu.VMEM((r8, HG_DK), F32), pltpu.VMEM((r8, HG_DK), F32),
                        pltpu.VMEM((r8, CONV_WIDTH), F32)],
        compiler_params=_cparams("parallel"),
        name="mix_sample",
    )(sink, proj, proj, proj, k_cache, v_cache, *tables, k_new, v_new,
      proj, proj, proj, proj, lb_raw, g_norm, state, state_new,
      proj, proj, proj, conv_w, conv_state)


def kernel(x_prompt, x_sample, cache_attn_k, cache_attn_v, state_hgrn, state_conv, w_in, attn_sink,
           hgrn_lower_bounds, hgrn_norm, conv_w, w_out, norm_mix, norm_ffn, w_gate_up, w_down, norm_final):
    bp, lp, d = x_prompt.shape
    bs, ls, _ = x_sample.shape
    assert ls == SAMPLE_T and (bs * ls) % SAMPLE_ROWS == 0
    lb = cache_attn_k.shape[2]
    mp, ms = bp * lp, bs * ls

    pos_p = jnp.arange(lp, dtype=jnp.int32)
    tab_p = _rope_tables(pos_p)
    tab_pt = _rope_tables_t(pos_p)
    pos_s = PAST_LEN + jnp.arange(SAMPLE_ROWS, dtype=jnp.int32) % SAMPLE_T
    tab_s = _rope_tables(pos_s)

    xp = x_prompt.reshape(mp, d)
    xs = x_sample.reshape(ms, d)
    kc = cache_attn_k.reshape(DEPTH, bs, lb, KV_WIDTH)
    vc = cache_attn_v.reshape(DEPTH, bs, lb, KV_WIDTH)

    g_mix = norm_mix.reshape(DEPTH, 1, d)
    g_ffn = norm_ffn.reshape(DEPTH, 1, d)
    g_final = norm_final.reshape(1, d)

    k_s = jnp.zeros(kc.shape, F32)
    v_s = jnp.zeros(vc.shape, F32)
    s_s = jnp.zeros(state_hgrn.shape, F32)

    in_tiles, out_tm, ffn_tiles = (1024, 1280), 512, (1024, 512)
    attn_rows, attn_rows_last, hgrn_rows = 4 * WINDOW, 8 * WINDOW, 8 * HG_CHUNK
    s_in_tn_f32, s_in_tn, s_out_tn_f32, s_out_tn, s_ffn_tf_f32, s_ffn_tf = 1024, 1024, 1024, 1024, 512, 512

    def gate_up_blocks(w_gate, w_up, tf):
        return (w_gate, 0, w_up, 0 if w_up is not w_gate else D_FF // tf)

    outs = {name: [] for name in ("kp", "vp", "sp", "cp", "cs")}
    for l in range(DEPTH):
        if l == 0:
            proj_s, w_in_b = in_proj_sample(xs, g_mix, w_in, l, s_in_tn_f32)
        else:
            proj_s = in_proj(xs, g_mix, w_in_b, l, ms, s_in_tn)
        a_s, k_s, v_s, o_s, s_s, c_s, cst_s = mix_sample(
            proj_s, kc, vc, k_s, v_s, attn_sink, tab_s, state_hgrn, s_s, hgrn_lower_bounds, hgrn_norm,
            state_conv, conv_w, l)
        if l == 0:
            xs, w_out_b = out_proj_sample(xs, a_s, o_s, c_s, w_out, l, s_out_tn_f32)
            xs, w_gate_b, w_up_b, w_down_b = ffn_sample(xs, g_ffn, w_gate_up, w_down, g_final, l, s_ffn_tf_f32)
        else:
            xs = out_proj_sample(xs, a_s, o_s, c_s, w_out_b, l, s_out_tn)
            xs = ffn(xs, g_ffn, *gate_up_blocks(w_gate_b, w_up_b, s_ffn_tf), w_down_b, g_final, l, ms, s_ffn_tf)

        more = l < DEPTH - 1
        proj_p = in_proj(xp, g_mix, w_in_b, l, *in_tiles)
        proj_p3 = proj_p.reshape(bp, lp, IN_WIDTH)
        a_p, k_p, v_p, *cast_a = attn_prompt(proj_p3, attn_sink, tab_p, tab_pt, l,
                                             attn_rows if more else attn_rows_last,
                                             convert=(w_in, w_out) if more else ())
        o_p, s_p, *cast_h = hgrn_prompt(proj_p3, hgrn_lower_bounds, hgrn_norm, l, hgrn_rows,
                                        convert=(w_down,) if more else ())
        xp, cst_p = out_proj(xp, a_p.reshape(mp, ATTN_WIDTH), o_p.reshape(mp, HG_WIDTH), proj_p,
                             conv_w, w_out_b, l, lp, out_tm)
        gate_up = gate_up_blocks(w_gate_b, w_up_b, ffn_tiles[1])
        if more:
            xp, w_gu_b = ffn(xp, g_ffn, *gate_up, w_down_b, g_final, l, *ffn_tiles, convert=w_gate_up)
            (w_in_b, w_out_b), (w_down_b,) = cast_a, cast_h
            w_gate_b = w_up_b = w_gu_b
        else:
            xp = ffn(xp, g_ffn, *gate_up, w_down_b, g_final, l, *ffn_tiles)

        outs["kp"].append(k_p.reshape(bp, WINDOW, N_KV_HEADS, HEAD_DIM))
        outs["vp"].append(v_p.reshape(bp, WINDOW, N_KV_HEADS, HEAD_DIM))
        outs["sp"].append(s_p)
        outs["cp"].append(cst_p)
        outs["cs"].append(cst_s)

    st = lambda name: jnp.stack(outs[name])
    kv_shape = (DEPTH, bs, lb, N_KV_HEADS, HEAD_DIM)
    return (xp.reshape(bp, lp, d), xs.reshape(bs, ls, d), st("kp"), st("vp"), st("sp"), st("cp"),
            k_s.reshape(kv_shape), v_s.reshape(kv_shape), s_s, st("cs"))
```
